```python
import math
import jax, jax.numpy as jnp
from jax import lax
import numpy as np

D_MODEL = 1024
BATCH = 4
SEQ = 8192
DEPTH = 1

N_HEADS = 8
HEAD_DIM = D_MODEL // N_HEADS
ATTN_WIDTH = N_HEADS * HEAD_DIM
MOBA_BLOCK = 256
MOBA_TOPK = 3
Q_CHUNK = 32
ROPE_THETA = 10000.0
SSM_WIDTH = D_MODEL // 2
SSM_GROUP = 16
SSM_GROUPS = SSM_WIDTH // SSM_GROUP
SSM_STATE = 64
DT_MIN = 1e-3
DT_MAX = 1e-1
N_BRANCHES = 2
D_FF = 4 * D_MODEL
RMS_EPS = 1e-6
NEG_BIG = -1e30
IN_WIDTH = 3 * ATTN_WIDTH + SSM_WIDTH + N_BRANCHES * D_MODEL

kernel_name = "hybrid_moba_s5_gated_block"


def rms_norm(x, g):
    xf = x.astype(jnp.float32)
    y = xf * lax.rsqrt(jnp.mean(xf * xf, axis=-1, keepdims=True) + RMS_EPS)
    return (y * g.astype(jnp.float32)).astype(x.dtype)


def rotary(x, pos):
    half = HEAD_DIM // 2
    inv_freq = ROPE_THETA ** (-jnp.arange(half, dtype=jnp.float32) / half)
    ang = pos.astype(jnp.float32)[:, None] * inv_freq[None, :]
    cos = jnp.cos(ang)[None, :, None, :]
    sin = jnp.sin(ang)[None, :, None, :]
    xf = x.astype(jnp.float32)
    x1, x2 = xf[..., :half], xf[..., half:]
    out = jnp.concatenate([x1 * cos - x2 * sin, x2 * cos + x1 * sin], axis=-1)
    return out.astype(x.dtype)


def moba_attention(q, k, v):
    b, l, h, d = q.shape
    lp = -(-l // MOBA_BLOCK) * MOBA_BLOCK
    nb = lp // MOBA_BLOCK
    pad = ((0, 0), (0, lp - l), (0, 0), (0, 0))
    q, k, v = [jnp.pad(t, pad).transpose(0, 2, 1, 3) for t in (q, k, v)]
    kb = k.reshape(b, h, nb, MOBA_BLOCK, d)
    vb = v.reshape(b, h, nb, MOBA_BLOCK, d)
    k_mean = jnp.mean(kb.astype(jnp.float32), axis=3)
    gate = jnp.einsum('bhqd,bhnd->bhqn', q.astype(jnp.float32), k_mean)
    pos = jnp.arange(lp)
    q_blk = pos // MOBA_BLOCK
    past = jnp.arange(nb)[None, :] < q_blk[:, None]
    gate = jnp.where(past, gate, NEG_BIG)
    topk = min(MOBA_TOPK, nb)
    _, sel = lax.top_k(gate, topk)
    sel_valid = sel < q_blk[:, None]
    own = jnp.broadcast_to(q_blk[:, None].astype(sel.dtype), (b, h, lp, 1))
    idx = jnp.concatenate([sel, own], axis=-1)
    valid = jnp.concatenate([sel_valid, jnp.ones((b, h, lp, 1), dtype=bool)], axis=-1)

    nc = lp // Q_CHUNK

    def to_chunks(t):
        return jnp.moveaxis(t.reshape((b, h, nc, Q_CHUNK) + t.shape[3:]), 2, 0)

    bi = jnp.arange(b)[:, None, None, None]
    hi = jnp.arange(h)[None, :, None, None]
    key_off = jnp.arange(MOBA_BLOCK)
    scale = HEAD_DIM ** -0.5

    def attend(args):
        qc, ic, vc, pc = args
        kg = kb[bi, hi, ic]
        vg = vb[bi, hi, ic]
        s = jnp.einsum('bhqd,bhqskd->bhqsk', qc, kg,
                       preferred_element_type=jnp.float32) * scale
        kpos = ic[..., None] * MOBA_BLOCK + key_off
        mask = vc[..., None] & (kpos <= pc[None, None, :, None, None])
        s = jnp.where(mask, s, NEG_BIG)
        p = jax.nn.softmax(s.reshape(b, h, Q_CHUNK, -1), axis=-1).reshape(s.shape)
        return jnp.einsum('bhqsk,bhqskd->bhqd', p.astype(vg.dtype), vg)

    out = lax.map(attend, (to_chunks(q), to_chunks(idx), to_chunks(valid),
                           pos.reshape(nc, Q_CHUNK)))
    out = jnp.moveaxis(out, 0, 2).reshape(b, h, lp, d)[:, :, :l]
    return out.transpose(0, 2, 1, 3).reshape(b, l, h * d)


def s5_branch(u, lam_re, lam_im, log_step, b_re, b_im, c_re, c_im, d_skip, w_glu):
    b, l, _ = u.shape
    f32 = jnp.float32
    uf = u.astype(f32).reshape(b, l, SSM_GROUPS, SSM_GROUP)
    step = jnp.exp(log_step.astype(f32))[:, None]
    lr, li = lam_re.astype(f32), lam_im.astype(f32)
    mag = jnp.exp(lr * step)
    ar = mag * jnp.cos(li * step)
    ai = mag * jnp.sin(li * step)
    den = lr * lr + li * li
    nr, ni = ar - 1.0, ai
    cr = (nr * lr + ni * li) / den
    ci = (ni * lr - nr * li) / den
    bu_r = jnp.einsum('blgc,gnc->blgn', uf, b_re.astype(f32))
    bu_i = jnp.einsum('blgc,gnc->blgn', uf, b_im.astype(f32))
    xr0 = cr * bu_r - ci * bu_i
    xi0 = cr * bu_i + ci * bu_r
    a_r = jnp.broadcast_to(ar[None, None], (1, l, SSM_GROUPS, SSM_STATE))
    a_i = jnp.broadcast_to(ai[None, None], (1, l, SSM_GROUPS, SSM_STATE))

    def combine(e1, e2):
        a1r, a1i, b1r, b1i = e1
        a2r, a2i, b2r, b2i = e2
        return (a2r * a1r - a2i * a1i,
                a2r * a1i + a2i * a1r,
                a2r * b1r - a2i * b1i + b2r,
                a2r * b1i + a2i * b1r + b2i)

    _, _, sr, si = lax.associative_scan(combine, (a_r, a_i, xr0, xi0), axis=1)
    y = (jnp.einsum('blgn,gcn->blgc', sr, c_re.astype(f32))
         - jnp.einsum('blgn,gcn->blgc', si, c_im.astype(f32))
         + d_skip.astype(f32) * uf)
    y = jax.nn.gelu(y.reshape(b, l, SSM_WIDTH)).astype(u.dtype)
    hgl = y @ w_glu
    val, gt = jnp.split(hgl, 2, axis=-1)
    return val * jax.nn.sigmoid(gt)


def setup_inputs(seed: int = 0) -> dict:
    key = jax.random.key(seed)
    ks = jax.random.split(key, 20)
    f32 = jnp.float32
    nrm = lambda k, s, sc: jax.random.normal(k, s, f32) * sc
    x = jax.random.normal(ks[0], (BATCH, SEQ, D_MODEL), f32)
    norm_mix_g = 1.0 + nrm(ks[1], (DEPTH, D_MODEL), 0.02)
    w_in = nrm(ks[2], (DEPTH, D_MODEL, IN_WIDTH), D_MODEL ** -0.5)
    n_idx = jnp.arange(SSM_STATE, dtype=f32)
    lam_re = -0.5 + nrm(ks[3], (DEPTH, SSM_GROUPS, SSM_STATE), 0.01)
    lam_im = jnp.broadcast_to(math.pi * n_idx, (DEPTH, SSM_GROUPS, SSM_STATE)) \
        + nrm(ks[4], (DEPTH, SSM_GROUPS, SSM_STATE), 0.01)
    log_step = jax.random.uniform(ks[5], (DEPTH, SSM_GROUPS), f32,
                                  math.log(DT_MIN), math.log(DT_MAX))
    b_sc = (2.0 * SSM_GROUP) ** -0.5
    b_re = nrm(ks[6], (DEPTH, SSM_GROUPS, SSM_STATE, SSM_GROUP), b_sc)
    b_im = nrm(ks[7], (DEPTH, SSM_GROUPS, SSM_STATE, SSM_GROUP), b_sc)
    c_sc = (2.0 * SSM_STATE) ** -0.5
    c_re = nrm(ks[8], (DEPTH, SSM_GROUPS, SSM_GROUP, SSM_STATE), c_sc)
    c_im = nrm(ks[9], (DEPTH, SSM_GROUPS, SSM_GROUP, SSM_STATE), c_sc)
    d_skip = nrm(ks[10], (DEPTH, SSM_GROUPS, SSM_GROUP), 1.0)
    w_glu = nrm(ks[11], (DEPTH, SSM_WIDTH, 2 * D_MODEL), SSM_WIDTH ** -0.5)
    w_out = nrm(ks[12], (DEPTH, D_MODEL, D_MODEL), D_MODEL ** -0.5)
    norm_mlp_g = 1.0 + nrm(ks[13], (DEPTH, D_MODEL), 0.02)
    w_up = nrm(ks[14], (DEPTH, D_MODEL, D_FF), D_MODEL ** -0.5)
    w_down = nrm(ks[15], (DEPTH, D_FF, D_MODEL), D_FF ** -0.5)
    norm_final_g = 1.0 + nrm(ks[16], (D_MODEL,), 0.02)
    return {"x": x, "norm_mix_g": norm_mix_g, "w_in": w_in, "lam_re": lam_re,
            "lam_im": lam_im, "log_step": log_step, "b_re": b_re, "b_im": b_im,
            "c_re": c_re, "c_im": c_im, "d_skip": d_skip, "w_glu": w_glu,
            "w_out": w_out, "norm_mlp_g": norm_mlp_g, "w_up": w_up,
            "w_down": w_down, "norm_final_g": norm_final_g}


def reference(x, norm_mix_g, w_in, lam_re, lam_im, log_step, b_re, b_im, c_re, c_im,
              d_skip, w_glu, w_out, norm_mlp_g, w_up, w_down, norm_final_g):
    b, l, _ = x.shape
    pos = jnp.arange(l)
    splits = [ATTN_WIDTH, 2 * ATTN_WIDTH, 3 * ATTN_WIDTH, 3 * ATTN_WIDTH + SSM_WIDTH]
    for i in range(DEPTH):
        h = rms_norm(x, norm_mix_g[i])
        proj = h @ w_in[i]
        q, k, v, u, g = jnp.split(proj, splits, axis=-1)
        q = rotary(q.reshape(b, l, N_HEADS, HEAD_DIM), pos)
        k = rotary(k.reshape(b, l, N_HEADS, HEAD_DIM), pos)
        v = v.reshape(b, l, N_HEADS, HEAD_DIM)
        o_a = moba_attention(q, k, v)
        o_b = s5_branch(u, lam_re[i], lam_im[i], log_step[i], b_re[i], b_im[i],
                        c_re[i], c_im[i], d_skip[i], w_glu[i])
        g_a, g_b = jnp.split(g, 2, axis=-1)
        mixed = jax.nn.sigmoid(g_a) * o_a + jax.nn.sigmoid(g_b) * o_b
        x = x + mixed @ w_out[i]
        h = rms_norm(x, norm_mlp_g[i])
        x = x + jnp.square(jax.nn.relu(h @ w_up[i])) @ w_down[i]
    return rms_norm(x, norm_final_g)
```

```python
import functools
import math

import jax
import jax.numpy as jnp
from jax import lax
from jax.experimental import pallas as pl
from jax.experimental.pallas import tpu as pltpu

F32 = jnp.float32
BF16 = jnp.bfloat16

N_HEADS = 8
HEAD_DIM = 128
HALF_DIM = HEAD_DIM // 2
MOBA_BLOCK = 256
MOBA_TOPK = 3
ROPE_THETA = 10000.0
SSM_GROUP = 16
SSM_STATE = 64
SSM_CHUNK = 16
CHUNK_W = SSM_CHUNK * SSM_GROUP
SCAN_ROWS = 8
RMS_EPS = 1e-6
NEG_BIG = -1e30
TOKEN_TILE = 512
VMEM_LIMIT = 56 * 1024 * 1024

_NT = (((1,), (1,)), ((), ()))


def _cparams(*sem):
    return pltpu.CompilerParams(dimension_semantics=sem, vmem_limit_bytes=VMEM_LIMIT)


def _const_spec(shape):
    nd = len(shape)
    return pl.BlockSpec(shape, lambda *_: (0,) * nd)


def _cexp(lr, li, step, e):
    mag = jnp.exp(lr * step * e)
    ang = li * step * e
    return mag * jnp.cos(ang), mag * jnp.sin(ang)


def _ssm_prep_kernel(lrr_ref, lir_ref, lrc_ref, lic_ref, ls_ref, btr_ref, bti_ref, ctr_ref, cti_ref,
                     mt_ref, gre_ref, gim_ref, hre_ref, him_ref, apr_ref, api_ref):
    step = jnp.exp(ls_ref[0])
    lr_r, li_r = lrr_ref[0], lir_ref[0]
    lr_c, li_c = lrc_ref[0], lic_ref[0]
    ar, ai = _cexp(lr_r, li_r, step, 1.0)
    den = lr_r * lr_r + li_r * li_r
    nr, ni = ar - 1.0, ai
    cr = (nr * lr_r + ni * li_r) / den
    ci = (ni * lr_r - nr * li_r) / den
    btr, bti = btr_ref[0], bti_ref[0]
    bbr = cr * btr - ci * bti
    bbi = cr * bti + ci * btr
    s_col = lax.shift_right_logical(lax.broadcasted_iota(jnp.int32, (CHUNK_W, 1), 0), 4)
    e_col = (SSM_CHUNK - 1 - s_col).astype(F32)
    pr, pi_ = _cexp(lr_r, li_r, step, e_col)
    gre_ref[0] = pr * bbr - pi_ * bbi
    gim_ref[0] = pr * bbi + pi_ * bbr
    tau = lax.shift_right_logical(lax.broadcasted_iota(jnp.int32, (1, CHUNK_W), 1), 4).astype(F32)
    ctr, cti = ctr_ref[0], cti_ref[0]
    wr, wi = _cexp(lr_c, li_c, step, tau)
    p_r = ctr * wr - cti * wi
    p_i = ctr * wi + cti * wr
    hp = lax.Precision.HIGHEST
    r0 = (jnp.dot(bbr[:SSM_GROUP], p_r, precision=hp, preferred_element_type=F32)
          - jnp.dot(bbi[:SSM_GROUP], p_i, precision=hp, preferred_element_type=F32))
    lane = lax.broadcasted_iota(jnp.int32, (SSM_GROUP, CHUNK_W), 1)
    for s in range(SSM_CHUNK):
        blk = r0 if s == 0 else pltpu.roll(r0, SSM_GROUP * s, axis=1)
        blk = jnp.where(lane >= SSM_GROUP * s, blk, 0.0)
        mt_ref[0, s * SSM_GROUP:(s + 1) * SSM_GROUP, :] = blk.astype(mt_ref.dtype)
    hr, hi = _cexp(lr_c, li_c, step, tau + 1.0)
    hre_ref[0] = ctr * hr - cti * hi
    him_ref[0] = -(ctr * hi + cti * hr)
    e_row = ((lax.broadcasted_iota(jnp.int32, (SCAN_ROWS, 1), 0) + 1) * SSM_CHUNK).astype(F32)
    qr, qi = _cexp(lr_r, li_r, step, e_row)
    apr_ref[0] = qr
    api_ref[0] = qi


def _ssm_prep(lam_re, lam_im, log_step, b_re, b_im, c_re, c_im):
    g, n = lam_re.shape
    c = b_re.shape[-1]
    bt = lambda b: jnp.tile(jnp.swapaxes(b, 1, 2), (1, SSM_CHUNK, 1))
    ct = lambda m: jnp.tile(jnp.swapaxes(m, 1, 2), (1, 1, SSM_CHUNK))
    args = (lam_re.reshape(g, 1, n), lam_im.reshape(g, 1, n), lam_re.reshape(g, n, 1),
            lam_im.reshape(g, n, 1), log_step.reshape(g, 1, 1), bt(b_re), bt(b_im), ct(c_re), ct(c_im))
    spec = lambda shp: pl.BlockSpec((1,) + shp, lambda i: (i, 0, 0))
    in_specs = [spec((1, n)), spec((1, n)), spec((n, 1)), spec((n, 1)), spec((1, 1)),
                spec((CHUNK_W, n)), spec((CHUNK_W, n)), spec((n, CHUNK_W)), spec((n, CHUNK_W))]
    out_shapes = [((CHUNK_W, CHUNK_W), BF16), ((CHUNK_W, n), F32), ((CHUNK_W, n), F32),
                  ((n, CHUNK_W), F32), ((n, CHUNK_W), F32), ((SCAN_ROWS, n), F32), ((SCAN_ROWS, n), F32)]
    return pl.pallas_call(
        _ssm_prep_kernel,
        grid=(g,),
        in_specs=in_specs,
        out_specs=[spec(s) for s, _ in out_shapes],
        out_shape=[jax.ShapeDtypeStruct((g,) + s, d) for s, d in out_shapes],
        compiler_params=_cparams("parallel"),
        name="ssm_prep",
    )(*args)


def _in_proj_kernel(x_ref, g_ref, wqt_ref, wk_ref, wvt_ref, wu_ref, wg_ref,
                    cos_ref, sin_ref, cost_ref, sint_ref,
                    qt_ref, k_ref, vt_ref, u_ref, sga_ref, sgb_ref):
    x = x_ref[0]
    d = x.shape[-1]
    ms = jnp.mean(x * x, axis=-1, keepdims=True)
    h = (x * lax.rsqrt(ms + RMS_EPS) * g_ref[...]).astype(BF16)
    cos, sin = cos_ref[...], sin_ref[...]
    for c0 in range(0, d, 512):
        p = jnp.dot(h, wk_ref[:, c0:c0 + 512], preferred_element_type=F32)
        for hh in range(512 // HEAD_DIM):
            sl = p[:, hh * HEAD_DIM:(hh + 1) * HEAD_DIM]
            r = sl * cos + pltpu.roll(sl, HALF_DIM, axis=1) * sin
            k_ref[0, :, c0 + hh * HEAD_DIM:c0 + (hh + 1) * HEAD_DIM] = r.astype(BF16)
    cost, sint = cost_ref[...], sint_ref[...]
    scale = HEAD_DIM ** -0.5
    for c0 in range(0, d, 256):
        pt = lax.dot_general(wqt_ref[c0:c0 + 256, :], h, _NT, preferred_element_type=F32)
        for hh in range(256 // HEAD_DIM):
            r0 = hh * HEAD_DIM
            x1 = pt[r0:r0 + HALF_DIM]
            x2 = pt[r0 + HALF_DIM:r0 + HEAD_DIM]
            qt_ref[0, c0 + r0:c0 + r0 + HALF_DIM, :] = ((x1 * cost - x2 * sint) * scale).astype(BF16)
            qt_ref[0, c0 + r0 + HALF_DIM:c0 + r0 + HEAD_DIM, :] = ((x2 * cost + x1 * sint) * scale).astype(BF16)
    for c0 in range(0, d, 256):
        vt_ref[0, c0:c0 + 256, :] = lax.dot_general(
            wvt_ref[c0:c0 + 256, :], h, _NT, preferred_element_type=F32).astype(BF16)
    u_ref[0] = jnp.dot(h, wu_ref[...], preferred_element_type=F32).astype(BF16)
    for c0 in range(0, 2 * d, 512):
        sg = jax.nn.sigmoid(jnp.dot(h, wg_ref[:, c0:c0 + 512], preferred_element_type=F32)).astype(BF16)
        if c0 < d:
            sga_ref[0, :, c0:c0 + 512] = sg
        else:
            sgb_ref[0, :, c0 - d:c0 - d + 512] = sg


def _in_proj(x, g, w_in):
    b, l, d = x.shape
    tm = TOKEN_TILE
    sw = w_in.shape[1] - 5 * d
    wqt = w_in[:, :d].T.astype(BF16)
    wk = w_in[:, d:2 * d].astype(BF16)
    wvt = w_in[:, 2 * d:3 * d].T.astype(BF16)
    wu = w_in[:, 3 * d:3 * d + sw].astype(BF16)
    wg = w_in[:, 3 * d + sw:].astype(BF16)
    inv_freq = ROPE_THETA ** (-jnp.arange(HALF_DIM, dtype=F32) / HALF_DIM)
    ang = jnp.arange(l).astype(F32)[:, None] * inv_freq[None, :]
    cos, sin = jnp.cos(ang), jnp.sin(ang)
    cos2 = jnp.concatenate([cos, cos], axis=1)
    sin2 = jnp.concatenate([-sin, sin], axis=1)
    tok = lambda w: pl.BlockSpec((1, tm, w), lambda bi, i: (bi, i, 0))
    feat = lambda w: pl.BlockSpec((1, w, tm), lambda bi, i: (bi, 0, i))
    sds = jax.ShapeDtypeStruct
    return pl.pallas_call(
        _in_proj_kernel,
        grid=(b, l // tm),
        in_specs=[tok(d), _const_spec((1, d)), _const_spec((d, d)), _const_spec((d, d)), _const_spec((d, d)),
                  _const_spec((d, sw)), _const_spec((d, 2 * d)),
                  pl.BlockSpec((tm, HEAD_DIM), lambda bi, i: (i, 0)),
                  pl.BlockSpec((tm, HEAD_DIM), lambda bi, i: (i, 0)),
                  pl.BlockSpec((HALF_DIM, tm), lambda bi, i: (0, i)),
                  pl.BlockSpec((HALF_DIM, tm), lambda bi, i: (0, i))],
        out_specs=[feat(d), tok(d), feat(d), tok(sw), tok(d), tok(d)],
        out_shape=[sds((b, d, l), BF16), sds((b, l, d), BF16), sds((b, d, l), BF16),
                   sds((b, l, sw), BF16), sds((b, l, d), BF16), sds((b, l, d), BF16)],
        compiler_params=_cparams("parallel", "parallel"),
        name="in_proj",
    )(x, g.reshape(1, d), wqt, wk, wvt, wu, wg, cos2, sin2, cos.T, sin.T)


def _ssm_chunk_kernel(u_ref, mt_ref, gre_ref, gim_ref, hre_ref, him_ref, apr_ref, api_ref, d_ref,
                      y_ref, ere_ref, eim_ref, xre_ref, xim_ref):
    nk = u_ref.shape[1]
    u0, u1 = u_ref[0], u_ref[1]
    dot = functools.partial(jnp.dot, preferred_element_type=F32)
    ere_ref[...] = dot(u0, gre_ref[0]) + dot(u1, gre_ref[1])
    eim_ref[...] = dot(u0, gim_ref[0]) + dot(u1, gim_ref[1])
    apr, api = apr_ref[0], api_ref[0]
    row = lax.broadcasted_iota(jnp.int32, apr.shape, 0)
    lvl = [(dd, jnp.broadcast_to(apr[dd - 1:dd], apr.shape), jnp.broadcast_to(api[dd - 1:dd], apr.shape))
           for dd in (1, 2, 4)]

    def tile(t, carry):
        cr, ci = carry
        r0 = pl.multiple_of(t * SCAN_ROWS, SCAN_ROWS)
        zr = ere_ref[pl.ds(r0, SCAN_ROWS), :]
        zi = eim_ref[pl.ds(r0, SCAN_ROWS), :]
        for dd, adr, adi in lvl:
            sr = jnp.where(row >= dd, pltpu.roll(zr, dd, axis=0), 0.0)
            si = jnp.where(row >= dd, pltpu.roll(zi, dd, axis=0), 0.0)
            zr, zi = zr + (adr * sr - adi * si), zi + (adr * si + adi * sr)
        sr = apr * cr - api * ci + zr
        si = apr * ci + api * cr + zi
        xre_ref[pl.ds(r0, SCAN_ROWS), :] = jnp.where(row >= 1, pltpu.roll(sr, 1, axis=0), cr)
        xim_ref[pl.ds(r0, SCAN_ROWS), :] = jnp.where(row >= 1, pltpu.roll(si, 1, axis=0), ci)
        last = SCAN_ROWS - 1
        return (jnp.broadcast_to(sr[last:last + 1], sr.shape), jnp.broadcast_to(si[last:last + 1], si.shape))

    zero = jnp.zeros(apr.shape, F32)
    lax.fori_loop(0, nk // SCAN_ROWS, tile, (zero, zero))
    xre = xre_ref[...].astype(BF16)
    xim = xim_ref[...].astype(BF16)
    for gi, ug in enumerate((u0, u1)):
        y = (dot(ug, mt_ref[gi]) + dot(xre, hre_ref[gi]) + dot(xim, him_ref[gi])
             + d_ref[gi] * ug.astype(F32))
        y_ref[gi] = jax.nn.gelu(y).astype(y_ref.dtype)


def _ssm_chunk(ut, mt, gre, gim, hre, him, apr, api, d_skip, batch):
    g, rows, _ = ut.shape
    nk = rows // batch
    n = gre.shape[-1]
    gp = g // 2
    zg = jnp.zeros_like(gre)
    zh = jnp.zeros_like(hre)
    even = (jnp.arange(g) % 2 == 0)[:, None, None]
    pad_g = lambda m: jnp.where(even, jnp.concatenate([m, zg], 2), jnp.concatenate([zg, m], 2)).astype(BF16)
    pad_h = lambda m: jnp.where(even, jnp.concatenate([m, zh], 1), jnp.concatenate([zh, m], 1)).astype(BF16)
    pair = lambda m: m.reshape(gp, 2, SCAN_ROWS, n).transpose(0, 2, 1, 3).reshape(gp, SCAN_ROWS, 2 * n)
    d_t = jnp.tile(d_skip, (1, SSM_CHUNK)).reshape(g, 1, CHUNK_W).astype(F32)
    two = lambda r, c: pl.BlockSpec((2, r, c), lambda p, bi: (p, 0, 0))
    return pl.pallas_call(
        _ssm_chunk_kernel,
        grid=(gp, batch),
        in_specs=[pl.BlockSpec((2, nk, CHUNK_W), lambda p, bi: (p, bi, 0)),
                  two(CHUNK_W, CHUNK_W), two(CHUNK_W, 2 * n), two(CHUNK_W, 2 * n),
                  two(2 * n, CHUNK_W), two(2 * n, CHUNK_W),
                  pl.BlockSpec((1, SCAN_ROWS, 2 * n), lambda p, bi: (p, 0, 0)),
                  pl.BlockSpec((1, SCAN_ROWS, 2 * n), lambda p, bi: (p, 0, 0)),
                  two(1, CHUNK_W)],
        out_specs=pl.BlockSpec((2, nk, CHUNK_W), lambda p, bi: (p, bi, 0)),
        out_shape=jax.ShapeDtypeStruct(ut.shape, BF16),
        scratch_shapes=[pltpu.VMEM((nk, 2 * n), F32) for _ in range(4)],
        compiler_params=_cparams("parallel", "parallel"),
        name="ssm_chunk",
    )(ut, mt, pad_g(gre), pad_g(gim), pad_h(hre), pad_h(him), pair(apr), pair(api), d_t)


def _moba_kernel(qt_ref, k_ref, vt_ref, o_ref, kmean_ref, sel_ref, m_ref, l_ref, acc_ref):
    i = pl.program_id(2)
    nb = k_ref.shape[1] // MOBA_BLOCK
    blk = MOBA_BLOCK

    @pl.when(i == 0)
    def _():
        for j in range(nb):
            kb = k_ref[0, j * blk:(j + 1) * blk, :].astype(F32)
            kmean_ref[j:j + 1, :] = jnp.mean(kb, axis=0, keepdims=True)

    qt = qt_ref[0]
    gate = jnp.dot(kmean_ref[...], qt.astype(F32), precision=lax.Precision.HIGHEST,
                   preferred_element_type=F32)
    brow = lax.broadcasted_iota(jnp.int32, gate.shape, 0)
    work = jnp.where(brow < i, gate, -jnp.inf)
    sel = jnp.zeros(gate.shape, F32)
    for _ in range(MOBA_TOPK):
        mx = jnp.max(work, axis=0, keepdims=True)
        idx = jnp.min(jnp.where(work == mx, brow, nb), axis=0, keepdims=True)
        hit = brow == idx
        sel = jnp.where(hit, jnp.where(idx < i, 1.0, 0.0), sel)
        work = jnp.where(hit, -jnp.inf, work)
    sel_ref[...] = sel

    d0 = pl.multiple_of(i * blk, blk)
    st = jnp.dot(k_ref[0, pl.ds(d0, blk), :], qt, preferred_element_type=F32)
    krow = lax.broadcasted_iota(jnp.int32, st.shape, 0)
    qcol = lax.broadcasted_iota(jnp.int32, st.shape, 1)
    st = jnp.where(krow <= qcol, st, NEG_BIG)
    m0 = jnp.max(st, axis=0, keepdims=True)
    p = jnp.exp(st - m0)
    m_ref[...] = m0
    l_ref[...] = jnp.sum(p, axis=0, keepdims=True)
    acc_ref[...] = jnp.dot(vt_ref[0, :, pl.ds(d0, blk)], p.astype(BF16), preferred_element_type=F32)

    def past(j, _):
        s_row = sel_ref[pl.ds(j, 1), :]

        @pl.when(jnp.max(s_row) > 0.0)
        def _():
            j0 = pl.multiple_of(j * blk, blk)
            s = jnp.dot(k_ref[0, pl.ds(j0, blk), :], qt, preferred_element_type=F32)
            s = jnp.where(s_row > 0.0, s, NEG_BIG)
            m_old = m_ref[...]
            m_new = jnp.maximum(m_old, jnp.max(s, axis=0, keepdims=True))
            alpha = jnp.exp(m_old - m_new)
            pj = jnp.exp(s - m_new)
            m_ref[...] = m_new
            l_ref[...] = alpha * l_ref[...] + jnp.sum(pj, axis=0, keepdims=True)
            acc_ref[...] = alpha * acc_ref[...] + jnp.dot(
                vt_ref[0, :, pl.ds(j0, blk)], pj.astype(BF16), preferred_element_type=F32)
        return 0

    lax.fori_loop(0, i, past, 0)
    ot = acc_ref[...] / l_ref[...]
    o_ref[0] = ot.T.astype(o_ref.dtype)


def _moba_attention(qt, k, vt):
    b, l, d = k.shape
    nb = l // MOBA_BLOCK
    return pl.pallas_call(
        _moba_kernel,
        grid=(b, N_HEADS, nb),
        in_specs=[pl.BlockSpec((1, HEAD_DIM, MOBA_BLOCK), lambda bi, h, i: (bi, h, i)),
                  pl.BlockSpec((1, l, HEAD_DIM), lambda bi, h, i: (bi, 0, h)),
                  pl.BlockSpec((1, HEAD_DIM, l), lambda bi, h, i: (bi, h, 0))],
        out_specs=pl.BlockSpec((1, MOBA_BLOCK, HEAD_DIM), lambda bi, h, i: (bi, i, h)),
        out_shape=jax.ShapeDtypeStruct((b, l, d), BF16),
        scratch_shapes=[pltpu.VMEM((nb, HEAD_DIM), F32), pltpu.VMEM((nb, MOBA_BLOCK), F32),
                        pltpu.VMEM((1, MOBA_BLOCK), F32), pltpu.VMEM((1, MOBA_BLOCK), F32),
                        pltpu.VMEM((HEAD_DIM, MOBA_BLOCK), F32)],
        compiler_params=_cparams("parallel", "parallel", "arbitrary"),
        name="moba_attn",
    )(qt, k, vt)


def _merge_kernel(x_ref, oa_ref, sga_ref, sgb_ref, y_ref, wglu_ref, wout_ref, o_ref):
    d = x_ref.shape[-1]
    hgl = jnp.dot(y_ref[0], wglu_ref[...], preferred_element_type=F32)
    ob = hgl[:, :d] * jax.nn.sigmoid(hgl[:, d:])
    mixed = sga_ref[0].astype(F32) * oa_ref[0].astype(F32) + sgb_ref[0].astype(F32) * ob
    o_ref[0] = x_ref[0] + jnp.dot(mixed.astype(BF16), wout_ref[...], preferred_element_type=F32)


def _merge_out(x, oa, sga, sgb, y, w_glu, w_out):
    b, l, d = x.shape
    tm = TOKEN_TILE
    sw = y.shape[-1]
    tok = lambda w: pl.BlockSpec((1, tm, w), lambda bi, i: (bi, i, 0))
    return pl.pallas_call(
        _merge_kernel,
        grid=(b, l // tm),
        in_specs=[tok(d), tok(d), tok(d), tok(d), tok(sw), _const_spec((sw, 2 * d)), _const_spec((d, d))],
        out_specs=tok(d),
        out_shape=jax.ShapeDtypeStruct((b, l, d), F32),
        compiler_params=_cparams("parallel", "parallel"),
        name="merge_out",
    )(x, oa, sga, sgb, y, w_glu.astype(BF16), w_out.astype(BF16))


def _mlp_kernel(x_ref, g1_ref, wup_ref, wdn_ref, g2_ref, o_ref, *, final_norm):
    x = x_ref[0]
    d = x.shape[-1]
    ff = wup_ref.shape[1]
    ms = jnp.mean(x * x, axis=-1, keepdims=True)
    h = (x * lax.rsqrt(ms + RMS_EPS) * g1_ref[...]).astype(BF16)
    acc = x
    for c0 in range(0, ff, d):
        a = jnp.maximum(jnp.dot(h, wup_ref[:, c0:c0 + d], preferred_element_type=F32), 0.0)
        acc = acc + jnp.dot((a * a).astype(BF16), wdn_ref[c0:c0 + d, :], preferred_element_type=F32)
    if final_norm:
        ms2 = jnp.mean(acc * acc, axis=-1, keepdims=True)
        acc = acc * lax.rsqrt(ms2 + RMS_EPS) * g2_ref[...]
    o_ref[0] = acc


def _mlp_final(x, g1, w_up, w_down, g2, final_norm):
    b, l, d = x.shape
    tm = TOKEN_TILE
    ff = w_up.shape[1]
    tok = pl.BlockSpec((1, tm, d), lambda bi, i: (bi, i, 0))
    return pl.pallas_call(
        functools.partial(_mlp_kernel, final_norm=final_norm),
        grid=(b, l // tm),
        in_specs=[tok, _const_spec((1, d)), _const_spec((d, ff)), _const_spec((ff, d)), _const_spec((1, d))],
        out_specs=tok,
        out_shape=jax.ShapeDtypeStruct((b, l, d), F32),
        compiler_params=_cparams("parallel", "parallel"),
        name="mlp_final",
    )(x, g1.reshape(1, d), w_up.astype(BF16), w_down.astype(BF16), g2.reshape(1, d))


def kernel(x, norm_mix_g, w_in, lam_re, lam_im, log_step, b_re, b_im, c_re, c_im, d_skip, w_glu, w_out,
           norm_mlp_g, w_up, w_down, norm_final_g):
    b, l, d = x.shape
    depth = w_in.shape[0]
    assert d == N_HEADS * HEAD_DIM and l % TOKEN_TILE == 0 and l % (SSM_CHUNK * SCAN_ROWS) == 0
    for i in range(depth):
        g = lam_re.shape[1]
        sw = g * SSM_GROUP
        nk = l // SSM_CHUNK
        mt, gre, gim, hre, him, apr, api = _ssm_prep(
            lam_re[i], lam_im[i], log_step[i], b_re[i], b_im[i], c_re[i], c_im[i])
        qt, k, vt, u, sga, sgb = _in_proj(x, norm_mix_g[i], w_in[i])
        ut = u.reshape(b, nk, SSM_CHUNK, g, SSM_GROUP).transpose(3, 0, 1, 2, 4).reshape(g, b * nk, CHUNK_W)
        yt = _ssm_chunk(ut, mt, gre, gim, hre, him, apr, api, d_skip[i], b)
        y = yt.reshape(g, b, nk, SSM_CHUNK, SSM_GROUP).transpose(1, 2, 3, 0, 4).reshape(b, l, sw)
        oa = _moba_attention(qt, k, vt)
        x = _merge_out(x, oa, sga, sgb, y, w_glu[i], w_out[i])
        x = _mlp_final(x, norm_mlp_g[i], w_up[i], w_down[i], norm_final_g, final_norm=(i == depth - 1))
    return x
```

```python
import functools
import math

import jax
import jax.numpy as jnp
from jax import lax
from jax.experimental import pallas as pl
from jax.experimental.pallas import tpu as pltpu

F32 = jnp.float32
BF16 = jnp.bfloat16

N_HEADS = 8
HEAD_DIM = 128
HALF_DIM = HEAD_DIM // 2
MOBA_BLOCK = 256
MOBA_TOPK = 3
ATTN_HEADS = 4
KV_GROUP = 4
BF16_SUBLANES = 16
V_ROWS = HEAD_DIM + BF16_SUBLANES
ROPE_THETA = 10000.0
SSM_GROUP = 16
SSM_STATE = 64
SSM_CHUNK = 16
CHUNK_W = SSM_CHUNK * SSM_GROUP
SCAN_ROWS = 8
RMS_EPS = 1e-6
NEG_BIG = -1e30
TOKEN_TILE = 512
VMEM_LIMIT = 56 * 1024 * 1024

_NT = (((1,), (1,)), ((), ()))


def _cparams(*sem):
    return pltpu.CompilerParams(dimension_semantics=sem, vmem_limit_bytes=VMEM_LIMIT)


def _const_spec(shape):
    nd = len(shape)
    return pl.BlockSpec(shape, lambda *_: (0,) * nd)


def _cexp(lr, li, step, e):
    mag = jnp.exp(lr * step * e)
    ang = li * step * e
    return mag * jnp.cos(ang), mag * jnp.sin(ang)


def _ssm_prep_kernel(lrr_ref, lir_ref, lrc_ref, lic_ref, ls_ref, btr_ref, bti_ref, ctr_ref, cti_ref,
                     mt_ref, gre_ref, gim_ref, hre_ref, him_ref, apr_ref, api_ref):
    step = jnp.exp(ls_ref[0])
    lr_r, li_r = lrr_ref[0], lir_ref[0]
    lr_c, li_c = lrc_ref[0], lic_ref[0]
    ar, ai = _cexp(lr_r, li_r, step, 1.0)
    den = lr_r * lr_r + li_r * li_r
    nr, ni = ar - 1.0, ai
    cr = (nr * lr_r + ni * li_r) / den
    ci = (ni * lr_r - nr * li_r) / den
    btr, bti = btr_ref[0], bti_ref[0]
    bbr = cr * btr - ci * bti
    bbi = cr * bti + ci * btr
    s_col = lax.shift_right_logical(lax.broadcasted_iota(jnp.int32, (CHUNK_W, 1), 0), 4)
    e_col = (SSM_CHUNK - 1 - s_col).astype(F32)
    pr, pi_ = _cexp(lr_r, li_r, step, e_col)
    gre_ref[0] = pr * bbr - pi_ * bbi
    gim_ref[0] = pr * bbi + pi_ * bbr
    tau = lax.shift_right_logical(lax.broadcasted_iota(jnp.int32, (1, CHUNK_W), 1), 4).astype(F32)
    ctr, cti = ctr_ref[0], cti_ref[0]
    wr, wi = _cexp(lr_c, li_c, step, tau)
    p_r = ctr * wr - cti * wi
    p_i = ctr * wi + cti * wr
    hp = lax.Precision.HIGHEST
    r0 = (jnp.dot(bbr[:SSM_GROUP], p_r, precision=hp, preferred_element_type=F32)
          - jnp.dot(bbi[:SSM_GROUP], p_i, precision=hp, preferred_element_type=F32))
    lane = lax.broadcasted_iota(jnp.int32, (SSM_GROUP, CHUNK_W), 1)
    for s in range(SSM_CHUNK):
        blk = r0 if s == 0 else pltpu.roll(r0, SSM_GROUP * s, axis=1)
        blk = jnp.where(lane >= SSM_GROUP * s, blk, 0.0)
        mt_ref[0, s * SSM_GROUP:(s + 1) * SSM_GROUP, :] = blk.astype(mt_ref.dtype)
    hr, hi = _cexp(lr_c, li_c, step, tau + 1.0)
    hre_ref[0] = ctr * hr - cti * hi
    him_ref[0] = -(ctr * hi + cti * hr)
    e_row = ((lax.broadcasted_iota(jnp.int32, (SCAN_ROWS, 1), 0) + 1) * SSM_CHUNK).astype(F32)
    qr, qi = _cexp(lr_r, li_r, step, e_row)
    apr_ref[0] = qr
    api_ref[0] = qi


def _ssm_prep(lam_re, lam_im, log_step, b_re, b_im, c_re, c_im):
    g, n = lam_re.shape
    c = b_re.shape[-1]
    bt = lambda b: jnp.tile(jnp.swapaxes(b, 1, 2), (1, SSM_CHUNK, 1))
    ct = lambda m: jnp.tile(jnp.swapaxes(m, 1, 2), (1, 1, SSM_CHUNK))
    args = (lam_re.reshape(g, 1, n), lam_im.reshape(g, 1, n), lam_re.reshape(g, n, 1),
            lam_im.reshape(g, n, 1), log_step.reshape(g, 1, 1), bt(b_re), bt(b_im), ct(c_re), ct(c_im))
    spec = lambda shp: pl.BlockSpec((1,) + shp, lambda i: (i, 0, 0))
    in_specs = [spec((1, n)), spec((1, n)), spec((n, 1)), spec((n, 1)), spec((1, 1)),
                spec((CHUNK_W, n)), spec((CHUNK_W, n)), spec((n, CHUNK_W)), spec((n, CHUNK_W))]
    out_shapes = [((CHUNK_W, CHUNK_W), BF16), ((CHUNK_W, n), F32), ((CHUNK_W, n), F32),
                  ((n, CHUNK_W), F32), ((n, CHUNK_W), F32), ((SCAN_ROWS, n), F32), ((SCAN_ROWS, n), F32)]
    return pl.pallas_call(
        _ssm_prep_kernel,
        grid=(g,),
        in_specs=in_specs,
        out_specs=[spec(s) for s, _ in out_shapes],
        out_shape=[jax.ShapeDtypeStruct((g,) + s, d) for s, d in out_shapes],
        compiler_params=_cparams("parallel"),
        name="ssm_prep",
    )(*args)


def _in_proj_kernel(x_ref, g_ref, wqt_ref, wk_ref, wvt_ref, wu_ref, wg_ref,
                    cos_ref, sin_ref, cost_ref, sint_ref,
                    qt_ref, k_ref, vt_ref, u_ref, sga_ref, sgb_ref):
    x = x_ref[0]
    d = x.shape[-1]
    ms = jnp.mean(x * x, axis=-1, keepdims=True)
    h = (x * lax.rsqrt(ms + RMS_EPS) * g_ref[...]).astype(BF16)
    cos, sin = cos_ref[...], sin_ref[...]
    for c0 in range(0, d, 512):
        p = jnp.dot(h, wk_ref[:, c0:c0 + 512], preferred_element_type=F32)
        for hh in range(512 // HEAD_DIM):
            sl = p[:, hh * HEAD_DIM:(hh + 1) * HEAD_DIM]
            r = sl * cos + pltpu.roll(sl, HALF_DIM, axis=1) * sin
            k_ref[0, :, c0 + hh * HEAD_DIM:c0 + (hh + 1) * HEAD_DIM] = r.astype(BF16)
    cost, sint = cost_ref[...], sint_ref[...]
    scale = HEAD_DIM ** -0.5 * math.log2(math.e)
    for c0 in range(0, d, 256):
        pt = lax.dot_general(wqt_ref[c0:c0 + 256, :], h, _NT, preferred_element_type=F32)
        for hh in range(256 // HEAD_DIM):
            r0 = hh * HEAD_DIM
            x1 = pt[r0:r0 + HALF_DIM]
            x2 = pt[r0 + HALF_DIM:r0 + HEAD_DIM]
            qt_ref[0, c0 + r0:c0 + r0 + HALF_DIM, :] = ((x1 * cost - x2 * sint) * scale).astype(BF16)
            qt_ref[0, c0 + r0 + HALF_DIM:c0 + r0 + HEAD_DIM, :] = ((x2 * cost + x1 * sint) * scale).astype(BF16)
    ones = jnp.ones((BF16_SUBLANES, x.shape[0]), BF16)
    for c0 in range(0, d, 256):
        pv = lax.dot_general(wvt_ref[c0:c0 + 256, :], h, _NT, preferred_element_type=F32).astype(BF16)
        for hh in range(256 // HEAD_DIM):
            r0 = (c0 // HEAD_DIM + hh) * V_ROWS
            vt_ref[0, r0:r0 + HEAD_DIM, :] = pv[hh * HEAD_DIM:(hh + 1) * HEAD_DIM]
            vt_ref[0, r0 + HEAD_DIM:r0 + V_ROWS, :] = ones
    u_ref[0] = jnp.dot(h, wu_ref[...], preferred_element_type=F32).astype(BF16)
    for c0 in range(0, 2 * d, 512):
        sg = jax.nn.sigmoid(jnp.dot(h, wg_ref[:, c0:c0 + 512], preferred_element_type=F32)).astype(BF16)
        if c0 < d:
            sga_ref[0, :, c0:c0 + 512] = sg
        else:
            sgb_ref[0, :, c0 - d:c0 - d + 512] = sg


def _in_proj(x, g, w_in):
    b, l, d = x.shape
    tm = TOKEN_TILE
    sw = w_in.shape[1] - 5 * d
    wqt = w_in[:, :d].T.astype(BF16)
    wk = w_in[:, d:2 * d].astype(BF16)
    wvt = w_in[:, 2 * d:3 * d].T.astype(BF16)
    wu = w_in[:, 3 * d:3 * d + sw].astype(BF16)
    wg = w_in[:, 3 * d + sw:].astype(BF16)
    inv_freq = ROPE_THETA ** (-jnp.arange(HALF_DIM, dtype=F32) / HALF_DIM)
    ang = jnp.arange(l).astype(F32)[:, None] * inv_freq[None, :]
    cos, sin = jnp.cos(ang), jnp.sin(ang)
    cos2 = jnp.concatenate([cos, cos], axis=1)
    sin2 = jnp.concatenate([-sin, sin], axis=1)
    tok = lambda w: pl.BlockSpec((1, tm, w), lambda bi, i: (bi, i, 0))
    feat = lambda w: pl.BlockSpec((1, w, tm), lambda bi, i: (bi, 0, i))
    sds = jax.ShapeDtypeStruct
    return pl.pallas_call(
        _in_proj_kernel,
        grid=(b, l // tm),
        in_specs=[tok(d), _const_spec((1, d)), _const_spec((d, d)), _const_spec((d, d)), _const_spec((d, d)),
                  _const_spec((d, sw)), _const_spec((d, 2 * d)),
                  pl.BlockSpec((tm, HEAD_DIM), lambda bi, i: (i, 0)),
                  pl.BlockSpec((tm, HEAD_DIM), lambda bi, i: (i, 0)),
                  pl.BlockSpec((HALF_DIM, tm), lambda bi, i: (0, i)),
                  pl.BlockSpec((HALF_DIM, tm), lambda bi, i: (0, i))],
        out_specs=[feat(d), tok(d), feat(N_HEADS * V_ROWS), tok(sw), tok(d), tok(d)],
        out_shape=[sds((b, d, l), BF16), sds((b, l, d), BF16), sds((b, N_HEADS * V_ROWS, l), BF16),
                   sds((b, l, sw), BF16), sds((b, l, d), BF16), sds((b, l, d), BF16)],
        compiler_params=_cparams("parallel", "parallel"),
        name="in_proj",
    )(x, g.reshape(1, d), wqt, wk, wvt, wu, wg, cos2, sin2, cos.T, sin.T)


def _ssm_chunk_kernel(u_ref, mt_ref, gre_ref, gim_ref, hre_ref, him_ref, apr_ref, api_ref, d_ref,
                      y_ref, ere_ref, eim_ref, xre_ref, xim_ref):
    nk = u_ref.shape[1]
    u0, u1 = u_ref[0], u_ref[1]
    dot = functools.partial(jnp.dot, preferred_element_type=F32)
    ere_ref[...] = dot(u0, gre_ref[0]) + dot(u1, gre_ref[1])
    eim_ref[...] = dot(u0, gim_ref[0]) + dot(u1, gim_ref[1])
    apr, api = apr_ref[0], api_ref[0]
    row = lax.broadcasted_iota(jnp.int32, apr.shape, 0)
    lvl = [(dd, jnp.broadcast_to(apr[dd - 1:dd], apr.shape), jnp.broadcast_to(api[dd - 1:dd], apr.shape))
           for dd in (1, 2, 4)]

    def tile(t, carry):
        cr, ci = carry
        r0 = pl.multiple_of(t * SCAN_ROWS, SCAN_ROWS)
        zr = ere_ref[pl.ds(r0, SCAN_ROWS), :]
        zi = eim_ref[pl.ds(r0, SCAN_ROWS), :]
        for dd, adr, adi in lvl:
            sr = jnp.where(row >= dd, pltpu.roll(zr, dd, axis=0), 0.0)
            si = jnp.where(row >= dd, pltpu.roll(zi, dd, axis=0), 0.0)
            zr, zi = zr + (adr * sr - adi * si), zi + (adr * si + adi * sr)
        sr = apr * cr - api * ci + zr
        si = apr * ci + api * cr + zi
        xre_ref[pl.ds(r0, SCAN_ROWS), :] = jnp.where(row >= 1, pltpu.roll(sr, 1, axis=0), cr)
        xim_ref[pl.ds(r0, SCAN_ROWS), :] = jnp.where(row >= 1, pltpu.roll(si, 1, axis=0), ci)
        last = SCAN_ROWS - 1
        return (jnp.broadcast_to(sr[last:last + 1], sr.shape), jnp.broadcast_to(si[last:last + 1], si.shape))

    zero = jnp.zeros(apr.shape, F32)
    lax.fori_loop(0, nk // SCAN_ROWS, tile, (zero, zero))
    xre = xre_ref[...].astype(BF16)
    xim = xim_ref[...].astype(BF16)
    for gi, ug in enumerate((u0, u1)):
        y = (dot(ug, mt_ref[gi]) + dot(xre, hre_ref[gi]) + dot(xim, him_ref[gi])
             + d_ref[gi] * ug.astype(F32))
        y_ref[gi] = jax.nn.gelu(y).astype(y_ref.dtype)


def _ssm_chunk(ut, mt, gre, gim, hre, him, apr, api, d_skip, batch):
    g, rows, _ = ut.shape
    nk = rows // batch
    n = gre.shape[-1]
    gp = g // 2
    zg = jnp.zeros_like(gre)
    zh = jnp.zeros_like(hre)
    even = (jnp.arange(g) % 2 == 0)[:, None, None]
    pad_g = lambda m: jnp.where(even, jnp.concatenate([m, zg], 2), jnp.concatenate([zg, m], 2)).astype(BF16)
    pad_h = lambda m: jnp.where(even, jnp.concatenate([m, zh], 1), jnp.concatenate([zh, m], 1)).astype(BF16)
    pair = lambda m: m.reshape(gp, 2, SCAN_ROWS, n).transpose(0, 2, 1, 3).reshape(gp, SCAN_ROWS, 2 * n)
    d_t = jnp.tile(d_skip, (1, SSM_CHUNK)).reshape(g, 1, CHUNK_W).astype(F32)
    two = lambda r, c: pl.BlockSpec((2, r, c), lambda p, bi: (p, 0, 0))
    return pl.pallas_call(
        _ssm_chunk_kernel,
        grid=(gp, batch),
        in_specs=[pl.BlockSpec((2, nk, CHUNK_W), lambda p, bi: (p, bi, 0)),
                  two(CHUNK_W, CHUNK_W), two(CHUNK_W, 2 * n), two(CHUNK_W, 2 * n),
                  two(2 * n, CHUNK_W), two(2 * n, CHUNK_W),
                  pl.BlockSpec((1, SCAN_ROWS, 2 * n), lambda p, bi: (p, 0, 0)),
                  pl.BlockSpec((1, SCAN_ROWS, 2 * n), lambda p, bi: (p, 0, 0)),
                  two(1, CHUNK_W)],
        out_specs=pl.BlockSpec((2, nk, CHUNK_W), lambda p, bi: (p, bi, 0)),
        out_shape=jax.ShapeDtypeStruct(ut.shape, BF16),
        scratch_shapes=[pltpu.VMEM((nk, 2 * n), F32) for _ in range(4)],
        compiler_params=_cparams("parallel", "parallel"),
        name="ssm_chunk",
    )(ut, mt, pad_g(gre), pad_g(gim), pad_h(hre), pad_h(him), pair(apr), pair(api), d_t)


def _moba_kernel(qt_ref, k_ref, vt_ref, o_ref, kmean_ref, sel_ref, m_ref, acc_ref, s_ref):
    i = pl.program_id(2)
    blk = MOBA_BLOCK
    nb = k_ref.shape[1] // blk
    heads = range(ATTN_HEADS)
    hsl = lambda hp: slice(hp * HEAD_DIM, (hp + 1) * HEAD_DIM)

    @pl.when(i == 0)
    def _():
        for hp in heads:
            for j in range(nb):
                kb = k_ref[0, j * blk:(j + 1) * blk, hsl(hp)].astype(F32)
                kmean_ref[hp, j:j + 1, :] = jnp.mean(kb, axis=0, keepdims=True)

    qts = [qt_ref[0, hsl(hp), :] for hp in heads]
    for hp in heads:
        gate = jnp.dot(kmean_ref[hp], qts[hp].astype(F32), precision=lax.Precision.HIGHEST,
                       preferred_element_type=F32)
        brow = lax.broadcasted_iota(jnp.int32, gate.shape, 0)
        work = jnp.where(brow < i, gate, -jnp.inf)
        sel = jnp.zeros(gate.shape, F32)
        for _ in range(MOBA_TOPK):
            mx = jnp.max(work, axis=0, keepdims=True)
            idx = jnp.min(jnp.where(work == mx, brow, nb), axis=0, keepdims=True)
            hit = brow == idx
            sel = jnp.where(hit, jnp.where(idx < i, 1.0, 0.0), sel)
            work = jnp.where(hit, -jnp.inf, work)
        sel_ref[hp] = sel

    vsl = lambda hp: slice(hp * V_ROWS, (hp + 1) * V_ROWS)

    d0 = pl.multiple_of(i * blk, blk)
    for hp in heads:
        st = jnp.dot(k_ref[0, pl.ds(d0, blk), hsl(hp)], qts[hp], preferred_element_type=F32)
        krow = lax.broadcasted_iota(jnp.int32, st.shape, 0)
        qcol = lax.broadcasted_iota(jnp.int32, st.shape, 1)
        st = jnp.where(krow <= qcol, st, NEG_BIG)
        m0 = jnp.max(st, axis=0, keepdims=True)
        p = jnp.exp2(st - m0)
        m_ref[hp] = m0
        acc_ref[hp] = jnp.dot(vt_ref[0, vsl(hp), pl.ds(d0, blk)], p.astype(BF16), preferred_element_type=F32)

    gw = KV_GROUP * blk
    last_group = nb // KV_GROUP - 1

    def scores(hp, gidx):
        r0 = pl.multiple_of(gidx * gw, gw)
        s_ref[hp] = jnp.dot(k_ref[0, pl.ds(r0, gw), hsl(hp)], qts[hp], preferred_element_type=F32)

    def weights(hp, gidx):
        sblk = lambda jl: s_ref[hp, jl * blk:(jl + 1) * blk, :]
        picked = [sel_ref[hp, pl.ds(gidx * KV_GROUP + jl, 1), :] > 0.0 for jl in range(KV_GROUP)]
        m_old = m_ref[hp]
        m_new = m_old
        for jl in range(KV_GROUP):
            cmax = jnp.max(sblk(jl), axis=0, keepdims=True)
            m_new = jnp.maximum(m_new, jnp.where(picked[jl], cmax, NEG_BIG))
        m_ref[hp] = m_new
        ps = [jnp.exp2(sblk(jl) - jnp.where(picked[jl], m_new, -NEG_BIG)).astype(BF16) for jl in range(KV_GROUP)]
        return jnp.exp2(m_old - m_new), jnp.concatenate(ps, axis=0)

    def accumulate(hp, gidx, alpha, p):
        r0 = pl.multiple_of(gidx * gw, gw)
        acc_ref[hp] = alpha * acc_ref[hp] + jnp.dot(vt_ref[0, vsl(hp), pl.ds(r0, gw)], p,
                                                    preferred_element_type=F32)

    for hp in heads:
        scores(hp, 0)

    def past_group(gidx, _):
        nxt = jnp.minimum(gidx + 1, last_group)
        for hp in heads:
            alpha, p = weights(hp, gidx)
            accumulate(hp, gidx, alpha, p)
            scores(hp, nxt)
        return 0

    lax.fori_loop(0, (i + KV_GROUP - 1) // KV_GROUP, past_group, 0)
    for hp in heads:
        ot = acc_ref[hp, :HEAD_DIM, :] / acc_ref[hp, HEAD_DIM:HEAD_DIM + 1, :]
        o_ref[0, :, hsl(hp)] = ot.T.astype(o_ref.dtype)


def _moba_attention(qt, k, vt):
    b, l, d = k.shape
    nb = l // MOBA_BLOCK
    assert nb % KV_GROUP == 0
    hw = ATTN_HEADS * HEAD_DIM
    return pl.pallas_call(
        _moba_kernel,
        grid=(b, N_HEADS // ATTN_HEADS, nb),
        in_specs=[pl.BlockSpec((1, hw, MOBA_BLOCK), lambda bi, h, i: (bi, h, i)),
                  pl.BlockSpec((1, l, hw), lambda bi, h, i: (bi, 0, h)),
                  pl.BlockSpec((1, ATTN_HEADS * V_ROWS, l), lambda bi, h, i: (bi, h, 0))],
        out_specs=pl.BlockSpec((1, MOBA_BLOCK, hw), lambda bi, h, i: (bi, i, h)),
        out_shape=jax.ShapeDtypeStruct((b, l, d), BF16),
        scratch_shapes=[pltpu.VMEM((ATTN_HEADS, nb, HEAD_DIM), F32),
                        pltpu.VMEM((ATTN_HEADS, nb, MOBA_BLOCK), F32),
                        pltpu.VMEM((ATTN_HEADS, 1, MOBA_BLOCK), F32),
                        pltpu.VMEM((ATTN_HEADS, V_ROWS, MOBA_BLOCK), F32),
                        pltpu.VMEM((ATTN_HEADS, KV_GROUP * MOBA_BLOCK, MOBA_BLOCK), F32)],
        compiler_params=_cparams("parallel", "parallel", "arbitrary"),
        name="moba_attn",
    )(qt, k, vt)


def _merge_kernel(x_ref, oa_ref, sga_ref, sgb_ref, y_ref, wglu_ref, wout_ref, o_ref):
    d = x_ref.shape[-1]
    hgl = jnp.dot(y_ref[0], wglu_ref[...], preferred_element_type=F32)
    ob = hgl[:, :d] * jax.nn.sigmoid(hgl[:, d:])
    mixed = sga_ref[0].astype(F32) * oa_ref[0].astype(F32) + sgb_ref[0].astype(F32) * ob
    o_ref[0] = x_ref[0] + jnp.dot(mixed.astype(BF16), wout_ref[...], preferred_element_type=F32)


def _merge_out(x, oa, sga, sgb, y, w_glu, w_out):
    b, l, d = x.shape
    tm = TOKEN_TILE
    sw = y.shape[-1]
    tok = lambda w: pl.BlockSpec((1, tm, w), lambda bi, i: (bi, i, 0))
    return pl.pallas_call(
        _merge_kernel,
        grid=(b, l // tm),
        in_specs=[tok(d), tok(d), tok(d), tok(d), tok(sw), _const_spec((sw, 2 * d)), _const_spec((d, d))],
        out_specs=tok(d),
        out_shape=jax.ShapeDtypeStruct((b, l, d), F32),
        compiler_params=_cparams("parallel", "parallel"),
        name="merge_out",
    )(x, oa, sga, sgb, y, w_glu.astype(BF16), w_out.astype(BF16))


def _mlp_kernel(x_ref, g1_ref, wup_ref, wdn_ref, g2_ref, o_ref, *, final_norm):
    x = x_ref[0]
    d = x.shape[-1]
    ff = wup_ref.shape[1]
    ms = jnp.mean(x * x, axis=-1, keepdims=True)
    h = (x * lax.rsqrt(ms + RMS_EPS) * g1_ref[...]).astype(BF16)
    acc = x
    for c0 in range(0, ff, d):
        a = jnp.maximum(jnp.dot(h, wup_ref[:, c0:c0 + d], preferred_element_type=F32), 0.0)
        acc = acc + jnp.dot((a * a).astype(BF16), wdn_ref[c0:c0 + d, :], preferred_element_type=F32)
    if final_norm:
        ms2 = jnp.mean(acc * acc, axis=-1, keepdims=True)
        acc = acc * lax.rsqrt(ms2 + RMS_EPS) * g2_ref[...]
    o_ref[0] = acc


def _mlp_final(x, g1, w_up, w_down, g2, final_norm):
    b, l, d = x.shape
    tm = TOKEN_TILE
    ff = w_up.shape[1]
    tok = pl.BlockSpec((1, tm, d), lambda bi, i: (bi, i, 0))
    return pl.pallas_call(
        functools.partial(_mlp_kernel, final_norm=final_norm),
        grid=(b, l // tm),
        in_specs=[tok, _const_spec((1, d)), _const_spec((d, ff)), _const_spec((ff, d)), _const_spec((1, d))],
        out_specs=tok,
        out_shape=jax.ShapeDtypeStruct((b, l, d), F32),
        compiler_params=_cparams("parallel", "parallel"),
        name="mlp_final",
    )(x, g1.reshape(1, d), w_up.astype(BF16), w_down.astype(BF16), g2.reshape(1, d))


def kernel(x, norm_mix_g, w_in, lam_re, lam_im, log_step, b_re, b_im, c_re, c_im, d_skip, w_glu, w_out,
           norm_mlp_g, w_up, w_down, norm_final_g):
    b, l, d = x.shape
    depth = w_in.shape[0]
    assert d == N_HEADS * HEAD_DIM and l % TOKEN_TILE == 0 and l % (SSM_CHUNK * SCAN_ROWS) == 0
    for i in range(depth):
        g = lam_re.shape[1]
        sw = g * SSM_GROUP
        nk = l // SSM_CHUNK
        mt, gre, gim, hre, him, apr, api = _ssm_prep(
            lam_re[i], lam_im[i], log_step[i], b_re[i], b_im[i], c_re[i], c_im[i])
        qt, k, vt, u, sga, sgb = _in_proj(x, norm_mix_g[i], w_in[i])
        ut = u.reshape(b, nk, SSM_CHUNK, g, SSM_GROUP).transpose(3, 0, 1, 2, 4).reshape(g, b * nk, CHUNK_W)
        yt = _ssm_chunk(ut, mt, gre, gim, hre, him, apr, api, d_skip[i], b)
        y = yt.reshape(g, b, nk, SSM_CHUNK, SSM_GROUP).transpose(1, 2, 3, 0, 4).reshape(b, l, sw)
        oa = _moba_attention(qt, k, vt)
        x = _merge_out(x, oa, sga, sgb, y, w_glu[i], w_out[i])
        x = _mlp_final(x, norm_mlp_g[i], w_up[i], w_down[i], norm_final_g, final_norm=(i == depth - 1))
    return x
```

```python
import functools
import math

import jax
import jax.numpy as jnp
from jax import lax
from jax.experimental import pallas as pl
from jax.experimental.pallas import tpu as pltpu

F32 = jnp.float32
BF16 = jnp.bfloat16

N_HEADS = 8
HEAD_DIM = 128
HALF_DIM = HEAD_DIM // 2
MOBA_BLOCK = 256
MOBA_TOPK = 3
ATTN_HEADS = 4
KV_GROUP = 4
BF16_SUBLANES = 16
V_ROWS = HEAD_DIM + BF16_SUBLANES
ROPE_THETA = 10000.0
SSM_GROUP = 16
SSM_STATE = 64
SSM_CHUNK = 16
CHUNK_W = SSM_CHUNK * SSM_GROUP
SCAN_ROWS = 8
LANE = 128
GROUPS_PER_SLAB = LANE // SSM_GROUP
RMS_EPS = 1e-6
NEG_BIG = -1e30
TOKEN_TILE = 512
VMEM_LIMIT = 56 * 1024 * 1024

_NT = (((1,), (1,)), ((), ()))


def _cparams(*sem):
    return pltpu.CompilerParams(dimension_semantics=sem, vmem_limit_bytes=VMEM_LIMIT)


def _const_spec(shape):
    nd = len(shape)
    return pl.BlockSpec(shape, lambda *_: (0,) * nd)


def _cexp(lr, li, step, e):
    mag = jnp.exp(lr * step * e)
    ang = li * step * e
    return mag * jnp.cos(ang), mag * jnp.sin(ang)


def _ssm_prep_kernel(lrr_ref, lir_ref, lrc_ref, lic_ref, ls_ref, btr_ref, bti_ref, ctr_ref, cti_ref,
                     mt_ref, gre_ref, gim_ref, hre_ref, him_ref, apr_ref, api_ref):
    step = jnp.exp(ls_ref[0])
    lr_r, li_r = lrr_ref[0], lir_ref[0]
    lr_c, li_c = lrc_ref[0], lic_ref[0]
    ar, ai = _cexp(lr_r, li_r, step, 1.0)
    den = lr_r * lr_r + li_r * li_r
    nr, ni = ar - 1.0, ai
    cr = (nr * lr_r + ni * li_r) / den
    ci = (ni * lr_r - nr * li_r) / den
    btr, bti = btr_ref[0], bti_ref[0]
    bbr = cr * btr - ci * bti
    bbi = cr * bti + ci * btr
    s_col = lax.shift_right_logical(lax.broadcasted_iota(jnp.int32, (CHUNK_W, 1), 0), 4)
    e_col = (SSM_CHUNK - 1 - s_col).astype(F32)
    pr, pi_ = _cexp(lr_r, li_r, step, e_col)
    gre_ref[0] = pr * bbr - pi_ * bbi
    gim_ref[0] = pr * bbi + pi_ * bbr
    tau = lax.shift_right_logical(lax.broadcasted_iota(jnp.int32, (1, CHUNK_W), 1), 4).astype(F32)
    ctr, cti = ctr_ref[0], cti_ref[0]
    wr, wi = _cexp(lr_c, li_c, step, tau)
    p_r = ctr * wr - cti * wi
    p_i = ctr * wi + cti * wr
    hp = lax.Precision.HIGHEST
    r0 = (jnp.dot(bbr[:SSM_GROUP], p_r, precision=hp, preferred_element_type=F32)
          - jnp.dot(bbi[:SSM_GROUP], p_i, precision=hp, preferred_element_type=F32))
    lane = lax.broadcasted_iota(jnp.int32, (SSM_GROUP, CHUNK_W), 1)
    for s in range(SSM_CHUNK):
        blk = r0 if s == 0 else pltpu.roll(r0, SSM_GROUP * s, axis=1)
        blk = jnp.where(lane >= SSM_GROUP * s, blk, 0.0)
        mt_ref[0, s * SSM_GROUP:(s + 1) * SSM_GROUP, :] = blk.astype(mt_ref.dtype)
    hr, hi = _cexp(lr_c, li_c, step, tau + 1.0)
    hre_ref[0] = ctr * hr - cti * hi
    him_ref[0] = -(ctr * hi + cti * hr)
    e_row = ((lax.broadcasted_iota(jnp.int32, (SCAN_ROWS, 1), 0) + 1) * SSM_CHUNK).astype(F32)
    qr, qi = _cexp(lr_r, li_r, step, e_row)
    apr_ref[0] = qr
    api_ref[0] = qi


def _ssm_prep(lam_re, lam_im, log_step, b_re, b_im, c_re, c_im):
    g, n = lam_re.shape
    c = b_re.shape[-1]
    bt = lambda b: jnp.tile(jnp.swapaxes(b, 1, 2), (1, SSM_CHUNK, 1))
    ct = lambda m: jnp.tile(jnp.swapaxes(m, 1, 2), (1, 1, SSM_CHUNK))
    args = (lam_re.reshape(g, 1, n), lam_im.reshape(g, 1, n), lam_re.reshape(g, n, 1),
            lam_im.reshape(g, n, 1), log_step.reshape(g, 1, 1), bt(b_re), bt(b_im), ct(c_re), ct(c_im))
    spec = lambda shp: pl.BlockSpec((1,) + shp, lambda i: (i, 0, 0))
    in_specs = [spec((1, n)), spec((1, n)), spec((n, 1)), spec((n, 1)), spec((1, 1)),
                spec((CHUNK_W, n)), spec((CHUNK_W, n)), spec((n, CHUNK_W)), spec((n, CHUNK_W))]
    out_shapes = [((CHUNK_W, CHUNK_W), BF16), ((CHUNK_W, n), F32), ((CHUNK_W, n), F32),
                  ((n, CHUNK_W), F32), ((n, CHUNK_W), F32), ((SCAN_ROWS, n), F32), ((SCAN_ROWS, n), F32)]
    return pl.pallas_call(
        _ssm_prep_kernel,
        grid=(g,),
        in_specs=in_specs,
        out_specs=[spec(s) for s, _ in out_shapes],
        out_shape=[jax.ShapeDtypeStruct((g,) + s, d) for s, d in out_shapes],
        compiler_params=_cparams("parallel"),
        name="ssm_prep",
    )(*args)


def _in_proj_kernel(x_ref, g_ref, wqt_ref, wk_ref, wvt_ref, wu_ref, wg_ref,
                    cos_ref, sin_ref, cost_ref, sint_ref,
                    qt_ref, k_ref, vt_ref, u_ref, sga_ref, sgb_ref):
    x = x_ref[0]
    d = x.shape[-1]
    ms = jnp.mean(x * x, axis=-1, keepdims=True)
    h = (x * lax.rsqrt(ms + RMS_EPS) * g_ref[...]).astype(BF16)
    cos, sin = cos_ref[...], sin_ref[...]
    for c0 in range(0, d, 512):
        p = jnp.dot(h, wk_ref[:, c0:c0 + 512], preferred_element_type=F32)
        for hh in range(512 // HEAD_DIM):
            sl = p[:, hh * HEAD_DIM:(hh + 1) * HEAD_DIM]
            r = sl * cos + pltpu.roll(sl, HALF_DIM, axis=1) * sin
            k_ref[0, :, c0 + hh * HEAD_DIM:c0 + (hh + 1) * HEAD_DIM] = r.astype(BF16)
    cost, sint = cost_ref[...], sint_ref[...]
    scale = HEAD_DIM ** -0.5 * math.log2(math.e)
    for c0 in range(0, d, 256):
        pt = lax.dot_general(wqt_ref[c0:c0 + 256, :], h, _NT, preferred_element_type=F32)
        for hh in range(256 // HEAD_DIM):
            r0 = hh * HEAD_DIM
            x1 = pt[r0:r0 + HALF_DIM]
            x2 = pt[r0 + HALF_DIM:r0 + HEAD_DIM]
            qt_ref[0, c0 + r0:c0 + r0 + HALF_DIM, :] = ((x1 * cost - x2 * sint) * scale).astype(BF16)
            qt_ref[0, c0 + r0 + HALF_DIM:c0 + r0 + HEAD_DIM, :] = ((x2 * cost + x1 * sint) * scale).astype(BF16)
    ones = jnp.ones((BF16_SUBLANES, x.shape[0]), BF16)
    for c0 in range(0, d, 256):
        pv = lax.dot_general(wvt_ref[c0:c0 + 256, :], h, _NT, preferred_element_type=F32).astype(BF16)
        for hh in range(256 // HEAD_DIM):
            r0 = (c0 // HEAD_DIM + hh) * V_ROWS
            vt_ref[0, r0:r0 + HEAD_DIM, :] = pv[hh * HEAD_DIM:(hh + 1) * HEAD_DIM]
            vt_ref[0, r0 + HEAD_DIM:r0 + V_ROWS, :] = ones
    pu = jnp.dot(h, wu_ref[...], preferred_element_type=F32).astype(BF16)
    for s in range(pu.shape[1] // LANE):
        u_ref[s, 0] = pu[:, s * LANE:(s + 1) * LANE]
    for c0 in range(0, 2 * d, 512):
        sg = jax.nn.sigmoid(jnp.dot(h, wg_ref[:, c0:c0 + 512], preferred_element_type=F32)).astype(BF16)
        if c0 < d:
            sga_ref[0, :, c0:c0 + 512] = sg
        else:
            sgb_ref[0, :, c0 - d:c0 - d + 512] = sg


def _in_proj(x, g, w_in):
    b, l, d = x.shape
    tm = TOKEN_TILE
    sw = w_in.shape[1] - 5 * d
    wqt = w_in[:, :d].T.astype(BF16)
    wk = w_in[:, d:2 * d].astype(BF16)
    wvt = w_in[:, 2 * d:3 * d].T.astype(BF16)
    wu = w_in[:, 3 * d:3 * d + sw].astype(BF16)
    wg = w_in[:, 3 * d + sw:].astype(BF16)
    inv_freq = ROPE_THETA ** (-jnp.arange(HALF_DIM, dtype=F32) / HALF_DIM)
    ang = jnp.arange(l).astype(F32)[:, None] * inv_freq[None, :]
    cos, sin = jnp.cos(ang), jnp.sin(ang)
    cos2 = jnp.concatenate([cos, cos], axis=1)
    sin2 = jnp.concatenate([-sin, sin], axis=1)
    tok = lambda w: pl.BlockSpec((1, tm, w), lambda bi, i: (bi, i, 0))
    feat = lambda w: pl.BlockSpec((1, w, tm), lambda bi, i: (bi, 0, i))
    sds = jax.ShapeDtypeStruct
    return pl.pallas_call(
        _in_proj_kernel,
        grid=(b, l // tm),
        in_specs=[tok(d), _const_spec((1, d)), _const_spec((d, d)), _const_spec((d, d)), _const_spec((d, d)),
                  _const_spec((d, sw)), _const_spec((d, 2 * d)),
                  pl.BlockSpec((tm, HEAD_DIM), lambda bi, i: (i, 0)),
                  pl.BlockSpec((tm, HEAD_DIM), lambda bi, i: (i, 0)),
                  pl.BlockSpec((HALF_DIM, tm), lambda bi, i: (0, i)),
                  pl.BlockSpec((HALF_DIM, tm), lambda bi, i: (0, i))],
        out_specs=[feat(d), tok(d), feat(N_HEADS * V_ROWS),
                   pl.BlockSpec((sw // LANE, 1, tm, LANE), lambda bi, i: (0, bi, i, 0)), tok(d), tok(d)],
        out_shape=[sds((b, d, l), BF16), sds((b, l, d), BF16), sds((b, N_HEADS * V_ROWS, l), BF16),
                   sds((sw // LANE, b, l, LANE), BF16), sds((b, l, d), BF16), sds((b, l, d), BF16)],
        compiler_params=_cparams("parallel", "parallel"),
        name="in_proj",
    )(x, g.reshape(1, d), wqt, wk, wvt, wu, wg, cos2, sin2, cos.T, sin.T)


def _ssm_chunk_kernel(u_ref, mx_ref, gx_ref, hx_ref, apr_ref, api_ref, d_ref,
                      y_ref, ere_ref, eim_ref, xre_ref, xim_ref):
    nk = u_ref.shape[2]
    sl = ere_ref.shape[1]
    u = u_ref[0, 0]
    dot = functools.partial(jnp.dot, preferred_element_type=F32)
    e = dot(u, gx_ref[0])
    ere_ref[...] = e[:, :sl]
    eim_ref[...] = e[:, sl:]
    apr, api = apr_ref[0], api_ref[0]
    row = lax.broadcasted_iota(jnp.int32, apr.shape, 0)
    lvl = [(dd, jnp.broadcast_to(apr[dd - 1:dd], apr.shape), jnp.broadcast_to(api[dd - 1:dd], apr.shape))
           for dd in (1, 2, 4)]

    def tile(t, carry):
        cr, ci = carry
        r0 = pl.multiple_of(t * SCAN_ROWS, SCAN_ROWS)
        zr = ere_ref[pl.ds(r0, SCAN_ROWS), :]
        zi = eim_ref[pl.ds(r0, SCAN_ROWS), :]
        for dd, adr, adi in lvl:
            sr = jnp.where(row >= dd, pltpu.roll(zr, dd, axis=0), 0.0)
            si = jnp.where(row >= dd, pltpu.roll(zi, dd, axis=0), 0.0)
            zr, zi = zr + (adr * sr - adi * si), zi + (adr * si + adi * sr)
        sr = apr * cr - api * ci + zr
        si = apr * ci + api * cr + zi
        xre_ref[pl.ds(r0, SCAN_ROWS), :] = jnp.where(row >= 1, pltpu.roll(sr, 1, axis=0), cr)
        xim_ref[pl.ds(r0, SCAN_ROWS), :] = jnp.where(row >= 1, pltpu.roll(si, 1, axis=0), ci)
        last = SCAN_ROWS - 1
        return (jnp.broadcast_to(sr[last:last + 1], sr.shape), jnp.broadcast_to(si[last:last + 1], si.shape))

    zero = jnp.zeros(apr.shape, F32)
    lax.fori_loop(0, nk // SCAN_ROWS, tile, (zero, zero))
    x0 = jnp.concatenate([xre_ref[...], xim_ref[...]], axis=1).astype(BF16)
    y = dot(u, mx_ref[0]) + dot(x0, hx_ref[0]) + d_ref[0] * u.astype(F32)
    y_ref[0, 0] = jax.nn.gelu(y).astype(y_ref.dtype)


def _slab_matrices(mt, gre, gim, hre, him, apr, api, d_skip):
    g = mt.shape[0]
    ns, gs, t, c, n = g // GROUPS_PER_SLAB, GROUPS_PER_SLAB, SSM_CHUNK, SSM_GROUP, gre.shape[-1]
    eye = jnp.eye(gs, dtype=F32)
    mx = jnp.einsum('sgtcuv,gh->stgcuhv', mt.astype(F32).reshape(ns, gs, t, c, t, c), eye)
    mx = mx.reshape(ns, t * gs * c, t * gs * c).astype(BF16)
    gx = [jnp.einsum('sgtcn,gh->stgchn', m.reshape(ns, gs, t, c, n), eye).reshape(ns, t * gs * c, gs * n)
          for m in (gre, gim)]
    gx = jnp.concatenate(gx, axis=2).astype(BF16)
    hx = [jnp.einsum('sgntc,gh->sgnthc', m.reshape(ns, gs, n, t, c), eye).reshape(ns, gs * n, t * gs * c)
          for m in (hre, him)]
    hx = jnp.concatenate(hx, axis=1).astype(BF16)
    pw = lambda m: m.reshape(ns, gs, SCAN_ROWS, n).transpose(0, 2, 1, 3).reshape(ns, SCAN_ROWS, gs * n)
    dx = jnp.broadcast_to(d_skip.astype(F32).reshape(ns, 1, gs, c), (ns, t, gs, c)).reshape(ns, 1, t * gs * c)
    return mx, gx, hx, pw(apr), pw(api), dx


def _ssm_chunk(u4, mt, gre, gim, hre, him, apr, api, d_skip):
    ns, b, l, lane = u4.shape
    nk = l // SSM_CHUNK
    sw = SSM_CHUNK * lane
    mx, gx, hx, apx, aix, dx = _slab_matrices(mt, gre, gim, hre, him, apr, api, d_skip)
    sl = apx.shape[-1]
    per_slab = lambda r, c: pl.BlockSpec((1, r, c), lambda s, bi: (s, 0, 0))
    rows = pl.BlockSpec((1, 1, nk, sw), lambda s, bi: (s, bi, 0, 0))
    yt = pl.pallas_call(
        _ssm_chunk_kernel,
        grid=(ns, b),
        in_specs=[rows, per_slab(sw, sw), per_slab(sw, 2 * sl), per_slab(2 * sl, sw),
                  per_slab(SCAN_ROWS, sl), per_slab(SCAN_ROWS, sl), per_slab(1, sw)],
        out_specs=rows,
        out_shape=jax.ShapeDtypeStruct((ns, b, nk, sw), BF16),
        scratch_shapes=[pltpu.VMEM((nk, sl), F32) for _ in range(4)],
        compiler_params=_cparams("parallel", "parallel"),
        name="ssm_chunk",
    )(u4.reshape(ns, b, nk, sw), mx, gx, hx, apx, aix, dx)
    return yt.reshape(ns, b, l, lane)


def _moba_kernel(qt_ref, k_ref, vt_ref, o_ref, kmean_ref, sel_ref, m_ref, acc_ref, s_ref):
    i = pl.program_id(2)
    blk = MOBA_BLOCK
    nb = k_ref.shape[1] // blk
    heads = range(ATTN_HEADS)
    hsl = lambda hp: slice(hp * HEAD_DIM, (hp + 1) * HEAD_DIM)

    @pl.when(i == 0)
    def _():
        for hp in heads:
            for j in range(nb):
                kb = k_ref[0, j * blk:(j + 1) * blk, hsl(hp)].astype(F32)
                kmean_ref[hp, j:j + 1, :] = jnp.mean(kb, axis=0, keepdims=True)

    qts = [qt_ref[0, hsl(hp), :] for hp in heads]
    for hp in heads:
        gate = jnp.dot(kmean_ref[hp], qts[hp].astype(F32), precision=lax.Precision.HIGHEST,
                       preferred_element_type=F32)
        brow = lax.broadcasted_iota(jnp.int32, gate.shape, 0)
        work = jnp.where(brow < i, gate, -jnp.inf)
        sel = jnp.zeros(gate.shape, F32)
        for _ in range(MOBA_TOPK):
            mx = jnp.max(work, axis=0, keepdims=True)
            idx = jnp.min(jnp.where(work == mx, brow, nb), axis=0, keepdims=True)
            hit = brow == idx
            sel = jnp.where(hit, jnp.where(idx < i, 1.0, 0.0), sel)
            work = jnp.where(hit, -jnp.inf, work)
        sel_ref[hp] = sel

    vsl = lambda hp: slice(hp * V_ROWS, (hp + 1) * V_ROWS)

    d0 = pl.multiple_of(i * blk, blk)
    for hp in heads:
        st = jnp.dot(k_ref[0, pl.ds(d0, blk), hsl(hp)], qts[hp], preferred_element_type=F32)
        krow = lax.broadcasted_iota(jnp.int32, st.shape, 0)
        qcol = lax.broadcasted_iota(jnp.int32, st.shape, 1)
        st = jnp.where(krow <= qcol, st, NEG_BIG)
        m0 = jnp.max(st, axis=0, keepdims=True)
        p = jnp.exp2(st - m0)
        m_ref[hp] = m0
        acc_ref[hp] = jnp.dot(vt_ref[0, vsl(hp), pl.ds(d0, blk)], p.astype(BF16), preferred_element_type=F32)

    gw = KV_GROUP * blk
    last_group = nb // KV_GROUP - 1

    def scores(hp, gidx):
        r0 = pl.multiple_of(gidx * gw, gw)
        s_ref[hp] = jnp.dot(k_ref[0, pl.ds(r0, gw), hsl(hp)], qts[hp], preferred_element_type=F32)

    def weights(hp, gidx):
        sblk = lambda jl: s_ref[hp, jl * blk:(jl + 1) * blk, :]
        picked = [sel_ref[hp, pl.ds(gidx * KV_GROUP + jl, 1), :] > 0.0 for jl in range(KV_GROUP)]
        m_old = m_ref[hp]
        m_new = m_old
        for jl in range(KV_GROUP):
            cmax = jnp.max(sblk(jl), axis=0, keepdims=True)
            m_new = jnp.maximum(m_new, jnp.where(picked[jl], cmax, NEG_BIG))
        m_ref[hp] = m_new
        ps = [jnp.exp2(sblk(jl) - jnp.where(picked[jl], m_new, -NEG_BIG)).astype(BF16) for jl in range(KV_GROUP)]
        return jnp.exp2(m_old - m_new), jnp.concatenate(ps, axis=0)

    def accumulate(hp, gidx, alpha, p):
        r0 = pl.multiple_of(gidx * gw, gw)
        acc_ref[hp] = alpha * acc_ref[hp] + jnp.dot(vt_ref[0, vsl(hp), pl.ds(r0, gw)], p,
                                                    preferred_element_type=F32)

    for hp in heads:
        scores(hp, 0)

    def past_group(gidx, _):
        nxt = jnp.minimum(gidx + 1, last_group)
        for hp in heads:
            alpha, p = weights(hp, gidx)
            accumulate(hp, gidx, alpha, p)
            scores(hp, nxt)
        return 0

    lax.fori_loop(0, (i + KV_GROUP - 1) // KV_GROUP, past_group, 0)
    for hp in heads:
        ot = acc_ref[hp, :HEAD_DIM, :] / acc_ref[hp, HEAD_DIM:HEAD_DIM + 1, :]
        o_ref[0, :, hsl(hp)] = ot.T.astype(o_ref.dtype)


def _moba_attention(qt, k, vt):
    b, l, d = k.shape
    nb = l // MOBA_BLOCK
    assert nb % KV_GROUP == 0
    hw = ATTN_HEADS * HEAD_DIM
    return pl.pallas_call(
        _moba_kernel,
        grid=(b, N_HEADS // ATTN_HEADS, nb),
        in_specs=[pl.BlockSpec((1, hw, MOBA_BLOCK), lambda bi, h, i: (bi, h, i)),
                  pl.BlockSpec((1, l, hw), lambda bi, h, i: (bi, 0, h)),
                  pl.BlockSpec((1, ATTN_HEADS * V_ROWS, l), lambda bi, h, i: (bi, h, 0))],
        out_specs=pl.BlockSpec((1, MOBA_BLOCK, hw), lambda bi, h, i: (bi, i, h)),
        out_shape=jax.ShapeDtypeStruct((b, l, d), BF16),
        scratch_shapes=[pltpu.VMEM((ATTN_HEADS, nb, HEAD_DIM), F32),
                        pltpu.VMEM((ATTN_HEADS, nb, MOBA_BLOCK), F32),
                        pltpu.VMEM((ATTN_HEADS, 1, MOBA_BLOCK), F32),
                        pltpu.VMEM((ATTN_HEADS, V_ROWS, MOBA_BLOCK), F32),
                        pltpu.VMEM((ATTN_HEADS, KV_GROUP * MOBA_BLOCK, MOBA_BLOCK), F32)],
        compiler_params=_cparams("parallel", "parallel", "arbitrary"),
        name="moba_attn",
    )(qt, k, vt)


def _merge_kernel(x_ref, oa_ref, sga_ref, sgb_ref, y_ref, wglu_ref, wout_ref, o_ref):
    d = x_ref.shape[-1]
    y = jnp.concatenate([y_ref[s, 0] for s in range(y_ref.shape[0])], axis=1)
    hgl = jnp.dot(y, wglu_ref[...], preferred_element_type=F32)
    ob = hgl[:, :d] * jax.nn.sigmoid(hgl[:, d:])
    mixed = sga_ref[0].astype(F32) * oa_ref[0].astype(F32) + sgb_ref[0].astype(F32) * ob
    o_ref[0] = x_ref[0] + jnp.dot(mixed.astype(BF16), wout_ref[...], preferred_element_type=F32)


def _merge_out(x, oa, sga, sgb, y, w_glu, w_out):
    b, l, d = x.shape
    tm = TOKEN_TILE
    ns, lane = y.shape[0], y.shape[-1]
    sw = ns * lane
    tok = lambda w: pl.BlockSpec((1, tm, w), lambda bi, i: (bi, i, 0))
    return pl.pallas_call(
        _merge_kernel,
        grid=(b, l // tm),
        in_specs=[tok(d), tok(d), tok(d), tok(d), pl.BlockSpec((ns, 1, tm, lane), lambda bi, i: (0, bi, i, 0)),
                  _const_spec((sw, 2 * d)), _const_spec((d, d))],
        out_specs=tok(d),
        out_shape=jax.ShapeDtypeStruct((b, l, d), F32),
        compiler_params=_cparams("parallel", "parallel"),
        name="merge_out",
    )(x, oa, sga, sgb, y, w_glu.astype(BF16), w_out.astype(BF16))


def _mlp_kernel(x_ref, g1_ref, wup_ref, wdn_ref, g2_ref, o_ref, *, final_norm):
    x = x_ref[0]
    d = x.shape[-1]
    ff = wup_ref.shape[1]
    ms = jnp.mean(x * x, axis=-1, keepdims=True)
    h = (x * lax.rsqrt(ms + RMS_EPS) * g1_ref[...]).astype(BF16)
    acc = x
    for c0 in range(0, ff, d):
        a = jnp.maximum(jnp.dot(h, wup_ref[:, c0:c0 + d], preferred_element_type=F32), 0.0)
        acc = acc + jnp.dot((a * a).astype(BF16), wdn_ref[c0:c0 + d, :], preferred_element_type=F32)
    if final_norm:
        ms2 = jnp.mean(acc * acc, axis=-1, keepdims=True)
        acc = acc * lax.rsqrt(ms2 + RMS_EPS) * g2_ref[...]
    o_ref[0] = acc


def _mlp_final(x, g1, w_up, w_down, g2, final_norm):
    b, l, d = x.shape
    tm = TOKEN_TILE
    ff = w_up.shape[1]
    tok = pl.BlockSpec((1, tm, d), lambda bi, i: (bi, i, 0))
    return pl.pallas_call(
        functools.partial(_mlp_kernel, final_norm=final_norm),
        grid=(b, l // tm),
        in_specs=[tok, _const_spec((1, d)), _const_spec((d, ff)), _const_spec((ff, d)), _const_spec((1, d))],
        out_specs=tok,
        out_shape=jax.ShapeDtypeStruct((b, l, d), F32),
        compiler_params=_cparams("parallel", "parallel"),
        name="mlp_final",
    )(x, g1.reshape(1, d), w_up.astype(BF16), w_down.astype(BF16), g2.reshape(1, d))


def kernel(x, norm_mix_g, w_in, lam_re, lam_im, log_step, b_re, b_im, c_re, c_im, d_skip, w_glu, w_out,
           norm_mlp_g, w_up, w_down, norm_final_g):
    b, l, d = x.shape
    depth = w_in.shape[0]
    assert d == N_HEADS * HEAD_DIM and l % TOKEN_TILE == 0 and l % (SSM_CHUNK * SCAN_ROWS) == 0
    for i in range(depth):
        mt, gre, gim, hre, him, apr, api = _ssm_prep(
            lam_re[i], lam_im[i], log_step[i], b_re[i], b_im[i], c_re[i], c_im[i])
        qt, k, vt, u, sga, sgb = _in_proj(x, norm_mix_g[i], w_in[i])
        y = _ssm_chunk(u, mt, gre, gim, hre, him, apr, api, d_skip[i])
        oa = _moba_attention(qt, k, vt)
        x = _merge_out(x, oa, sga, sgb, y, w_glu[i], w_out[i])
        x = _mlp_final(x, norm_mlp_g[i], w_up[i], w_down[i], norm_final_g, final_norm=(i == depth - 1))
    return x
```

```python
import functools
import math

import jax
import jax.numpy as jnp
from jax import lax
from jax.experimental import pallas as pl
from jax.experimental.pallas import tpu as pltpu

F32 = jnp.float32
BF16 = jnp.bfloat16

N_HEADS = 8
HEAD_DIM = 128
HALF_DIM = HEAD_DIM // 2
MOBA_BLOCK = 256
MOBA_TOPK = 3
ATTN_HEADS = 4
KV_GROUP = 4
BF16_SUBLANES = 16
V_ROWS = HEAD_DIM + BF16_SUBLANES
ROPE_THETA = 10000.0
SSM_GROUP = 16
SSM_STATE = 64
SSM_CHUNK = 8
CHUNK_W = SSM_CHUNK * SSM_GROUP
SCAN_ROWS = 8
LANE = 128
GROUPS_PER_SLAB = LANE // SSM_GROUP
SLAB_W = SSM_CHUNK * LANE
RMS_EPS = 1e-6
NEG_BIG = -1e30
TOKEN_TILE = 512
VMEM_LIMIT = 56 * 1024 * 1024

_NT = (((1,), (1,)), ((), ()))


def _cparams(*sem):
    return pltpu.CompilerParams(dimension_semantics=sem, vmem_limit_bytes=VMEM_LIMIT)


def _const_spec(shape):
    nd = len(shape)
    return pl.BlockSpec(shape, lambda *_: (0,) * nd)


def _cexp(lr, li, step, e):
    mag = jnp.exp(lr * step * e)
    ang = li * step * e
    return mag * jnp.cos(ang), mag * jnp.sin(ang)


def _spread(rows, cols, target_fn):
    r = lax.broadcasted_iota(jnp.int32, (rows, cols), 0)
    c = lax.broadcasted_iota(jnp.int32, (rows, cols), 1)
    return jnp.where(c == target_fn(r), 1.0, 0.0).astype(F32)


def _ssm_prep_kernel(lrr_ref, lir_ref, lrc_ref, lic_ref, ls_ref, btr_ref, bti_ref, ctr_ref, cti_ref, dt_ref,
                     mx_ref, gx_ref, hx_ref, apx_ref, aix_ref, dx_ref):
    n = lrr_ref.shape[-1]
    gl = lax.rem(pl.program_id(0), GROUPS_PER_SLAB)
    hp = lax.Precision.HIGHEST
    place = functools.partial(jnp.dot, precision=hp, preferred_element_type=F32)
    to_slab = _spread(CHUNK_W, SLAB_W, lambda r: lax.shift_right_logical(r, 4) * LANE + gl * SSM_GROUP + (r & 15))
    to_state = _spread(n, GROUPS_PER_SLAB * n, lambda r: gl * n + r)
    to_state2 = _spread(2 * n, 2 * GROUPS_PER_SLAB * n,
                        lambda r: jnp.where(r >= n, GROUPS_PER_SLAB * n - n, 0) + gl * n + r)
    step = jnp.exp(ls_ref[0])
    lr_r, li_r = lrr_ref[0], lir_ref[0]
    lr_c, li_c = lrc_ref[0], lic_ref[0]
    ar, ai = _cexp(lr_r, li_r, step, 1.0)
    den = lr_r * lr_r + li_r * li_r
    nr, ni = ar - 1.0, ai
    cr = (nr * lr_r + ni * li_r) / den
    ci = (ni * lr_r - nr * li_r) / den
    btr, bti = btr_ref[0], bti_ref[0]
    bbr = cr * btr - ci * bti
    bbi = cr * bti + ci * btr
    s_col = lax.shift_right_logical(lax.broadcasted_iota(jnp.int32, (CHUNK_W, 1), 0), 4)
    e_col = (SSM_CHUNK - 1 - s_col).astype(F32)
    pr, pi_ = _cexp(lr_r, li_r, step, e_col)
    gcat = jnp.concatenate([pr * bbr - pi_ * bbi, pr * bbi + pi_ * bbr], axis=1)
    gwide = place(gcat, to_state2).astype(gx_ref.dtype)
    row0 = pl.multiple_of(gl * SSM_GROUP, SSM_GROUP)
    for s in range(SSM_CHUNK):
        gx_ref[0, pl.ds(s * LANE + row0, SSM_GROUP), :] = gwide[s * SSM_GROUP:(s + 1) * SSM_GROUP]
    tau = lax.shift_right_logical(lax.broadcasted_iota(jnp.int32, (1, CHUNK_W), 1), 4).astype(F32)
    ctr, cti = ctr_ref[0], cti_ref[0]
    wr, wi = _cexp(lr_c, li_c, step, tau)
    p_r = ctr * wr - cti * wi
    p_i = ctr * wi + cti * wr
    r0 = place(bbr[:SSM_GROUP], p_r) - place(bbi[:SSM_GROUP], p_i)
    r0w = place(r0, to_slab)
    lane = lax.broadcasted_iota(jnp.int32, (SSM_GROUP, SLAB_W), 1)
    for s in range(SSM_CHUNK):
        blk = r0w if s == 0 else pltpu.roll(r0w, LANE * s, axis=1)
        blk = jnp.where(lane >= LANE * s, blk, 0.0)
        mx_ref[0, pl.ds(s * LANE + row0, SSM_GROUP), :] = blk.astype(mx_ref.dtype)
    hr, hi = _cexp(lr_c, li_c, step, tau + 1.0)
    st0 = pl.multiple_of(gl * n, n)
    hx_ref[0, pl.ds(st0, n), :] = place(ctr * hr - cti * hi, to_slab).astype(hx_ref.dtype)
    hx_ref[0, pl.ds(GROUPS_PER_SLAB * n + st0, n), :] = place(-(ctr * hi + cti * hr), to_slab).astype(hx_ref.dtype)
    e_row = ((lax.broadcasted_iota(jnp.int32, (SCAN_ROWS, 1), 0) + 1) * SSM_CHUNK).astype(F32)
    qr, qi = _cexp(lr_r, li_r, step, e_row)
    shared = ((apx_ref, place(qr, to_state)), (aix_ref, place(qi, to_state)), (dx_ref, place(dt_ref[0], to_slab)))

    @pl.when(gl == 0)
    def _():
        for ref, val in shared:
            ref[0] = val

    @pl.when(gl != 0)
    def _():
        for ref, val in shared:
            ref[0] = ref[0] + val


def _ssm_prep(lam_re, lam_im, log_step, b_re, b_im, c_re, c_im, d_skip):
    g, n = lam_re.shape
    ns = g // GROUPS_PER_SLAB
    sl = GROUPS_PER_SLAB * n
    bt = lambda b: jnp.tile(jnp.swapaxes(b, 1, 2), (1, SSM_CHUNK, 1))
    ct = lambda m: jnp.tile(jnp.swapaxes(m, 1, 2), (1, 1, SSM_CHUNK))
    args = (lam_re.reshape(g, 1, n), lam_im.reshape(g, 1, n), lam_re.reshape(g, n, 1),
            lam_im.reshape(g, n, 1), log_step.reshape(g, 1, 1), bt(b_re), bt(b_im), ct(c_re), ct(c_im),
            jnp.tile(d_skip.astype(F32), (1, SSM_CHUNK)).reshape(g, 1, CHUNK_W))
    spec = lambda shp: pl.BlockSpec((1,) + shp, lambda i: (i, 0, 0))
    slab = lambda shp: pl.BlockSpec((1,) + shp, lambda i: (i // GROUPS_PER_SLAB, 0, 0))
    in_specs = [spec((1, n)), spec((1, n)), spec((n, 1)), spec((n, 1)), spec((1, 1)),
                spec((CHUNK_W, n)), spec((CHUNK_W, n)), spec((n, CHUNK_W)), spec((n, CHUNK_W)), spec((1, CHUNK_W))]
    out_shapes = [((SLAB_W, SLAB_W), BF16), ((SLAB_W, 2 * sl), BF16), ((2 * sl, SLAB_W), BF16),
                  ((SCAN_ROWS, sl), F32), ((SCAN_ROWS, sl), F32), ((1, SLAB_W), F32)]
    return pl.pallas_call(
        _ssm_prep_kernel,
        grid=(g,),
        in_specs=in_specs,
        out_specs=[slab(s) for s, _ in out_shapes],
        out_shape=[jax.ShapeDtypeStruct((ns,) + s, d) for s, d in out_shapes],
        compiler_params=_cparams("arbitrary"),
        name="ssm_prep",
    )(*args)


def _in_proj_kernel(x_ref, g_ref, wqt_ref, wk_ref, wvt_ref, wu_ref, wg_ref,
                    cos_ref, sin_ref, cost_ref, sint_ref,
                    qt_ref, k_ref, vt_ref, u_ref, sga_ref, sgb_ref):
    x = x_ref[0]
    d = x.shape[-1]
    ms = jnp.mean(x * x, axis=-1, keepdims=True)
    h = (x * lax.rsqrt(ms + RMS_EPS) * g_ref[...]).astype(BF16)
    cos, sin = cos_ref[...], sin_ref[...]
    for c0 in range(0, d, 512):
        p = jnp.dot(h, wk_ref[:, c0:c0 + 512], preferred_element_type=F32)
        for hh in range(512 // HEAD_DIM):
            sl = p[:, hh * HEAD_DIM:(hh + 1) * HEAD_DIM]
            r = sl * cos + pltpu.roll(sl, HALF_DIM, axis=1) * sin
            k_ref[0, :, c0 + hh * HEAD_DIM:c0 + (hh + 1) * HEAD_DIM] = r.astype(BF16)
    cost, sint = cost_ref[...], sint_ref[...]
    scale = HEAD_DIM ** -0.5 * math.log2(math.e)
    for c0 in range(0, d, 256):
        pt = lax.dot_general(wqt_ref[c0:c0 + 256, :], h, _NT, preferred_element_type=F32)
        for hh in range(256 // HEAD_DIM):
            r0 = hh * HEAD_DIM
            x1 = pt[r0:r0 + HALF_DIM]
            x2 = pt[r0 + HALF_DIM:r0 + HEAD_DIM]
            qt_ref[0, c0 + r0:c0 + r0 + HALF_DIM, :] = ((x1 * cost - x2 * sint) * scale).astype(BF16)
            qt_ref[0, c0 + r0 + HALF_DIM:c0 + r0 + HEAD_DIM, :] = ((x2 * cost + x1 * sint) * scale).astype(BF16)
    ones = jnp.ones((BF16_SUBLANES, x.shape[0]), BF16)
    for c0 in range(0, d, 256):
        pv = lax.dot_general(wvt_ref[c0:c0 + 256, :], h, _NT, preferred_element_type=F32).astype(BF16)
        for hh in range(256 // HEAD_DIM):
            r0 = (c0 // HEAD_DIM + hh) * V_ROWS
            vt_ref[0, r0:r0 + HEAD_DIM, :] = pv[hh * HEAD_DIM:(hh + 1) * HEAD_DIM]
            vt_ref[0, r0 + HEAD_DIM:r0 + V_ROWS, :] = ones
    pu = jnp.dot(h, wu_ref[...], preferred_element_type=F32)
    for s in range(pu.shape[1] // LANE):
        u_ref[s, 0] = pu[:, s * LANE:(s + 1) * LANE]
    for c0 in range(0, 2 * d, 512):
        sg = jax.nn.sigmoid(jnp.dot(h, wg_ref[:, c0:c0 + 512], preferred_element_type=F32)).astype(BF16)
        if c0 < d:
            sga_ref[0, :, c0:c0 + 512] = sg
        else:
            sgb_ref[0, :, c0 - d:c0 - d + 512] = sg


def _in_proj(x, g, w_in):
    b, l, d = x.shape
    tm = TOKEN_TILE
    sw = w_in.shape[1] - 5 * d
    wqt = w_in[:, :d].T.astype(BF16)
    wk = w_in[:, d:2 * d].astype(BF16)
    wvt = w_in[:, 2 * d:3 * d].T.astype(BF16)
    wu = w_in[:, 3 * d:3 * d + sw].astype(BF16)
    wg = w_in[:, 3 * d + sw:].astype(BF16)
    inv_freq = ROPE_THETA ** (-jnp.arange(HALF_DIM, dtype=F32) / HALF_DIM)
    ang = jnp.arange(l).astype(F32)[:, None] * inv_freq[None, :]
    cos, sin = jnp.cos(ang), jnp.sin(ang)
    cos2 = jnp.concatenate([cos, cos], axis=1)
    sin2 = jnp.concatenate([-sin, sin], axis=1)
    tok = lambda w: pl.BlockSpec((1, tm, w), lambda bi, i: (bi, i, 0))
    feat = lambda w: pl.BlockSpec((1, w, tm), lambda bi, i: (bi, 0, i))
    sds = jax.ShapeDtypeStruct
    return pl.pallas_call(
        _in_proj_kernel,
        grid=(b, l // tm),
        in_specs=[tok(d), _const_spec((1, d)), _const_spec((d, d)), _const_spec((d, d)), _const_spec((d, d)),
                  _const_spec((d, sw)), _const_spec((d, 2 * d)),
                  pl.BlockSpec((tm, HEAD_DIM), lambda bi, i: (i, 0)),
                  pl.BlockSpec((tm, HEAD_DIM), lambda bi, i: (i, 0)),
                  pl.BlockSpec((HALF_DIM, tm), lambda bi, i: (0, i)),
                  pl.BlockSpec((HALF_DIM, tm), lambda bi, i: (0, i))],
        out_specs=[feat(d), tok(d), feat(N_HEADS * V_ROWS),
                   pl.BlockSpec((sw // LANE, 1, tm, LANE), lambda bi, i: (0, bi, i, 0)), tok(d), tok(d)],
        out_shape=[sds((b, d, l), BF16), sds((b, l, d), BF16), sds((b, N_HEADS * V_ROWS, l), BF16),
                   sds((sw // LANE, b, l, LANE), F32), sds((b, l, d), BF16), sds((b, l, d), BF16)],
        compiler_params=_cparams("parallel", "parallel"),
        name="in_proj",
    )(x, g.reshape(1, d), wqt, wk, wvt, wu, wg, cos2, sin2, cos.T, sin.T)


def _ssm_chunk_kernel(u_ref, mx_ref, gx_ref, hx_ref, apr_ref, api_ref, d_ref,
                      y_ref, ere_ref, eim_ref, xre_ref, xim_ref):
    nk = u_ref.shape[2] // SSM_CHUNK
    sl = ere_ref.shape[1]
    uf = jnp.concatenate([u_ref[0, 0, pl.ds(t, nk, stride=SSM_CHUNK), :] for t in range(SSM_CHUNK)], axis=1)
    u = uf.astype(BF16)
    dot = functools.partial(jnp.dot, preferred_element_type=F32)
    e = dot(u, gx_ref[0])
    ere_ref[...] = e[:, :sl]
    eim_ref[...] = e[:, sl:]
    apr, api = apr_ref[0], api_ref[0]
    row = lax.broadcasted_iota(jnp.int32, apr.shape, 0)
    lvl = [(dd, jnp.broadcast_to(apr[dd - 1:dd], apr.shape), jnp.broadcast_to(api[dd - 1:dd], apr.shape))
           for dd in (1, 2, 4)]

    def tile(t, carry):
        cr, ci = carry
        r0 = pl.multiple_of(t * SCAN_ROWS, SCAN_ROWS)
        zr = ere_ref[pl.ds(r0, SCAN_ROWS), :]
        zi = eim_ref[pl.ds(r0, SCAN_ROWS), :]
        for dd, adr, adi in lvl:
            sr = jnp.where(row >= dd, pltpu.roll(zr, dd, axis=0), 0.0)
            si = jnp.where(row >= dd, pltpu.roll(zi, dd, axis=0), 0.0)
            zr, zi = zr + (adr * sr - adi * si), zi + (adr * si + adi * sr)
        sr = apr * cr - api * ci + zr
        si = apr * ci + api * cr + zi
        xre_ref[pl.ds(r0, SCAN_ROWS), :] = jnp.where(row >= 1, pltpu.roll(sr, 1, axis=0), cr)
        xim_ref[pl.ds(r0, SCAN_ROWS), :] = jnp.where(row >= 1, pltpu.roll(si, 1, axis=0), ci)
        last = SCAN_ROWS - 1
        return (jnp.broadcast_to(sr[last:last + 1], sr.shape), jnp.broadcast_to(si[last:last + 1], si.shape))

    zero = jnp.zeros(apr.shape, F32)
    lax.fori_loop(0, nk // SCAN_ROWS, tile, (zero, zero))
    x0 = jnp.concatenate([xre_ref[...], xim_ref[...]], axis=1).astype(BF16)
    y = jax.nn.gelu(dot(u, mx_ref[0]) + dot(x0, hx_ref[0]) + d_ref[0] * uf)
    for t in range(SSM_CHUNK):
        y_ref[0, 0, pl.ds(t, nk, stride=SSM_CHUNK), :] = y[:, t * LANE:(t + 1) * LANE]


def _ssm_chunk(u4, mx, gx, hx, apx, aix, dx):
    ns, b, l, lane = u4.shape
    nk = l // SSM_CHUNK
    sl = apx.shape[-1]
    per_slab = lambda r, c: pl.BlockSpec((1, r, c), lambda s, bi: (s, 0, 0))
    rows = pl.BlockSpec((1, 1, l, lane), lambda s, bi: (s, bi, 0, 0))
    return pl.pallas_call(
        _ssm_chunk_kernel,
        grid=(ns, b),
        in_specs=[rows, per_slab(SLAB_W, SLAB_W), per_slab(SLAB_W, 2 * sl), per_slab(2 * sl, SLAB_W),
                  per_slab(SCAN_ROWS, sl), per_slab(SCAN_ROWS, sl), per_slab(1, SLAB_W)],
        out_specs=rows,
        out_shape=jax.ShapeDtypeStruct(u4.shape, F32),
        scratch_shapes=[pltpu.VMEM((nk, sl), F32) for _ in range(4)],
        compiler_params=_cparams("parallel", "parallel"),
        name="ssm_chunk",
    )(u4, mx, gx, hx, apx, aix, dx)


def _moba_kernel(qt_ref, k_ref, vt_ref, o_ref, kmean_ref, sel_ref, m_ref, acc_ref, s_ref):
    i = pl.program_id(2)
    blk = MOBA_BLOCK
    nb = k_ref.shape[1] // blk
    heads = range(ATTN_HEADS)
    hsl = lambda hp: slice(hp * HEAD_DIM, (hp + 1) * HEAD_DIM)

    @pl.when(i == 0)
    def _():
        for hp in heads:
            for j in range(nb):
                kb = k_ref[0, j * blk:(j + 1) * blk, hsl(hp)].astype(F32)
                kmean_ref[hp, j:j + 1, :] = jnp.mean(kb, axis=0, keepdims=True)

    qts = [qt_ref[0, hsl(hp), :] for hp in heads]
    for hp in heads:
        gate = jnp.dot(kmean_ref[hp], qts[hp].astype(F32), precision=lax.Precision.HIGHEST,
                       preferred_element_type=F32)
        brow = lax.broadcasted_iota(jnp.int32, gate.shape, 0)
        work = jnp.where(brow < i, gate, -jnp.inf)
        sel = jnp.zeros(gate.shape, F32)
        for _ in range(MOBA_TOPK):
            mx = jnp.max(work, axis=0, keepdims=True)
            idx = jnp.min(jnp.where(work == mx, brow, nb), axis=0, keepdims=True)
            hit = brow == idx
            sel = jnp.where(hit, jnp.where(idx < i, 1.0, 0.0), sel)
            work = jnp.where(hit, -jnp.inf, work)
        sel_ref[hp] = sel

    vsl = lambda hp: slice(hp * V_ROWS, (hp + 1) * V_ROWS)

    d0 = pl.multiple_of(i * blk, blk)
    for hp in heads:
        st = jnp.dot(k_ref[0, pl.ds(d0, blk), hsl(hp)], qts[hp], preferred_element_type=F32)
        krow = lax.broadcasted_iota(jnp.int32, st.shape, 0)
        qcol = lax.broadcasted_iota(jnp.int32, st.shape, 1)
        st = jnp.where(krow <= qcol, st, NEG_BIG)
        m0 = jnp.max(st, axis=0, keepdims=True)
        p = jnp.exp2(st - m0)
        m_ref[hp] = m0
        acc_ref[hp] = jnp.dot(vt_ref[0, vsl(hp), pl.ds(d0, blk)], p.astype(BF16), preferred_element_type=F32)

    gw = KV_GROUP * blk
    last_group = nb // KV_GROUP - 1

    def scores(hp, gidx):
        r0 = pl.multiple_of(gidx * gw, gw)
        s_ref[hp] = jnp.dot(k_ref[0, pl.ds(r0, gw), hsl(hp)], qts[hp], preferred_element_type=F32)

    def weights(hp, gidx):
        sblk = lambda jl: s_ref[hp, jl * blk:(jl + 1) * blk, :]
        picked = [sel_ref[hp, pl.ds(gidx * KV_GROUP + jl, 1), :] > 0.0 for jl in range(KV_GROUP)]
        m_old = m_ref[hp]
        m_new = m_old
        for jl in range(KV_GROUP):
            cmax = jnp.max(sblk(jl), axis=0, keepdims=True)
            m_new = jnp.maximum(m_new, jnp.where(picked[jl], cmax, NEG_BIG))
        m_ref[hp] = m_new
        ps = [jnp.exp2(sblk(jl) - jnp.where(picked[jl], m_new, -NEG_BIG)).astype(BF16) for jl in range(KV_GROUP)]
        return jnp.exp2(m_old - m_new), jnp.concatenate(ps, axis=0)

    def accumulate(hp, gidx, alpha, p):
        r0 = pl.multiple_of(gidx * gw, gw)
        acc_ref[hp] = alpha * acc_ref[hp] + jnp.dot(vt_ref[0, vsl(hp), pl.ds(r0, gw)], p,
                                                    preferred_element_type=F32)

    for hp in heads:
        scores(hp, 0)

    def past_group(gidx, _):
        nxt = jnp.minimum(gidx + 1, last_group)
        for hp in heads:
            alpha, p = weights(hp, gidx)
            accumulate(hp, gidx, alpha, p)
            scores(hp, nxt)
        return 0

    lax.fori_loop(0, (i + KV_GROUP - 1) // KV_GROUP, past_group, 0)
    for hp in heads:
        ot = acc_ref[hp, :HEAD_DIM, :] / acc_ref[hp, HEAD_DIM:HEAD_DIM + 1, :]
        o_ref[0, :, hsl(hp)] = ot.T.astype(o_ref.dtype)


def _moba_attention(qt, k, vt):
    b, l, d = k.shape
    nb = l // MOBA_BLOCK
    assert nb % KV_GROUP == 0
    hw = ATTN_HEADS * HEAD_DIM
    return pl.pallas_call(
        _moba_kernel,
        grid=(b, N_HEADS // ATTN_HEADS, nb),
        in_specs=[pl.BlockSpec((1, hw, MOBA_BLOCK), lambda bi, h, i: (bi, h, i)),
                  pl.BlockSpec((1, l, hw), lambda bi, h, i: (bi, 0, h)),
                  pl.BlockSpec((1, ATTN_HEADS * V_ROWS, l), lambda bi, h, i: (bi, h, 0))],
        out_specs=pl.BlockSpec((1, MOBA_BLOCK, hw), lambda bi, h, i: (bi, i, h)),
        out_shape=jax.ShapeDtypeStruct((b, l, d), BF16),
        scratch_shapes=[pltpu.VMEM((ATTN_HEADS, nb, HEAD_DIM), F32),
                        pltpu.VMEM((ATTN_HEADS, nb, MOBA_BLOCK), F32),
                        pltpu.VMEM((ATTN_HEADS, 1, MOBA_BLOCK), F32),
                        pltpu.VMEM((ATTN_HEADS, V_ROWS, MOBA_BLOCK), F32),
                        pltpu.VMEM((ATTN_HEADS, KV_GROUP * MOBA_BLOCK, MOBA_BLOCK), F32)],
        compiler_params=_cparams("parallel", "parallel", "arbitrary"),
        name="moba_attn",
    )(qt, k, vt)


def _merge_kernel(x_ref, oa_ref, sga_ref, sgb_ref, y_ref, wglu_ref, wout_ref, o_ref):
    d = x_ref.shape[-1]
    y = jnp.concatenate([y_ref[s, 0] for s in range(y_ref.shape[0])], axis=1)
    hgl = jnp.dot(y.astype(BF16), wglu_ref[...], preferred_element_type=F32)
    ob = hgl[:, :d] * jax.nn.sigmoid(hgl[:, d:])
    mixed = sga_ref[0].astype(F32) * oa_ref[0].astype(F32) + sgb_ref[0].astype(F32) * ob
    o_ref[0] = x_ref[0] + jnp.dot(mixed.astype(BF16), wout_ref[...], preferred_element_type=F32)


def _merge_out(x, oa, sga, sgb, y, w_glu, w_out):
    b, l, d = x.shape
    tm = TOKEN_TILE
    ns, lane = y.shape[0], y.shape[-1]
    sw = ns * lane
    tok = lambda w: pl.BlockSpec((1, tm, w), lambda bi, i: (bi, i, 0))
    return pl.pallas_call(
        _merge_kernel,
        grid=(b, l // tm),
        in_specs=[tok(d), tok(d), tok(d), tok(d), pl.BlockSpec((ns, 1, tm, lane), lambda bi, i: (0, bi, i, 0)),
                  _const_spec((sw, 2 * d)), _const_spec((d, d))],
        out_specs=tok(d),
        out_shape=jax.ShapeDtypeStruct((b, l, d), F32),
        compiler_params=_cparams("parallel", "parallel"),
        name="merge_out",
    )(x, oa, sga, sgb, y, w_glu.astype(BF16), w_out.astype(BF16))


def _mlp_kernel(x_ref, g1_ref, wup_ref, wdn_ref, g2_ref, o_ref, *, final_norm):
    x = x_ref[0]
    d = x.shape[-1]
    ff = wup_ref.shape[1]
    ms = jnp.mean(x * x, axis=-1, keepdims=True)
    h = (x * lax.rsqrt(ms + RMS_EPS) * g1_ref[...]).astype(BF16)
    acc = x
    for c0 in range(0, ff, d):
        a = jnp.maximum(jnp.dot(h, wup_ref[:, c0:c0 + d], preferred_element_type=F32), 0.0)
        acc = acc + jnp.dot((a * a).astype(BF16), wdn_ref[c0:c0 + d, :], preferred_element_type=F32)
    if final_norm:
        ms2 = jnp.mean(acc * acc, axis=-1, keepdims=True)
        acc = acc * lax.rsqrt(ms2 + RMS_EPS) * g2_ref[...]
    o_ref[0] = acc


def _mlp_final(x, g1, w_up, w_down, g2, final_norm):
    b, l, d = x.shape
    tm = TOKEN_TILE
    ff = w_up.shape[1]
    tok = pl.BlockSpec((1, tm, d), lambda bi, i: (bi, i, 0))
    return pl.pallas_call(
        functools.partial(_mlp_kernel, final_norm=final_norm),
        grid=(b, l // tm),
        in_specs=[tok, _const_spec((1, d)), _const_spec((d, ff)), _const_spec((ff, d)), _const_spec((1, d))],
        out_specs=tok,
        out_shape=jax.ShapeDtypeStruct((b, l, d), F32),
        compiler_params=_cparams("parallel", "parallel"),
        name="mlp_final",
    )(x, g1.reshape(1, d), w_up.astype(BF16), w_down.astype(BF16), g2.reshape(1, d))


def kernel(x, norm_mix_g, w_in, lam_re, lam_im, log_step, b_re, b_im, c_re, c_im, d_skip, w_glu, w_out,
           norm_mlp_g, w_up, w_down, norm_final_g):
    b, l, d = x.shape
    depth = w_in.shape[0]
    assert d == N_HEADS * HEAD_DIM and l % TOKEN_TILE == 0 and l % (SSM_CHUNK * SCAN_ROWS) == 0
    for i in range(depth):
        slab_mats = _ssm_prep(lam_re[i], lam_im[i], log_step[i], b_re[i], b_im[i], c_re[i], c_im[i], d_skip[i])
        qt, k, vt, u, sga, sgb = _in_proj(x, norm_mix_g[i], w_in[i])
        y = _ssm_chunk(u, *slab_mats)
        oa = _moba_attention(qt, k, vt)
        x = _merge_out(x, oa, sga, sgb, y, w_glu[i], w_out[i])
        x = _mlp_final(x, norm_mlp_g[i], w_up[i], w_down[i], norm_final_g, final_norm=(i == depth - 1))
    return x
```

```python
import functools
import math

import jax
import jax.numpy as jnp
from jax import lax
from jax.experimental import pallas as pl
from jax.experimental.pallas import tpu as pltpu

F32 = jnp.float32
BF16 = jnp.bfloat16

N_HEADS = 8
HEAD_DIM = 128
HALF_DIM = HEAD_DIM // 2
MOBA_BLOCK = 256
MOBA_TOPK = 3
ATTN_HEADS = 2
Q_BLOCKS = 2
KV_GROUP = 4
BF16_SUBLANES = 16
V_ROWS = HEAD_DIM + BF16_SUBLANES
ROPE_THETA = 10000.0
SSM_GROUP = 16
SSM_STATE = 64
SSM_CHUNK = 8
CHUNK_W = SSM_CHUNK * SSM_GROUP
SCAN_ROWS = 8
LANE = 128
GROUPS_PER_SLAB = LANE // SSM_GROUP
SLAB_W = SSM_CHUNK * LANE
RMS_EPS = 1e-6
NEG_BIG = -1e30
TOKEN_TILE = 512
VMEM_LIMIT = 56 * 1024 * 1024

_NT = (((1,), (1,)), ((), ()))


def _cparams(*sem):
    return pltpu.CompilerParams(dimension_semantics=sem, vmem_limit_bytes=VMEM_LIMIT)


def _const_spec(shape):
    nd = len(shape)
    return pl.BlockSpec(shape, lambda *_: (0,) * nd)


def _cexp(lr, li, step, e):
    mag = jnp.exp(lr * step * e)
    ang = li * step * e
    return mag * jnp.cos(ang), mag * jnp.sin(ang)


def _spread(rows, cols, target_fn):
    r = lax.broadcasted_iota(jnp.int32, (rows, cols), 0)
    c = lax.broadcasted_iota(jnp.int32, (rows, cols), 1)
    return jnp.where(c == target_fn(r), 1.0, 0.0).astype(F32)


def _ssm_prep_kernel(lrr_ref, lir_ref, lrc_ref, lic_ref, ls_ref, btr_ref, bti_ref, ctr_ref, cti_ref, dt_ref,
                     mx_ref, gx_ref, hx_ref, apx_ref, aix_ref, dx_ref):
    n = lrr_ref.shape[-1]
    gl = lax.rem(pl.program_id(0), GROUPS_PER_SLAB)
    hp = lax.Precision.HIGHEST
    place = functools.partial(jnp.dot, precision=hp, preferred_element_type=F32)
    to_slab = _spread(CHUNK_W, SLAB_W, lambda r: lax.shift_right_logical(r, 4) * LANE + gl * SSM_GROUP + (r & 15))
    to_state = _spread(n, GROUPS_PER_SLAB * n, lambda r: gl * n + r)
    to_state2 = _spread(2 * n, 2 * GROUPS_PER_SLAB * n,
                        lambda r: jnp.where(r >= n, GROUPS_PER_SLAB * n - n, 0) + gl * n + r)
    step = jnp.exp(ls_ref[0])
    lr_r, li_r = lrr_ref[0], lir_ref[0]
    lr_c, li_c = lrc_ref[0], lic_ref[0]
    ar, ai = _cexp(lr_r, li_r, step, 1.0)
    den = lr_r * lr_r + li_r * li_r
    nr, ni = ar - 1.0, ai
    cr = (nr * lr_r + ni * li_r) / den
    ci = (ni * lr_r - nr * li_r) / den
    btr, bti = btr_ref[0], bti_ref[0]
    bbr = cr * btr - ci * bti
    bbi = cr * bti + ci * btr
    s_col = lax.shift_right_logical(lax.broadcasted_iota(jnp.int32, (CHUNK_W, 1), 0), 4)
    e_col = (SSM_CHUNK - 1 - s_col).astype(F32)
    pr, pi_ = _cexp(lr_r, li_r, step, e_col)
    gcat = jnp.concatenate([pr * bbr - pi_ * bbi, pr * bbi + pi_ * bbr], axis=1)
    gwide = place(gcat, to_state2).astype(gx_ref.dtype)
    row0 = pl.multiple_of(gl * SSM_GROUP, SSM_GROUP)
    for s in range(SSM_CHUNK):
        gx_ref[0, pl.ds(s * LANE + row0, SSM_GROUP), :] = gwide[s * SSM_GROUP:(s + 1) * SSM_GROUP]
    tau = lax.shift_right_logical(lax.broadcasted_iota(jnp.int32, (1, CHUNK_W), 1), 4).astype(F32)
    ctr, cti = ctr_ref[0], cti_ref[0]
    wr, wi = _cexp(lr_c, li_c, step, tau)
    p_r = ctr * wr - cti * wi
    p_i = ctr * wi + cti * wr
    r0 = place(bbr[:SSM_GROUP], p_r) - place(bbi[:SSM_GROUP], p_i)
    r0w = place(r0, to_slab)
    lane = lax.broadcasted_iota(jnp.int32, (SSM_GROUP, SLAB_W), 1)
    for s in range(SSM_CHUNK):
        blk = r0w if s == 0 else pltpu.roll(r0w, LANE * s, axis=1)
        blk = jnp.where(lane >= LANE * s, blk, 0.0)
        mx_ref[0, pl.ds(s * LANE + row0, SSM_GROUP), :] = blk.astype(mx_ref.dtype)
    hr, hi = _cexp(lr_c, li_c, step, tau + 1.0)
    st0 = pl.multiple_of(gl * n, n)
    hx_ref[0, pl.ds(st0, n), :] = place(ctr * hr - cti * hi, to_slab).astype(hx_ref.dtype)
    hx_ref[0, pl.ds(GROUPS_PER_SLAB * n + st0, n), :] = place(-(ctr * hi + cti * hr), to_slab).astype(hx_ref.dtype)
    e_row = ((lax.broadcasted_iota(jnp.int32, (SCAN_ROWS, 1), 0) + 1) * SSM_CHUNK).astype(F32)
    qr, qi = _cexp(lr_r, li_r, step, e_row)
    shared = ((apx_ref, place(qr, to_state)), (aix_ref, place(qi, to_state)), (dx_ref, place(dt_ref[0], to_slab)))

    @pl.when(gl == 0)
    def _():
        for ref, val in shared:
            ref[0] = val

    @pl.when(gl != 0)
    def _():
        for ref, val in shared:
            ref[0] = ref[0] + val


def _ssm_prep(lam_re, lam_im, log_step, b_re, b_im, c_re, c_im, d_skip):
    g, n = lam_re.shape
    ns = g // GROUPS_PER_SLAB
    sl = GROUPS_PER_SLAB * n
    bt = lambda b: jnp.tile(jnp.swapaxes(b, 1, 2), (1, SSM_CHUNK, 1))
    ct = lambda m: jnp.tile(jnp.swapaxes(m, 1, 2), (1, 1, SSM_CHUNK))
    args = (lam_re.reshape(g, 1, n), lam_im.reshape(g, 1, n), lam_re.reshape(g, n, 1),
            lam_im.reshape(g, n, 1), log_step.reshape(g, 1, 1), bt(b_re), bt(b_im), ct(c_re), ct(c_im),
            jnp.tile(d_skip.astype(F32), (1, SSM_CHUNK)).reshape(g, 1, CHUNK_W))
    spec = lambda shp: pl.BlockSpec((1,) + shp, lambda i: (i, 0, 0))
    slab = lambda shp: pl.BlockSpec((1,) + shp, lambda i: (i // GROUPS_PER_SLAB, 0, 0))
    in_specs = [spec((1, n)), spec((1, n)), spec((n, 1)), spec((n, 1)), spec((1, 1)),
                spec((CHUNK_W, n)), spec((CHUNK_W, n)), spec((n, CHUNK_W)), spec((n, CHUNK_W)), spec((1, CHUNK_W))]
    out_shapes = [((SLAB_W, SLAB_W), BF16), ((SLAB_W, 2 * sl), BF16), ((2 * sl, SLAB_W), BF16),
                  ((SCAN_ROWS, sl), F32), ((SCAN_ROWS, sl), F32), ((1, SLAB_W), F32)]
    return pl.pallas_call(
        _ssm_prep_kernel,
        grid=(g,),
        in_specs=in_specs,
        out_specs=[slab(s) for s, _ in out_shapes],
        out_shape=[jax.ShapeDtypeStruct((ns,) + s, d) for s, d in out_shapes],
        compiler_params=_cparams("arbitrary"),
        name="ssm_prep",
    )(*args)


def _in_proj_kernel(x_ref, g_ref, wqt_ref, wk_ref, wvt_ref, wu_ref, wg_ref,
                    cos_ref, sin_ref, cost_ref, sint_ref,
                    qt_ref, k_ref, vt_ref, u_ref, sga_ref, sgb_ref):
    x = x_ref[0]
    d = x.shape[-1]
    ms = jnp.mean(x * x, axis=-1, keepdims=True)
    h = (x * lax.rsqrt(ms + RMS_EPS) * g_ref[...]).astype(BF16)
    cos, sin = cos_ref[...], sin_ref[...]
    for c0 in range(0, d, 512):
        p = jnp.dot(h, wk_ref[:, c0:c0 + 512], preferred_element_type=F32)
        for hh in range(512 // HEAD_DIM):
            sl = p[:, hh * HEAD_DIM:(hh + 1) * HEAD_DIM]
            r = sl * cos + pltpu.roll(sl, HALF_DIM, axis=1) * sin
            k_ref[0, :, c0 + hh * HEAD_DIM:c0 + (hh + 1) * HEAD_DIM] = r.astype(BF16)
    cost, sint = cost_ref[...], sint_ref[...]
    scale = HEAD_DIM ** -0.5 * math.log2(math.e)
    for c0 in range(0, d, 256):
        pt = lax.dot_general(wqt_ref[c0:c0 + 256, :], h, _NT, preferred_element_type=F32)
        for hh in range(256 // HEAD_DIM):
            r0 = hh * HEAD_DIM
            x1 = pt[r0:r0 + HALF_DIM]
            x2 = pt[r0 + HALF_DIM:r0 + HEAD_DIM]
            qt_ref[0, c0 + r0:c0 + r0 + HALF_DIM, :] = ((x1 * cost - x2 * sint) * scale).astype(BF16)
            qt_ref[0, c0 + r0 + HALF_DIM:c0 + r0 + HEAD_DIM, :] = ((x2 * cost + x1 * sint) * scale).astype(BF16)
    ones = jnp.ones((BF16_SUBLANES, x.shape[0]), BF16)
    for c0 in range(0, d, 256):
        pv = lax.dot_general(wvt_ref[c0:c0 + 256, :], h, _NT, preferred_element_type=F32).astype(BF16)
        for hh in range(256 // HEAD_DIM):
            r0 = (c0 // HEAD_DIM + hh) * V_ROWS
            vt_ref[0, r0:r0 + HEAD_DIM, :] = pv[hh * HEAD_DIM:(hh + 1) * HEAD_DIM]
            vt_ref[0, r0 + HEAD_DIM:r0 + V_ROWS, :] = ones
    pu = jnp.dot(h, wu_ref[...], preferred_element_type=F32)
    for s in range(pu.shape[1] // LANE):
        u_ref[s, 0] = pu[:, s * LANE:(s + 1) * LANE]
    for c0 in range(0, 2 * d, 512):
        sg = jax.nn.sigmoid(jnp.dot(h, wg_ref[:, c0:c0 + 512], preferred_element_type=F32)).astype(BF16)
        if c0 < d:
            sga_ref[0, :, c0:c0 + 512] = sg
        else:
            sgb_ref[0, :, c0 - d:c0 - d + 512] = sg


def _in_proj(x, g, w_in):
    b, l, d = x.shape
    tm = TOKEN_TILE
    sw = w_in.shape[1] - 5 * d
    wqt = w_in[:, :d].T.astype(BF16)
    wk = w_in[:, d:2 * d].astype(BF16)
    wvt = w_in[:, 2 * d:3 * d].T.astype(BF16)
    wu = w_in[:, 3 * d:3 * d + sw].astype(BF16)
    wg = w_in[:, 3 * d + sw:].astype(BF16)
    inv_freq = ROPE_THETA ** (-jnp.arange(HALF_DIM, dtype=F32) / HALF_DIM)
    ang = jnp.arange(l).astype(F32)[:, None] * inv_freq[None, :]
    cos, sin = jnp.cos(ang), jnp.sin(ang)
    cos2 = jnp.concatenate([cos, cos], axis=1)
    sin2 = jnp.concatenate([-sin, sin], axis=1)
    tok = lambda w: pl.BlockSpec((1, tm, w), lambda bi, i: (bi, i, 0))
    feat = lambda w: pl.BlockSpec((1, w, tm), lambda bi, i: (bi, 0, i))
    sds = jax.ShapeDtypeStruct
    return pl.pallas_call(
        _in_proj_kernel,
        grid=(b, l // tm),
        in_specs=[tok(d), _const_spec((1, d)), _const_spec((d, d)), _const_spec((d, d)), _const_spec((d, d)),
                  _const_spec((d, sw)), _const_spec((d, 2 * d)),
                  pl.BlockSpec((tm, HEAD_DIM), lambda bi, i: (i, 0)),
                  pl.BlockSpec((tm, HEAD_DIM), lambda bi, i: (i, 0)),
                  pl.BlockSpec((HALF_DIM, tm), lambda bi, i: (0, i)),
                  pl.BlockSpec((HALF_DIM, tm), lambda bi, i: (0, i))],
        out_specs=[feat(d), tok(d), feat(N_HEADS * V_ROWS),
                   pl.BlockSpec((sw // LANE, 1, tm, LANE), lambda bi, i: (0, bi, i, 0)), tok(d), tok(d)],
        out_shape=[sds((b, d, l), BF16), sds((b, l, d), BF16), sds((b, N_HEADS * V_ROWS, l), BF16),
                   sds((sw // LANE, b, l, LANE), F32), sds((b, l, d), BF16), sds((b, l, d), BF16)],
        compiler_params=_cparams("parallel", "parallel"),
        name="in_proj",
    )(x, g.reshape(1, d), wqt, wk, wvt, wu, wg, cos2, sin2, cos.T, sin.T)


def _ssm_chunk_kernel(u_ref, mx_ref, gx_ref, hx_ref, apr_ref, api_ref, d_ref,
                      y_ref, ere_ref, eim_ref, xre_ref, xim_ref):
    nk = u_ref.shape[2] // SSM_CHUNK
    sl = ere_ref.shape[1]
    uf = jnp.concatenate([u_ref[0, 0, pl.ds(t, nk, stride=SSM_CHUNK), :] for t in range(SSM_CHUNK)], axis=1)
    u = uf.astype(BF16)
    dot = functools.partial(jnp.dot, preferred_element_type=F32)
    e = dot(u, gx_ref[0])
    ere_ref[...] = e[:, :sl]
    eim_ref[...] = e[:, sl:]
    apr, api = apr_ref[0], api_ref[0]
    row = lax.broadcasted_iota(jnp.int32, apr.shape, 0)
    lvl = [(dd, jnp.broadcast_to(apr[dd - 1:dd], apr.shape), jnp.broadcast_to(api[dd - 1:dd], apr.shape))
           for dd in (1, 2, 4)]

    def tile(t, carry):
        cr, ci = carry
        r0 = pl.multiple_of(t * SCAN_ROWS, SCAN_ROWS)
        zr = ere_ref[pl.ds(r0, SCAN_ROWS), :]
        zi = eim_ref[pl.ds(r0, SCAN_ROWS), :]
        for dd, adr, adi in lvl:
            sr = jnp.where(row >= dd, pltpu.roll(zr, dd, axis=0), 0.0)
            si = jnp.where(row >= dd, pltpu.roll(zi, dd, axis=0), 0.0)
            zr, zi = zr + (adr * sr - adi * si), zi + (adr * si + adi * sr)
        sr = apr * cr - api * ci + zr
        si = apr * ci + api * cr + zi
        xre_ref[pl.ds(r0, SCAN_ROWS), :] = jnp.where(row >= 1, pltpu.roll(sr, 1, axis=0), cr)
        xim_ref[pl.ds(r0, SCAN_ROWS), :] = jnp.where(row >= 1, pltpu.roll(si, 1, axis=0), ci)
        last = SCAN_ROWS - 1
        return (jnp.broadcast_to(sr[last:last + 1], sr.shape), jnp.broadcast_to(si[last:last + 1], si.shape))

    zero = jnp.zeros(apr.shape, F32)
    lax.fori_loop(0, nk // SCAN_ROWS, tile, (zero, zero))
    x0 = jnp.concatenate([xre_ref[...], xim_ref[...]], axis=1).astype(BF16)
    y = jax.nn.gelu(dot(u, mx_ref[0]) + dot(x0, hx_ref[0]) + d_ref[0] * uf)
    for t in range(SSM_CHUNK):
        y_ref[0, 0, pl.ds(t, nk, stride=SSM_CHUNK), :] = y[:, t * LANE:(t + 1) * LANE]


def _ssm_chunk(u4, mx, gx, hx, apx, aix, dx):
    ns, b, l, lane = u4.shape
    nk = l // SSM_CHUNK
    sl = apx.shape[-1]
    per_slab = lambda r, c: pl.BlockSpec((1, r, c), lambda s, bi: (s, 0, 0))
    rows = pl.BlockSpec((1, 1, l, lane), lambda s, bi: (s, bi, 0, 0))
    return pl.pallas_call(
        _ssm_chunk_kernel,
        grid=(ns, b),
        in_specs=[rows, per_slab(SLAB_W, SLAB_W), per_slab(SLAB_W, 2 * sl), per_slab(2 * sl, SLAB_W),
                  per_slab(SCAN_ROWS, sl), per_slab(SCAN_ROWS, sl), per_slab(1, SLAB_W)],
        out_specs=rows,
        out_shape=jax.ShapeDtypeStruct(u4.shape, F32),
        scratch_shapes=[pltpu.VMEM((nk, sl), F32) for _ in range(4)],
        compiler_params=_cparams("parallel", "parallel"),
        name="ssm_chunk",
    )(u4, mx, gx, hx, apx, aix, dx)


def _moba_kernel(qt_ref, k_ref, vt_ref, o_ref, *scratch):
    i = pl.program_id(2)
    blk = MOBA_BLOCK
    qw = Q_BLOCKS * blk
    nb = k_ref.shape[1] // blk
    fb = i * Q_BLOCKS
    heads = range(ATTN_HEADS)
    hsl = lambda hp: slice(hp * HEAD_DIM, (hp + 1) * HEAD_DIM)
    kmean_ref, sel_ref, m_ref, acc_ref, s_even, s_odd = (
        scratch[r * ATTN_HEADS:(r + 1) * ATTN_HEADS] for r in range(6))
    s_buf = lambda hp, g: (s_odd if g % 2 else s_even)[hp]

    @pl.when(i == 0)
    def _():
        for hp in heads:
            for j in range(nb):
                kb = k_ref[0, j * blk:(j + 1) * blk, hsl(hp)].astype(F32)
                kmean_ref[hp][j:j + 1, :] = jnp.mean(kb, axis=0, keepdims=True)

    vsl = lambda hp: slice(hp * V_ROWS, (hp + 1) * V_ROWS)

    def select_blocks(hp, qt):
        gate = jnp.dot(kmean_ref[hp][...], qt.astype(F32), precision=lax.Precision.HIGHEST,
                       preferred_element_type=F32)
        brow = lax.broadcasted_iota(jnp.int32, gate.shape, 0)
        own = fb + lax.broadcasted_iota(jnp.int32, (1, qw), 1) // blk
        work = jnp.where(brow < own, gate, -jnp.inf)
        sel = jnp.zeros(gate.shape, F32)
        for _ in range(MOBA_TOPK):
            mx = jnp.max(work, axis=0, keepdims=True)
            idx = jnp.min(jnp.where(work == mx, brow, nb), axis=0, keepdims=True)
            hit = brow == idx
            sel = jnp.where(hit, jnp.where(idx < own, 1.0, 0.0), sel)
            work = jnp.where(hit, -jnp.inf, work)
        sel_ref[hp][...] = sel

    def item_blocks(g):
        if g is None:
            return [pl.ds(pl.multiple_of((fb + jl) * blk, blk), blk) for jl in range(Q_BLOCKS)]
        return [slice((g * KV_GROUP + jl) * blk, (g * KV_GROUP + jl + 1) * blk) for jl in range(KV_GROUP)]

    def score_steps(hp, qt, g, cmax):
        buf = s_buf(hp, 0 if g is None else g + 1)

        def step(jl, rows):
            s = jnp.dot(k_ref[0, rows, hsl(hp)], qt, preferred_element_type=F32)
            if g is None:
                krow = lax.broadcasted_iota(jnp.int32, s.shape, 0)
                qcol = lax.broadcasted_iota(jnp.int32, s.shape, 1)
                picked = sel_ref[hp][pl.ds(fb + jl, 1), :] > 0.0
                s = jnp.where(qcol >= (jl + 1) * blk, jnp.where(picked, s, NEG_BIG),
                              jnp.where(krow + jl * blk <= qcol, s, NEG_BIG))
            buf[jl * blk:(jl + 1) * blk, :] = s
            cmax.append(jnp.max(s, axis=0, keepdims=True))

        return [functools.partial(step, jl, rows) for jl, rows in enumerate(item_blocks(g))]

    def weight_steps(hp, g, cmax):
        buf = s_buf(hp, 0 if g is None else g + 1)
        refs = []

        def step(jl, rows):
            if jl == 0:
                if g is None:
                    m_new = functools.reduce(jnp.maximum, cmax)
                    refs.extend(m_new for _ in cmax)
                else:
                    picked = [jnp.logical_and(sel_ref[hp][g * KV_GROUP + j:g * KV_GROUP + j + 1, :] > 0.0,
                                              g * KV_GROUP + j < fb) for j in range(KV_GROUP)]
                    m_old = m_ref[hp][...]
                    m_new = m_old
                    for j in range(KV_GROUP):
                        m_new = jnp.maximum(m_new, jnp.where(picked[j], cmax[j], NEG_BIG))
                    acc_ref[hp][...] = jnp.exp2(m_old - m_new) * acc_ref[hp][...]
                    refs.extend(jnp.where(picked[j], m_new, -NEG_BIG) for j in range(KV_GROUP))
                m_ref[hp][...] = m_new
            p = jnp.exp2(buf[jl * blk:(jl + 1) * blk, :] - refs[jl]).astype(BF16)
            pv = jnp.dot(vt_ref[0, vsl(hp), rows], p, preferred_element_type=F32)
            if g is None and jl == 0:
                acc_ref[hp][...] = pv
            else:
                acc_ref[hp][...] = acc_ref[hp][...] + pv

        return [functools.partial(step, jl, rows) for jl, rows in enumerate(item_blocks(g))]

    def query_tile(ngroups):
        qts = [qt_ref[0, hsl(hp), :] for hp in heads]
        for hp in heads:
            select_blocks(hp, qts[hp])
        items = [(hp, None) for hp in heads] + [(hp, g) for g in range(ngroups) for hp in heads]
        cmaxes = [[] for _ in items]
        for step in score_steps(items[0][0], qts[items[0][0]], items[0][1], cmaxes[0]):
            step()
        for t, (hp, g) in enumerate(items):
            w_steps = weight_steps(hp, g, cmaxes[t])
            s_steps = []
            if t + 1 < len(items):
                nhp, ng_ = items[t + 1]
                s_steps = score_steps(nhp, qts[nhp], ng_, cmaxes[t + 1])
            for n in range(max(len(w_steps), len(s_steps))):
                if n < len(s_steps):
                    s_steps[n]()
                if n < len(w_steps):
                    w_steps[n]()
        for hp in heads:
            ot = acc_ref[hp][:HEAD_DIM, :] / acc_ref[hp][HEAD_DIM:HEAD_DIM + 1, :]
            o_ref[0, :, hsl(hp)] = ot.T.astype(o_ref.dtype)

    lax.switch((fb + KV_GROUP - 1) // KV_GROUP,
               [functools.partial(query_tile, n) for n in range(nb // KV_GROUP + 1)])


def _moba_attention(qt, k, vt):
    b, l, d = k.shape
    nb = l // MOBA_BLOCK
    assert nb % KV_GROUP == 0 and nb % Q_BLOCKS == 0
    hw = ATTN_HEADS * HEAD_DIM
    qw = Q_BLOCKS * MOBA_BLOCK
    return pl.pallas_call(
        _moba_kernel,
        grid=(b, N_HEADS // ATTN_HEADS, nb // Q_BLOCKS),
        in_specs=[pl.BlockSpec((1, hw, qw), lambda bi, h, i: (bi, h, i)),
                  pl.BlockSpec((1, l, hw), lambda bi, h, i: (bi, 0, h)),
                  pl.BlockSpec((1, ATTN_HEADS * V_ROWS, l), lambda bi, h, i: (bi, h, 0))],
        out_specs=pl.BlockSpec((1, qw, hw), lambda bi, h, i: (bi, i, h)),
        out_shape=jax.ShapeDtypeStruct((b, l, d), BF16),
        scratch_shapes=[pltpu.VMEM(shape, F32)
                        for shape in ((nb, HEAD_DIM), (nb, qw), (1, qw), (V_ROWS, qw),
                                      (KV_GROUP * MOBA_BLOCK, qw), (KV_GROUP * MOBA_BLOCK, qw))
                        for _ in range(ATTN_HEADS)],
        compiler_params=_cparams("parallel", "parallel", "arbitrary"),
        name="moba_attn",
    )(qt, k, vt)


def _merge_kernel(x_ref, oa_ref, sga_ref, sgb_ref, y_ref, wglu_ref, wout_ref, o_ref):
    d = x_ref.shape[-1]
    y = jnp.concatenate([y_ref[s, 0] for s in range(y_ref.shape[0])], axis=1)
    hgl = jnp.dot(y.astype(BF16), wglu_ref[...], preferred_element_type=F32)
    ob = hgl[:, :d] * jax.nn.sigmoid(hgl[:, d:])
    mixed = sga_ref[0].astype(F32) * oa_ref[0].astype(F32) + sgb_ref[0].astype(F32) * ob
    o_ref[0] = x_ref[0] + jnp.dot(mixed.astype(BF16), wout_ref[...], preferred_element_type=F32)


def _merge_out(x, oa, sga, sgb, y, w_glu, w_out):
    b, l, d = x.shape
    tm = TOKEN_TILE
    ns, lane = y.shape[0], y.shape[-1]
    sw = ns * lane
    tok = lambda w: pl.BlockSpec((1, tm, w), lambda bi, i: (bi, i, 0))
    return pl.pallas_call(
        _merge_kernel,
        grid=(b, l // tm),
        in_specs=[tok(d), tok(d), tok(d), tok(d), pl.BlockSpec((ns, 1, tm, lane), lambda bi, i: (0, bi, i, 0)),
                  _const_spec((sw, 2 * d)), _const_spec((d, d))],
        out_specs=tok(d),
        out_shape=jax.ShapeDtypeStruct((b, l, d), F32),
        compiler_params=_cparams("parallel", "parallel"),
        name="merge_out",
    )(x, oa, sga, sgb, y, w_glu.astype(BF16), w_out.astype(BF16))


def _mlp_kernel(x_ref, g1_ref, wup_ref, wdn_ref, g2_ref, o_ref, *, final_norm):
    x = x_ref[0]
    d = x.shape[-1]
    ff = wup_ref.shape[1]
    ms = jnp.mean(x * x, axis=-1, keepdims=True)
    h = (x * lax.rsqrt(ms + RMS_EPS) * g1_ref[...]).astype(BF16)
    acc = x
    for c0 in range(0, ff, d):
        a = jnp.maximum(jnp.dot(h, wup_ref[:, c0:c0 + d], preferred_element_type=F32), 0.0)
        acc = acc + jnp.dot((a * a).astype(BF16), wdn_ref[c0:c0 + d, :], preferred_element_type=F32)
    if final_norm:
        ms2 = jnp.mean(acc * acc, axis=-1, keepdims=True)
        acc = acc * lax.rsqrt(ms2 + RMS_EPS) * g2_ref[...]
    o_ref[0] = acc


def _mlp_final(x, g1, w_up, w_down, g2, final_norm):
    b, l, d = x.shape
    tm = TOKEN_TILE
    ff = w_up.shape[1]
    tok = pl.BlockSpec((1, tm, d), lambda bi, i: (bi, i, 0))
    return pl.pallas_call(
        functools.partial(_mlp_kernel, final_norm=final_norm),
        grid=(b, l // tm),
        in_specs=[tok, _const_spec((1, d)), _const_spec((d, ff)), _const_spec((ff, d)), _const_spec((1, d))],
        out_specs=tok,
        out_shape=jax.ShapeDtypeStruct((b, l, d), F32),
        compiler_params=_cparams("parallel", "parallel"),
        name="mlp_final",
    )(x, g1.reshape(1, d), w_up.astype(BF16), w_down.astype(BF16), g2.reshape(1, d))


def kernel(x, norm_mix_g, w_in, lam_re, lam_im, log_step, b_re, b_im, c_re, c_im, d_skip, w_glu, w_out,
           norm_mlp_g, w_up, w_down, norm_final_g):
    b, l, d = x.shape
    depth = w_in.shape[0]
    assert d == N_HEADS * HEAD_DIM and l % TOKEN_TILE == 0 and l % (SSM_CHUNK * SCAN_ROWS) == 0
    for i in range(depth):
        slab_mats = _ssm_prep(lam_re[i], lam_im[i], log_step[i], b_re[i], b_im[i], c_re[i], c_im[i], d_skip[i])
        qt, k, vt, u, sga, sgb = _in_proj(x, norm_mix_g[i], w_in[i])
        y = _ssm_chunk(u, *slab_mats)
        oa = _moba_attention(qt, k, vt)
        x = _merge_out(x, oa, sga, sgb, y, w_glu[i], w_out[i])
        x = _mlp_final(x, norm_mlp_g[i], w_up[i], w_down[i], norm_final_g, final_norm=(i == depth - 1))
    return x
```

```python
import functools
import math

import jax
import jax.numpy as jnp
from jax import lax
from jax.experimental import pallas as pl
from jax.experimental.pallas import tpu as pltpu

F32 = jnp.float32
BF16 = jnp.bfloat16

N_HEADS = 8
HEAD_DIM = 128
HALF_DIM = HEAD_DIM // 2
MOBA_BLOCK = 256
MOBA_TOPK = 3
ATTN_HEADS = 2
Q_BLOCKS = 2
KV_GROUP = 4
BF16_SUBLANES = 16
V_ROWS = HEAD_DIM + BF16_SUBLANES
ROPE_THETA = 10000.0
SSM_GROUP = 16
SSM_STATE = 64
SSM_CHUNK = 8
CHUNK_W = SSM_CHUNK * SSM_GROUP
SCAN_ROWS = 8
LANE = 128
GROUPS_PER_SLAB = LANE // SSM_GROUP
SLAB_W = SSM_CHUNK * LANE
RMS_EPS = 1e-6
NEG_BIG = -1e30
TOKEN_TILE = 512
VMEM_LIMIT = 56 * 1024 * 1024

_NT = (((1,), (1,)), ((), ()))


def _cparams(*sem):
    return pltpu.CompilerParams(dimension_semantics=sem, vmem_limit_bytes=VMEM_LIMIT)


def _const_spec(shape):
    nd = len(shape)
    return pl.BlockSpec(shape, lambda *_: (0,) * nd)


def _cexp(lr, li, step, e):
    mag = jnp.exp(lr * step * e)
    ang = li * step * e
    return mag * jnp.cos(ang), mag * jnp.sin(ang)


def _spread(rows, cols, target_fn):
    r = lax.broadcasted_iota(jnp.int32, (rows, cols), 0)
    c = lax.broadcasted_iota(jnp.int32, (rows, cols), 1)
    return jnp.where(c == target_fn(r), 1.0, 0.0).astype(F32)


def _ssm_prep_kernel(lrr_ref, lir_ref, lrc_ref, lic_ref, ls_ref, btr_ref, bti_ref, ctr_ref, cti_ref, dt_ref,
                     mx_ref, gx_ref, hx_ref, apx_ref, aix_ref, dx_ref):
    n = lrr_ref.shape[-1]
    gl = lax.rem(pl.program_id(0), GROUPS_PER_SLAB)
    hp = lax.Precision.HIGHEST
    place = functools.partial(jnp.dot, precision=hp, preferred_element_type=F32)
    to_slab = _spread(CHUNK_W, SLAB_W, lambda r: lax.shift_right_logical(r, 4) * LANE + gl * SSM_GROUP + (r & 15))
    to_state = _spread(n, GROUPS_PER_SLAB * n, lambda r: gl * n + r)
    to_state2 = _spread(2 * n, 2 * GROUPS_PER_SLAB * n,
                        lambda r: jnp.where(r >= n, GROUPS_PER_SLAB * n - n, 0) + gl * n + r)
    step = jnp.exp(ls_ref[0])
    lr_r, li_r = lrr_ref[0], lir_ref[0]
    lr_c, li_c = lrc_ref[0], lic_ref[0]
    ar, ai = _cexp(lr_r, li_r, step, 1.0)
    den = lr_r * lr_r + li_r * li_r
    nr, ni = ar - 1.0, ai
    cr = (nr * lr_r + ni * li_r) / den
    ci = (ni * lr_r - nr * li_r) / den
    btr, bti = btr_ref[0], bti_ref[0]
    bbr = cr * btr - ci * bti
    bbi = cr * bti + ci * btr
    s_col = lax.shift_right_logical(lax.broadcasted_iota(jnp.int32, (CHUNK_W, 1), 0), 4)
    e_col = (SSM_CHUNK - 1 - s_col).astype(F32)
    pr, pi_ = _cexp(lr_r, li_r, step, e_col)
    gcat = jnp.concatenate([pr * bbr - pi_ * bbi, pr * bbi + pi_ * bbr], axis=1)
    gwide = place(gcat, to_state2).astype(gx_ref.dtype)
    row0 = pl.multiple_of(gl * SSM_GROUP, SSM_GROUP)
    for s in range(SSM_CHUNK):
        gx_ref[0, pl.ds(s * LANE + row0, SSM_GROUP), :] = gwide[s * SSM_GROUP:(s + 1) * SSM_GROUP]
    tau = lax.shift_right_logical(lax.broadcasted_iota(jnp.int32, (1, CHUNK_W), 1), 4).astype(F32)
    ctr, cti = ctr_ref[0], cti_ref[0]
    wr, wi = _cexp(lr_c, li_c, step, tau)
    p_r = ctr * wr - cti * wi
    p_i = ctr * wi + cti * wr
    r0 = place(bbr[:SSM_GROUP], p_r) - place(bbi[:SSM_GROUP], p_i)
    r0w = place(r0, to_slab)
    lane = lax.broadcasted_iota(jnp.int32, (SSM_GROUP, SLAB_W), 1)
    for s in range(SSM_CHUNK):
        blk = r0w if s == 0 else pltpu.roll(r0w, LANE * s, axis=1)
        blk = jnp.where(lane >= LANE * s, blk, 0.0)
        mx_ref[0, pl.ds(s * LANE + row0, SSM_GROUP), :] = blk.astype(mx_ref.dtype)
    hr, hi = _cexp(lr_c, li_c, step, tau + 1.0)
    st0 = pl.multiple_of(gl * n, n)
    hx_ref[0, pl.ds(st0, n), :] = place(ctr * hr - cti * hi, to_slab).astype(hx_ref.dtype)
    hx_ref[0, pl.ds(GROUPS_PER_SLAB * n + st0, n), :] = place(-(ctr * hi + cti * hr), to_slab).astype(hx_ref.dtype)
    e_row = ((lax.broadcasted_iota(jnp.int32, (SCAN_ROWS, 1), 0) + 1) * SSM_CHUNK).astype(F32)
    qr, qi = _cexp(lr_r, li_r, step, e_row)
    shared = ((apx_ref, place(qr, to_state)), (aix_ref, place(qi, to_state)), (dx_ref, place(dt_ref[0], to_slab)))

    @pl.when(gl == 0)
    def _():
        for ref, val in shared:
            ref[0] = val

    @pl.when(gl != 0)
    def _():
        for ref, val in shared:
            ref[0] = ref[0] + val


def _ssm_prep(lam_re, lam_im, log_step, b_re, b_im, c_re, c_im, d_skip):
    g, n = lam_re.shape
    ns = g // GROUPS_PER_SLAB
    sl = GROUPS_PER_SLAB * n
    bt = lambda b: jnp.tile(jnp.swapaxes(b, 1, 2), (1, SSM_CHUNK, 1))
    ct = lambda m: jnp.tile(jnp.swapaxes(m, 1, 2), (1, 1, SSM_CHUNK))
    args = (lam_re.reshape(g, 1, n), lam_im.reshape(g, 1, n), lam_re.reshape(g, n, 1),
            lam_im.reshape(g, n, 1), log_step.reshape(g, 1, 1), bt(b_re), bt(b_im), ct(c_re), ct(c_im),
            jnp.tile(d_skip.astype(F32), (1, SSM_CHUNK)).reshape(g, 1, CHUNK_W))
    spec = lambda shp: pl.BlockSpec((1,) + shp, lambda i: (i, 0, 0))
    slab = lambda shp: pl.BlockSpec((1,) + shp, lambda i: (i // GROUPS_PER_SLAB, 0, 0))
    in_specs = [spec((1, n)), spec((1, n)), spec((n, 1)), spec((n, 1)), spec((1, 1)),
                spec((CHUNK_W, n)), spec((CHUNK_W, n)), spec((n, CHUNK_W)), spec((n, CHUNK_W)), spec((1, CHUNK_W))]
    out_shapes = [((SLAB_W, SLAB_W), BF16), ((SLAB_W, 2 * sl), BF16), ((2 * sl, SLAB_W), BF16),
                  ((SCAN_ROWS, sl), F32), ((SCAN_ROWS, sl), F32), ((1, SLAB_W), F32)]
    return pl.pallas_call(
        _ssm_prep_kernel,
        grid=(g,),
        in_specs=in_specs,
        out_specs=[slab(s) for s, _ in out_shapes],
        out_shape=[jax.ShapeDtypeStruct((ns,) + s, d) for s, d in out_shapes],
        compiler_params=_cparams("arbitrary"),
        name="ssm_prep",
    )(*args)


def _in_proj_kernel(x_ref, g_ref, wqt_ref, wk_ref, wvt_ref, wu_ref, wg_ref,
                    cos_ref, sin_ref, cost_ref, sint_ref,
                    qt_ref, k_ref, vt_ref, u_ref, sga_ref, sgb_ref):
    x = x_ref[0]
    d = x.shape[-1]
    ms = jnp.mean(x * x, axis=-1, keepdims=True)
    h = (x * lax.rsqrt(ms + RMS_EPS) * g_ref[...]).astype(BF16)
    cos, sin = cos_ref[...], sin_ref[...]
    for c0 in range(0, d, 512):
        p = jnp.dot(h, wk_ref[:, c0:c0 + 512], preferred_element_type=F32)
        for hh in range(512 // HEAD_DIM):
            sl = p[:, hh * HEAD_DIM:(hh + 1) * HEAD_DIM]
            r = sl * cos + pltpu.roll(sl, HALF_DIM, axis=1) * sin
            k_ref[0, :, c0 + hh * HEAD_DIM:c0 + (hh + 1) * HEAD_DIM] = r.astype(BF16)
    cost, sint = cost_ref[...], sint_ref[...]
    scale = HEAD_DIM ** -0.5 * math.log2(math.e)
    for c0 in range(0, d, 256):
        pt = lax.dot_general(wqt_ref[c0:c0 + 256, :], h, _NT, preferred_element_type=F32)
        for hh in range(256 // HEAD_DIM):
            r0 = hh * HEAD_DIM
            x1 = pt[r0:r0 + HALF_DIM]
            x2 = pt[r0 + HALF_DIM:r0 + HEAD_DIM]
            qt_ref[0, c0 + r0:c0 + r0 + HALF_DIM, :] = ((x1 * cost - x2 * sint) * scale).astype(BF16)
            qt_ref[0, c0 + r0 + HALF_DIM:c0 + r0 + HEAD_DIM, :] = ((x2 * cost + x1 * sint) * scale).astype(BF16)
    ones = jnp.ones((BF16_SUBLANES, x.shape[0]), BF16)
    for c0 in range(0, d, 256):
        pv = lax.dot_general(wvt_ref[c0:c0 + 256, :], h, _NT, preferred_element_type=F32).astype(BF16)
        for hh in range(256 // HEAD_DIM):
            r0 = (c0 // HEAD_DIM + hh) * V_ROWS
            vt_ref[0, r0:r0 + HEAD_DIM, :] = pv[hh * HEAD_DIM:(hh + 1) * HEAD_DIM]
            vt_ref[0, r0 + HEAD_DIM:r0 + V_ROWS, :] = ones
    pu = jnp.dot(h, wu_ref[...], preferred_element_type=F32)
    for s in range(pu.shape[1] // LANE):
        u_ref[s, 0] = pu[:, s * LANE:(s + 1) * LANE]
    for c0 in range(0, 2 * d, 512):
        sg = jax.nn.sigmoid(jnp.dot(h, wg_ref[:, c0:c0 + 512], preferred_element_type=F32)).astype(BF16)
        if c0 < d:
            sga_ref[0, :, c0:c0 + 512] = sg
        else:
            sgb_ref[0, :, c0 - d:c0 - d + 512] = sg


def _in_proj(x, g, w_in):
    b, l, d = x.shape
    tm = TOKEN_TILE
    sw = w_in.shape[1] - 5 * d
    wqt = w_in[:, :d].T.astype(BF16)
    wk = w_in[:, d:2 * d].astype(BF16)
    wvt = w_in[:, 2 * d:3 * d].T.astype(BF16)
    wu = w_in[:, 3 * d:3 * d + sw].astype(BF16)
    wg = w_in[:, 3 * d + sw:].astype(BF16)
    inv_freq = ROPE_THETA ** (-jnp.arange(HALF_DIM, dtype=F32) / HALF_DIM)
    ang = jnp.arange(l).astype(F32)[:, None] * inv_freq[None, :]
    cos, sin = jnp.cos(ang), jnp.sin(ang)
    cos2 = jnp.concatenate([cos, cos], axis=1)
    sin2 = jnp.concatenate([-sin, sin], axis=1)
    tok = lambda w: pl.BlockSpec((1, tm, w), lambda bi, i: (bi, i, 0))
    feat = lambda w: pl.BlockSpec((1, w, tm), lambda bi, i: (bi, 0, i))
    sds = jax.ShapeDtypeStruct
    return pl.pallas_call(
        _in_proj_kernel,
        grid=(b, l // tm),
        in_specs=[tok(d), _const_spec((1, d)), _const_spec((d, d)), _const_spec((d, d)), _const_spec((d, d)),
                  _const_spec((d, sw)), _const_spec((d, 2 * d)),
                  pl.BlockSpec((tm, HEAD_DIM), lambda bi, i: (i, 0)),
                  pl.BlockSpec((tm, HEAD_DIM), lambda bi, i: (i, 0)),
                  pl.BlockSpec((HALF_DIM, tm), lambda bi, i: (0, i)),
                  pl.BlockSpec((HALF_DIM, tm), lambda bi, i: (0, i))],
        out_specs=[feat(d), tok(d), feat(N_HEADS * V_ROWS),
                   pl.BlockSpec((sw // LANE, 1, tm, LANE), lambda bi, i: (0, bi, i, 0)), tok(d), tok(d)],
        out_shape=[sds((b, d, l), BF16), sds((b, l, d), BF16), sds((b, N_HEADS * V_ROWS, l), BF16),
                   sds((sw // LANE, b, l, LANE), F32), sds((b, l, d), BF16), sds((b, l, d), BF16)],
        compiler_params=_cparams("parallel", "parallel"),
        name="in_proj",
    )(x, g.reshape(1, d), wqt, wk, wvt, wu, wg, cos2, sin2, cos.T, sin.T)


def _ssm_chunk_kernel(u_ref, mx_ref, gx_ref, hx_ref, apr_ref, api_ref, d_ref,
                      y_ref, ere_ref, eim_ref, xre_ref, xim_ref):
    nk = u_ref.shape[2] // SSM_CHUNK
    sl = ere_ref.shape[1]
    uf = jnp.concatenate([u_ref[0, 0, pl.ds(t, nk, stride=SSM_CHUNK), :] for t in range(SSM_CHUNK)], axis=1)
    u = uf.astype(BF16)
    dot = functools.partial(jnp.dot, preferred_element_type=F32)
    e = dot(u, gx_ref[0])
    ere_ref[...] = e[:, :sl]
    eim_ref[...] = e[:, sl:]
    apr, api = apr_ref[0], api_ref[0]
    row = lax.broadcasted_iota(jnp.int32, apr.shape, 0)
    lvl = [(dd, jnp.broadcast_to(apr[dd - 1:dd], apr.shape), jnp.broadcast_to(api[dd - 1:dd], apr.shape))
           for dd in (1, 2, 4)]

    def tile(t, carry):
        cr, ci = carry
        r0 = pl.multiple_of(t * SCAN_ROWS, SCAN_ROWS)
        zr = ere_ref[pl.ds(r0, SCAN_ROWS), :]
        zi = eim_ref[pl.ds(r0, SCAN_ROWS), :]
        for dd, adr, adi in lvl:
            sr = jnp.where(row >= dd, pltpu.roll(zr, dd, axis=0), 0.0)
            si = jnp.where(row >= dd, pltpu.roll(zi, dd, axis=0), 0.0)
            zr, zi = zr + (adr * sr - adi * si), zi + (adr * si + adi * sr)
        sr = apr * cr - api * ci + zr
        si = apr * ci + api * cr + zi
        xre_ref[pl.ds(r0, SCAN_ROWS), :] = jnp.where(row >= 1, pltpu.roll(sr, 1, axis=0), cr)
        xim_ref[pl.ds(r0, SCAN_ROWS), :] = jnp.where(row >= 1, pltpu.roll(si, 1, axis=0), ci)
        last = SCAN_ROWS - 1
        return (jnp.broadcast_to(sr[last:last + 1], sr.shape), jnp.broadcast_to(si[last:last + 1], si.shape))

    zero = jnp.zeros(apr.shape, F32)
    lax.fori_loop(0, nk // SCAN_ROWS, tile, (zero, zero))
    x0 = jnp.concatenate([xre_ref[...], xim_ref[...]], axis=1).astype(BF16)
    y = jax.nn.gelu(dot(u, mx_ref[0]) + dot(x0, hx_ref[0]) + d_ref[0] * uf)
    for t in range(SSM_CHUNK):
        y_ref[0, 0, pl.ds(t, nk, stride=SSM_CHUNK), :] = y[:, t * LANE:(t + 1) * LANE]


def _ssm_chunk(u4, mx, gx, hx, apx, aix, dx):
    ns, b, l, lane = u4.shape
    nk = l // SSM_CHUNK
    sl = apx.shape[-1]
    per_slab = lambda r, c: pl.BlockSpec((1, r, c), lambda s, bi: (s, 0, 0))
    rows = pl.BlockSpec((1, 1, l, lane), lambda s, bi: (s, bi, 0, 0))
    return pl.pallas_call(
        _ssm_chunk_kernel,
        grid=(ns, b),
        in_specs=[rows, per_slab(SLAB_W, SLAB_W), per_slab(SLAB_W, 2 * sl), per_slab(2 * sl, SLAB_W),
                  per_slab(SCAN_ROWS, sl), per_slab(SCAN_ROWS, sl), per_slab(1, SLAB_W)],
        out_specs=rows,
        out_shape=jax.ShapeDtypeStruct(u4.shape, F32),
        scratch_shapes=[pltpu.VMEM((nk, sl), F32) for _ in range(4)],
        compiler_params=_cparams("parallel", "parallel"),
        name="ssm_chunk",
    )(u4, mx, gx, hx, apx, aix, dx)


def _moba_kernel(qt_ref, k_ref, vt_ref, o_ref, *scratch):
    i = pl.program_id(2)
    blk = MOBA_BLOCK
    qw = Q_BLOCKS * blk
    nb = k_ref.shape[1] // blk
    fb = i * Q_BLOCKS
    heads = range(ATTN_HEADS)
    hsl = lambda hp: slice(hp * HEAD_DIM, (hp + 1) * HEAD_DIM)
    kmean_ref, sel_ref, m_ref, acc_ref, s_ref, cm_ref = (
        scratch[r * ATTN_HEADS:(r + 1) * ATTN_HEADS] for r in range(6))

    @pl.when(i == 0)
    def _():
        for hp in heads:
            for j in range(nb):
                kb = k_ref[0, j * blk:(j + 1) * blk, hsl(hp)].astype(F32)
                kmean_ref[hp][j:j + 1, :] = jnp.mean(kb, axis=0, keepdims=True)

    vsl = lambda hp: slice(hp * V_ROWS, (hp + 1) * V_ROWS)

    def select_blocks(hp, qt):
        gate = jnp.dot(kmean_ref[hp][...], qt.astype(F32), precision=lax.Precision.HIGHEST,
                       preferred_element_type=F32)
        brow = lax.broadcasted_iota(jnp.int32, gate.shape, 0)
        own = fb + lax.broadcasted_iota(jnp.int32, (1, qw), 1) // blk
        work = jnp.where(brow < own, gate, -jnp.inf)
        sel = jnp.zeros(gate.shape, F32)
        for _ in range(MOBA_TOPK):
            mx = jnp.max(work, axis=0, keepdims=True)
            idx = jnp.min(jnp.where(work == mx, brow, nb), axis=0, keepdims=True)
            hit = brow == idx
            sel = jnp.where(hit, jnp.where(idx < own, 1.0, 0.0), sel)
            work = jnp.where(hit, -jnp.inf, work)
        sel_ref[hp][...] = sel

    def item_blocks(g):
        first, count = (fb, Q_BLOCKS) if g is None else (g * KV_GROUP, KV_GROUP)
        if isinstance(first, int):
            return [slice((first + jl) * blk, (first + jl + 1) * blk) for jl in range(count)]
        return [pl.ds(pl.multiple_of((first + jl) * blk, blk), blk) for jl in range(count)]

    def score_steps(hp, qt, g):
        buf = s_ref[hp]

        def step(jl, rows):
            s = jnp.dot(k_ref[0, rows, hsl(hp)], qt, preferred_element_type=F32)
            if g is None:
                krow = lax.broadcasted_iota(jnp.int32, s.shape, 0)
                qcol = lax.broadcasted_iota(jnp.int32, s.shape, 1)
                picked = sel_ref[hp][pl.ds(fb + jl, 1), :] > 0.0
                s = jnp.where(qcol >= (jl + 1) * blk, jnp.where(picked, s, NEG_BIG),
                              jnp.where(krow + jl * blk <= qcol, s, NEG_BIG))
            buf[jl * blk:(jl + 1) * blk, :] = s
            cm_ref[hp][jl:jl + 1, :] = jnp.max(s, axis=0, keepdims=True)

        return [functools.partial(step, jl, rows) for jl, rows in enumerate(item_blocks(g))]

    def weight_steps(hp, g):
        buf = s_ref[hp]
        refs = []

        def step(jl, rows):
            if jl == 0:
                cmax = [cm_ref[hp][j:j + 1, :] for j in range(Q_BLOCKS if g is None else KV_GROUP)]
                if g is None:
                    m_new = functools.reduce(jnp.maximum, cmax)
                    refs.extend(m_new for _ in cmax)
                else:
                    picked = [jnp.logical_and(sel_ref[hp][pl.ds(g * KV_GROUP + j, 1), :] > 0.0,
                                              g * KV_GROUP + j < fb) for j in range(KV_GROUP)]
                    m_old = m_ref[hp][...]
                    m_new = m_old
                    for j in range(KV_GROUP):
                        m_new = jnp.maximum(m_new, jnp.where(picked[j], cmax[j], NEG_BIG))
                    acc_ref[hp][...] = jnp.exp2(m_old - m_new) * acc_ref[hp][...]
                    refs.extend(jnp.where(picked[j], m_new, -NEG_BIG) for j in range(KV_GROUP))
                m_ref[hp][...] = m_new
            p = jnp.exp2(buf[jl * blk:(jl + 1) * blk, :] - refs[jl]).astype(BF16)
            pv = jnp.dot(vt_ref[0, vsl(hp), rows], p, preferred_element_type=F32)
            if g is None and jl == 0:
                acc_ref[hp][...] = pv
            else:
                acc_ref[hp][...] = acc_ref[hp][...] + pv

        return [functools.partial(step, jl, rows) for jl, rows in enumerate(item_blocks(g))]

    def interleave(w_steps, s_steps):
        for n in range(max(len(w_steps), len(s_steps))):
            if n < len(s_steps):
                s_steps[n]()
            if n < len(w_steps):
                w_steps[n]()

    qts = [qt_ref[0, hsl(hp), :] for hp in heads]
    for hp in heads:
        select_blocks(hp, qts[hp])
    interleave([], score_steps(0, qts[0], None))
    for hp in heads:
        nhp = (hp + 1) % ATTN_HEADS
        interleave(weight_steps(hp, None), score_steps(nhp, qts[nhp], None if nhp else 0))

    last_group = nb // KV_GROUP - 1

    def past_group(g, _):
        for hp in heads:
            nhp = (hp + 1) % ATTN_HEADS
            interleave(weight_steps(hp, g), score_steps(nhp, qts[nhp], g if nhp else jnp.minimum(g + 1, last_group)))
        return 0

    lax.fori_loop(0, (fb + KV_GROUP - 1) // KV_GROUP, past_group, 0)
    for hp in heads:
        ot = acc_ref[hp][:HEAD_DIM, :] / acc_ref[hp][HEAD_DIM:HEAD_DIM + 1, :]
        o_ref[0, :, hsl(hp)] = ot.T.astype(o_ref.dtype)


def _moba_attention(qt, k, vt):
    b, l, d = k.shape
    nb = l // MOBA_BLOCK
    assert nb % KV_GROUP == 0 and nb % Q_BLOCKS == 0
    hw = ATTN_HEADS * HEAD_DIM
    qw = Q_BLOCKS * MOBA_BLOCK
    return pl.pallas_call(
        _moba_kernel,
        grid=(b, N_HEADS // ATTN_HEADS, nb // Q_BLOCKS),
        in_specs=[pl.BlockSpec((1, hw, qw), lambda bi, h, i: (bi, h, i)),
                  pl.BlockSpec((1, l, hw), lambda bi, h, i: (bi, 0, h)),
                  pl.BlockSpec((1, ATTN_HEADS * V_ROWS, l), lambda bi, h, i: (bi, h, 0))],
        out_specs=pl.BlockSpec((1, qw, hw), lambda bi, h, i: (bi, i, h)),
        out_shape=jax.ShapeDtypeStruct((b, l, d), BF16),
        scratch_shapes=[pltpu.VMEM(shape, F32)
                        for shape in ((nb, HEAD_DIM), (nb, qw), (1, qw), (V_ROWS, qw),
                                      (KV_GROUP * MOBA_BLOCK, qw), (KV_GROUP, qw))
                        for _ in range(ATTN_HEADS)],
        compiler_params=_cparams("parallel", "parallel", "arbitrary"),
        name="moba_attn",
    )(qt, k, vt)


def _merge_kernel(x_ref, oa_ref, sga_ref, sgb_ref, y_ref, wglu_ref, wout_ref, o_ref):
    d = x_ref.shape[-1]
    y = jnp.concatenate([y_ref[s, 0] for s in range(y_ref.shape[0])], axis=1)
    hgl = jnp.dot(y.astype(BF16), wglu_ref[...], preferred_element_type=F32)
    ob = hgl[:, :d] * jax.nn.sigmoid(hgl[:, d:])
    mixed = sga_ref[0].astype(F32) * oa_ref[0].astype(F32) + sgb_ref[0].astype(F32) * ob
    o_ref[0] = x_ref[0] + jnp.dot(mixed.astype(BF16), wout_ref[...], preferred_element_type=F32)


def _merge_out(x, oa, sga, sgb, y, w_glu, w_out):
    b, l, d = x.shape
    tm = TOKEN_TILE
    ns, lane = y.shape[0], y.shape[-1]
    sw = ns * lane
    tok = lambda w: pl.BlockSpec((1, tm, w), lambda bi, i: (bi, i, 0))
    return pl.pallas_call(
        _merge_kernel,
        grid=(b, l // tm),
        in_specs=[tok(d), tok(d), tok(d), tok(d), pl.BlockSpec((ns, 1, tm, lane), lambda bi, i: (0, bi, i, 0)),
                  _const_spec((sw, 2 * d)), _const_spec((d, d))],
        out_specs=tok(d),
        out_shape=jax.ShapeDtypeStruct((b, l, d), F32),
        compiler_params=_cparams("parallel", "parallel"),
        name="merge_out",
    )(x, oa, sga, sgb, y, w_glu.astype(BF16), w_out.astype(BF16))


def _mlp_kernel(x_ref, g1_ref, wup_ref, wdn_ref, g2_ref, o_ref, *, final_norm):
    x = x_ref[0]
    d = x.shape[-1]
    ff = wup_ref.shape[1]
    ms = jnp.mean(x * x, axis=-1, keepdims=True)
    h = (x * lax.rsqrt(ms + RMS_EPS) * g1_ref[...]).astype(BF16)
    acc = x
    for c0 in range(0, ff, d):
        a = jnp.maximum(jnp.dot(h, wup_ref[:, c0:c0 + d], preferred_element_type=F32), 0.0)
        acc = acc + jnp.dot((a * a).astype(BF16), wdn_ref[c0:c0 + d, :], preferred_element_type=F32)
    if final_norm:
        ms2 = jnp.mean(acc * acc, axis=-1, keepdims=True)
        acc = acc * lax.rsqrt(ms2 + RMS_EPS) * g2_ref[...]
    o_ref[0] = acc


def _mlp_final(x, g1, w_up, w_down, g2, final_norm):
    b, l, d = x.shape
    tm = TOKEN_TILE
    ff = w_up.shape[1]
    tok = pl.BlockSpec((1, tm, d), lambda bi, i: (bi, i, 0))
    return pl.pallas_call(
        functools.partial(_mlp_kernel, final_norm=final_norm),
        grid=(b, l // tm),
        in_specs=[tok, _const_spec((1, d)), _const_spec((d, ff)), _const_spec((ff, d)), _const_spec((1, d))],
        out_specs=tok,
        out_shape=jax.ShapeDtypeStruct((b, l, d), F32),
        compiler_params=_cparams("parallel", "parallel"),
        name="mlp_final",
    )(x, g1.reshape(1, d), w_up.astype(BF16), w_down.astype(BF16), g2.reshape(1, d))


def kernel(x, norm_mix_g, w_in, lam_re, lam_im, log_step, b_re, b_im, c_re, c_im, d_skip, w_glu, w_out,
           norm_mlp_g, w_up, w_down, norm_final_g):
    b, l, d = x.shape
    depth = w_in.shape[0]
    assert d == N_HEADS * HEAD_DIM and l % TOKEN_TILE == 0 and l % (SSM_CHUNK * SCAN_ROWS) == 0
    for i in range(depth):
        slab_mats = _ssm_prep(lam_re[i], lam_im[i], log_step[i], b_re[i], b_im[i], c_re[i], c_im[i], d_skip[i])
        qt, k, vt, u, sga, sgb = _in_proj(x, norm_mix_g[i], w_in[i])
        y = _ssm_chunk(u, *slab_mats)
        oa = _moba_attention(qt, k, vt)
        x = _merge_out(x, oa, sga, sgb, y, w_glu[i], w_out[i])
        x = _mlp_final(x, norm_mlp_g[i], w_up[i], w_down[i], norm_final_g, final_norm=(i == depth - 1))
    return x
```

```python
import functools
import math

import jax
import jax.numpy as jnp
from jax import lax
from jax.experimental import pallas as pl
from jax.experimental.pallas import tpu as pltpu

F32 = jnp.float32
BF16 = jnp.bfloat16

N_HEADS = 8
HEAD_DIM = 128
HALF_DIM = HEAD_DIM // 2
MOBA_BLOCK = 256
MOBA_TOPK = 3
ATTN_HEADS = 2
Q_BLOCKS = 2
KV_GROUP = 4
BF16_SUBLANES = 16
V_ROWS = HEAD_DIM + BF16_SUBLANES
ROPE_THETA = 10000.0
SSM_GROUP = 16
SSM_STATE = 64
SSM_CHUNK = 8
CHUNK_W = SSM_CHUNK * SSM_GROUP
SCAN_ROWS = 8
LANE = 128
GROUPS_PER_SLAB = LANE // SSM_GROUP
SLAB_W = SSM_CHUNK * LANE
RMS_EPS = 1e-6
NEG_BIG = -1e30
TOKEN_TILE = 512
VMEM_LIMIT = 56 * 1024 * 1024

_NT = (((1,), (1,)), ((), ()))


def _cparams(*sem):
    return pltpu.CompilerParams(dimension_semantics=sem, vmem_limit_bytes=VMEM_LIMIT)


def _const_spec(shape):
    nd = len(shape)
    return pl.BlockSpec(shape, lambda *_: (0,) * nd)


def _resident_spec(shape):
    nd = len(shape)
    return pl.BlockSpec(shape, lambda *_: (0,) * nd, pipeline_mode=pl.Buffered(1))


def _cexp(lr, li, step, e):
    mag = jnp.exp(lr * step * e)
    ang = li * step * e
    return mag * jnp.cos(ang), mag * jnp.sin(ang)


def _spread(rows, cols, target_fn):
    r = lax.broadcasted_iota(jnp.int32, (rows, cols), 0)
    c = lax.broadcasted_iota(jnp.int32, (rows, cols), 1)
    return jnp.where(c == target_fn(r), 1.0, 0.0).astype(F32)


def _ssm_prep_kernel(lrr_ref, lir_ref, lrc_ref, lic_ref, ls_ref, btr_ref, bti_ref, ctr_ref, cti_ref, dt_ref,
                     mx_ref, gx_ref, hx_ref, apx_ref, aix_ref, dx_ref):
    n = lrr_ref.shape[-1]
    gl = lax.rem(pl.program_id(0), GROUPS_PER_SLAB)
    hp = lax.Precision.HIGHEST
    place = functools.partial(jnp.dot, precision=hp, preferred_element_type=F32)
    to_slab = _spread(CHUNK_W, SLAB_W, lambda r: lax.shift_right_logical(r, 4) * LANE + gl * SSM_GROUP + (r & 15))
    to_state = _spread(n, GROUPS_PER_SLAB * n, lambda r: gl * n + r)
    to_state2 = _spread(2 * n, 2 * GROUPS_PER_SLAB * n,
                        lambda r: jnp.where(r >= n, GROUPS_PER_SLAB * n - n, 0) + gl * n + r)
    step = jnp.exp(ls_ref[0])
    lr_r, li_r = lrr_ref[0], lir_ref[0]
    lr_c, li_c = lrc_ref[0], lic_ref[0]
    ar, ai = _cexp(lr_r, li_r, step, 1.0)
    den = lr_r * lr_r + li_r * li_r
    nr, ni = ar - 1.0, ai
    cr = (nr * lr_r + ni * li_r) / den
    ci = (ni * lr_r - nr * li_r) / den
    btr, bti = btr_ref[0], bti_ref[0]
    bbr = cr * btr - ci * bti
    bbi = cr * bti + ci * btr
    s_col = lax.shift_right_logical(lax.broadcasted_iota(jnp.int32, (CHUNK_W, 1), 0), 4)
    e_col = (SSM_CHUNK - 1 - s_col).astype(F32)
    pr, pi_ = _cexp(lr_r, li_r, step, e_col)
    gcat = jnp.concatenate([pr * bbr - pi_ * bbi, pr * bbi + pi_ * bbr], axis=1)
    gwide = place(gcat, to_state2).astype(gx_ref.dtype)
    row0 = pl.multiple_of(gl * SSM_GROUP, SSM_GROUP)
    for s in range(SSM_CHUNK):
        gx_ref[0, pl.ds(s * LANE + row0, SSM_GROUP), :] = gwide[s * SSM_GROUP:(s + 1) * SSM_GROUP]
    tau = lax.shift_right_logical(lax.broadcasted_iota(jnp.int32, (1, CHUNK_W), 1), 4).astype(F32)
    ctr, cti = ctr_ref[0], cti_ref[0]
    wr, wi = _cexp(lr_c, li_c, step, tau)
    p_r = ctr * wr - cti * wi
    p_i = ctr * wi + cti * wr
    r0 = place(bbr[:SSM_GROUP], p_r) - place(bbi[:SSM_GROUP], p_i)
    r0w = place(r0, to_slab)
    lane = lax.broadcasted_iota(jnp.int32, (SSM_GROUP, SLAB_W), 1)
    for s in range(SSM_CHUNK):
        blk = r0w if s == 0 else pltpu.roll(r0w, LANE * s, axis=1)
        blk = jnp.where(lane >= LANE * s, blk, 0.0)
        mx_ref[0, pl.ds(s * LANE + row0, SSM_GROUP), :] = blk.astype(mx_ref.dtype)
    hr, hi = _cexp(lr_c, li_c, step, tau + 1.0)
    st0 = pl.multiple_of(gl * n, n)
    hx_ref[0, pl.ds(st0, n), :] = place(ctr * hr - cti * hi, to_slab).astype(hx_ref.dtype)
    hx_ref[0, pl.ds(GROUPS_PER_SLAB * n + st0, n), :] = place(-(ctr * hi + cti * hr), to_slab).astype(hx_ref.dtype)
    e_row = ((lax.broadcasted_iota(jnp.int32, (SCAN_ROWS, 1), 0) + 1) * SSM_CHUNK).astype(F32)
    qr, qi = _cexp(lr_r, li_r, step, e_row)
    shared = ((apx_ref, place(qr, to_state)), (aix_ref, place(qi, to_state)), (dx_ref, place(dt_ref[0], to_slab)))

    @pl.when(gl == 0)
    def _():
        for ref, val in shared:
            ref[0] = val

    @pl.when(gl != 0)
    def _():
        for ref, val in shared:
            ref[0] = ref[0] + val


def _ssm_prep(lam_re, lam_im, log_step, b_re, b_im, c_re, c_im, d_skip):
    g, n = lam_re.shape
    ns = g // GROUPS_PER_SLAB
    sl = GROUPS_PER_SLAB * n
    bt = lambda b: jnp.tile(jnp.swapaxes(b, 1, 2), (1, SSM_CHUNK, 1))
    ct = lambda m: jnp.tile(jnp.swapaxes(m, 1, 2), (1, 1, SSM_CHUNK))
    args = (lam_re.reshape(g, 1, n), lam_im.reshape(g, 1, n), lam_re.reshape(g, n, 1),
            lam_im.reshape(g, n, 1), log_step.reshape(g, 1, 1), bt(b_re), bt(b_im), ct(c_re), ct(c_im),
            jnp.tile(d_skip.astype(F32), (1, SSM_CHUNK)).reshape(g, 1, CHUNK_W))
    spec = lambda shp: pl.BlockSpec((1,) + shp, lambda i: (i, 0, 0))
    slab = lambda shp: pl.BlockSpec((1,) + shp, lambda i: (i // GROUPS_PER_SLAB, 0, 0))
    in_specs = [spec((1, n)), spec((1, n)), spec((n, 1)), spec((n, 1)), spec((1, 1)),
                spec((CHUNK_W, n)), spec((CHUNK_W, n)), spec((n, CHUNK_W)), spec((n, CHUNK_W)), spec((1, CHUNK_W))]
    out_shapes = [((SLAB_W, SLAB_W), BF16), ((SLAB_W, 2 * sl), BF16), ((2 * sl, SLAB_W), BF16),
                  ((SCAN_ROWS, sl), F32), ((SCAN_ROWS, sl), F32), ((1, SLAB_W), F32)]
    return pl.pallas_call(
        _ssm_prep_kernel,
        grid=(g,),
        in_specs=in_specs,
        out_specs=[slab(s) for s, _ in out_shapes],
        out_shape=[jax.ShapeDtypeStruct((ns,) + s, d) for s, d in out_shapes],
        compiler_params=_cparams("arbitrary"),
        name="ssm_prep",
    )(*args)


def _in_proj_kernel(x_ref, g_ref, wqt_ref, wk_ref, wvt_ref, wu_ref, wg_ref,
                    cos_ref, sin_ref, cost_ref, sint_ref,
                    qt_ref, k_ref, vt_ref, u_ref, sga_ref, sgb_ref):
    x = x_ref[0]
    d = x.shape[-1]
    ms = jnp.mean(x * x, axis=-1, keepdims=True)
    h = (x * lax.rsqrt(ms + RMS_EPS) * g_ref[...]).astype(BF16)
    cos, sin = cos_ref[...], sin_ref[...]
    for c0 in range(0, d, 512):
        p = jnp.dot(h, wk_ref[:, c0:c0 + 512], preferred_element_type=F32)
        for hh in range(512 // HEAD_DIM):
            sl = p[:, hh * HEAD_DIM:(hh + 1) * HEAD_DIM]
            r = sl * cos + pltpu.roll(sl, HALF_DIM, axis=1) * sin
            k_ref[0, :, c0 + hh * HEAD_DIM:c0 + (hh + 1) * HEAD_DIM] = r.astype(BF16)
    cost, sint = cost_ref[...], sint_ref[...]
    scale = HEAD_DIM ** -0.5 * math.log2(math.e)
    for c0 in range(0, d, 256):
        pt = lax.dot_general(wqt_ref[c0:c0 + 256, :], h, _NT, preferred_element_type=F32)
        for hh in range(256 // HEAD_DIM):
            r0 = hh * HEAD_DIM
            x1 = pt[r0:r0 + HALF_DIM]
            x2 = pt[r0 + HALF_DIM:r0 + HEAD_DIM]
            qt_ref[0, c0 + r0:c0 + r0 + HALF_DIM, :] = ((x1 * cost - x2 * sint) * scale).astype(BF16)
            qt_ref[0, c0 + r0 + HALF_DIM:c0 + r0 + HEAD_DIM, :] = ((x2 * cost + x1 * sint) * scale).astype(BF16)
    ones = jnp.ones((BF16_SUBLANES, x.shape[0]), BF16)
    for c0 in range(0, d, 256):
        pv = lax.dot_general(wvt_ref[c0:c0 + 256, :], h, _NT, preferred_element_type=F32).astype(BF16)
        for hh in range(256 // HEAD_DIM):
            r0 = (c0 // HEAD_DIM + hh) * V_ROWS
            vt_ref[0, r0:r0 + HEAD_DIM, :] = pv[hh * HEAD_DIM:(hh + 1) * HEAD_DIM]
            vt_ref[0, r0 + HEAD_DIM:r0 + V_ROWS, :] = ones
    pu = jnp.dot(h, wu_ref[...], preferred_element_type=F32)
    for s in range(pu.shape[1] // LANE):
        u_ref[s, 0] = pu[:, s * LANE:(s + 1) * LANE]
    for c0 in range(0, 2 * d, 512):
        sg = jax.nn.sigmoid(jnp.dot(h, wg_ref[:, c0:c0 + 512], preferred_element_type=F32)).astype(BF16)
        if c0 < d:
            sga_ref[0, :, c0:c0 + 512] = sg
        else:
            sgb_ref[0, :, c0 - d:c0 - d + 512] = sg


def _in_proj(x, g, w_in):
    b, l, d = x.shape
    tm = TOKEN_TILE
    sw = w_in.shape[1] - 5 * d
    wqt = w_in[:, :d].T.astype(BF16)
    wk = w_in[:, d:2 * d].astype(BF16)
    wvt = w_in[:, 2 * d:3 * d].T.astype(BF16)
    wu = w_in[:, 3 * d:3 * d + sw].astype(BF16)
    wg = w_in[:, 3 * d + sw:].astype(BF16)
    inv_freq = ROPE_THETA ** (-jnp.arange(HALF_DIM, dtype=F32) / HALF_DIM)
    ang = jnp.arange(l).astype(F32)[:, None] * inv_freq[None, :]
    cos, sin = jnp.cos(ang), jnp.sin(ang)
    cos2 = jnp.concatenate([cos, cos], axis=1)
    sin2 = jnp.concatenate([-sin, sin], axis=1)
    tok = lambda w: pl.BlockSpec((1, tm, w), lambda bi, i: (bi, i, 0))
    feat = lambda w: pl.BlockSpec((1, w, tm), lambda bi, i: (bi, 0, i))
    sds = jax.ShapeDtypeStruct
    return pl.pallas_call(
        _in_proj_kernel,
        grid=(b, l // tm),
        in_specs=[tok(d), _const_spec((1, d)), _const_spec((d, d)), _const_spec((d, d)), _const_spec((d, d)),
                  _const_spec((d, sw)), _const_spec((d, 2 * d)),
                  pl.BlockSpec((tm, HEAD_DIM), lambda bi, i: (i, 0)),
                  pl.BlockSpec((tm, HEAD_DIM), lambda bi, i: (i, 0)),
                  pl.BlockSpec((HALF_DIM, tm), lambda bi, i: (0, i)),
                  pl.BlockSpec((HALF_DIM, tm), lambda bi, i: (0, i))],
        out_specs=[feat(d), tok(d), feat(N_HEADS * V_ROWS),
                   pl.BlockSpec((sw // LANE, 1, tm, LANE), lambda bi, i: (0, bi, i, 0)), tok(d), tok(d)],
        out_shape=[sds((b, d, l), BF16), sds((b, l, d), BF16), sds((b, N_HEADS * V_ROWS, l), BF16),
                   sds((sw // LANE, b, l, LANE), F32), sds((b, l, d), BF16), sds((b, l, d), BF16)],
        compiler_params=_cparams("parallel", "parallel"),
        name="in_proj",
    )(x, g.reshape(1, d), wqt, wk, wvt, wu, wg, cos2, sin2, cos.T, sin.T)


def _ssm_chunk_kernel(u_ref, mx_ref, gx_ref, hx_ref, apr_ref, api_ref, d_ref,
                      y_ref, ere_ref, eim_ref, xre_ref, xim_ref):
    nk = u_ref.shape[2] // SSM_CHUNK
    sl = ere_ref.shape[1]
    uf = jnp.concatenate([u_ref[0, 0, pl.ds(t, nk, stride=SSM_CHUNK), :] for t in range(SSM_CHUNK)], axis=1)
    u = uf.astype(BF16)
    dot = functools.partial(jnp.dot, preferred_element_type=F32)
    e = dot(u, gx_ref[0])
    ere_ref[...] = e[:, :sl]
    eim_ref[...] = e[:, sl:]
    apr, api = apr_ref[0], api_ref[0]
    row = lax.broadcasted_iota(jnp.int32, apr.shape, 0)
    lvl = [(dd, jnp.broadcast_to(apr[dd - 1:dd], apr.shape), jnp.broadcast_to(api[dd - 1:dd], apr.shape))
           for dd in (1, 2, 4)]

    def tile(t, carry):
        cr, ci = carry
        r0 = pl.multiple_of(t * SCAN_ROWS, SCAN_ROWS)
        zr = ere_ref[pl.ds(r0, SCAN_ROWS), :]
        zi = eim_ref[pl.ds(r0, SCAN_ROWS), :]
        for dd, adr, adi in lvl:
            sr = jnp.where(row >= dd, pltpu.roll(zr, dd, axis=0), 0.0)
            si = jnp.where(row >= dd, pltpu.roll(zi, dd, axis=0), 0.0)
            zr, zi = zr + (adr * sr - adi * si), zi + (adr * si + adi * sr)
        sr = apr * cr - api * ci + zr
        si = apr * ci + api * cr + zi
        xre_ref[pl.ds(r0, SCAN_ROWS), :] = jnp.where(row >= 1, pltpu.roll(sr, 1, axis=0), cr)
        xim_ref[pl.ds(r0, SCAN_ROWS), :] = jnp.where(row >= 1, pltpu.roll(si, 1, axis=0), ci)
        last = SCAN_ROWS - 1
        return (jnp.broadcast_to(sr[last:last + 1], sr.shape), jnp.broadcast_to(si[last:last + 1], si.shape))

    zero = jnp.zeros(apr.shape, F32)
    lax.fori_loop(0, nk // SCAN_ROWS, tile, (zero, zero))
    x0 = jnp.concatenate([xre_ref[...], xim_ref[...]], axis=1).astype(BF16)
    y = jax.nn.gelu(dot(u, mx_ref[0]) + dot(x0, hx_ref[0]) + d_ref[0] * uf)
    for t in range(SSM_CHUNK):
        y_ref[0, 0, pl.ds(t, nk, stride=SSM_CHUNK), :] = y[:, t * LANE:(t + 1) * LANE]


def _ssm_chunk(u4, mx, gx, hx, apx, aix, dx):
    ns, b, l, lane = u4.shape
    nk = l // SSM_CHUNK
    sl = apx.shape[-1]
    per_slab = lambda r, c: pl.BlockSpec((1, r, c), lambda s, bi: (s, 0, 0))
    rows = pl.BlockSpec((1, 1, l, lane), lambda s, bi: (s, bi, 0, 0))
    return pl.pallas_call(
        _ssm_chunk_kernel,
        grid=(ns, b),
        in_specs=[rows, per_slab(SLAB_W, SLAB_W), per_slab(SLAB_W, 2 * sl), per_slab(2 * sl, SLAB_W),
                  per_slab(SCAN_ROWS, sl), per_slab(SCAN_ROWS, sl), per_slab(1, SLAB_W)],
        out_specs=rows,
        out_shape=jax.ShapeDtypeStruct(u4.shape, F32),
        scratch_shapes=[pltpu.VMEM((nk, sl), F32) for _ in range(4)],
        compiler_params=_cparams("parallel", "parallel"),
        name="ssm_chunk",
    )(u4, mx, gx, hx, apx, aix, dx)


def _moba_kernel(qt_ref, k_ref, vt_ref, o_ref, *scratch):
    i = pl.program_id(2)
    blk = MOBA_BLOCK
    qw = Q_BLOCKS * blk
    nb = k_ref.shape[1] // blk
    fb = i * Q_BLOCKS
    heads = range(ATTN_HEADS)
    hsl = lambda hp: slice(hp * HEAD_DIM, (hp + 1) * HEAD_DIM)
    kmean_ref, sel_ref, m_ref, acc_ref, s_ref, cm_ref = (
        scratch[r * ATTN_HEADS:(r + 1) * ATTN_HEADS] for r in range(6))

    @pl.when(i == 0)
    def _():
        for hp in heads:
            for j in range(nb):
                kb = k_ref[0, j * blk:(j + 1) * blk, hsl(hp)].astype(F32)
                kmean_ref[hp][j:j + 1, :] = jnp.mean(kb, axis=0, keepdims=True)

    vsl = lambda hp: slice(hp * V_ROWS, (hp + 1) * V_ROWS)

    def select_blocks(hp, qt):
        gate = jnp.dot(kmean_ref[hp][...], qt.astype(F32), precision=lax.Precision.HIGHEST,
                       preferred_element_type=F32)
        brow = lax.broadcasted_iota(jnp.int32, gate.shape, 0)
        own = fb + lax.broadcasted_iota(jnp.int32, (1, qw), 1) // blk
        work = jnp.where(brow < own, gate, -jnp.inf)
        sel = jnp.zeros(gate.shape, F32)
        for _ in range(MOBA_TOPK):
            mx = jnp.max(work, axis=0, keepdims=True)
            idx = jnp.min(jnp.where(work == mx, brow, nb), axis=0, keepdims=True)
            hit = brow == idx
            sel = jnp.where(hit, jnp.where(idx < own, 1.0, 0.0), sel)
            work = jnp.where(hit, -jnp.inf, work)
        sel_ref[hp][...] = sel

    def item_blocks(g):
        first, count = (fb, Q_BLOCKS) if g is None else (g * KV_GROUP, KV_GROUP)
        if isinstance(first, int):
            return [slice((first + jl) * blk, (first + jl + 1) * blk) for jl in range(count)]
        return [pl.ds(pl.multiple_of((first + jl) * blk, blk), blk) for jl in range(count)]

    def score_steps(hp, qt, g):
        buf = s_ref[hp]

        def step(jl, rows):
            s = jnp.dot(k_ref[0, rows, hsl(hp)], qt, preferred_element_type=F32)
            if g is None:
                krow = lax.broadcasted_iota(jnp.int32, s.shape, 0)
                qcol = lax.broadcasted_iota(jnp.int32, s.shape, 1)
                picked = sel_ref[hp][pl.ds(fb + jl, 1), :] > 0.0
                s = jnp.where(qcol >= (jl + 1) * blk, jnp.where(picked, s, NEG_BIG),
                              jnp.where(krow + jl * blk <= qcol, s, NEG_BIG))
            buf[jl * blk:(jl + 1) * blk, :] = s
            cm_ref[hp][jl:jl + 1, :] = jnp.max(s, axis=0, keepdims=True)

        return [functools.partial(step, jl, rows) for jl, rows in enumerate(item_blocks(g))]

    def weight_steps(hp, g):
        buf = s_ref[hp]
        refs = []

        def step(jl, rows):
            if jl == 0:
                cmax = [cm_ref[hp][j:j + 1, :] for j in range(Q_BLOCKS if g is None else KV_GROUP)]
                if g is None:
                    m_new = functools.reduce(jnp.maximum, cmax)
                    refs.extend(m_new for _ in cmax)
                else:
                    picked = [jnp.logical_and(sel_ref[hp][pl.ds(g * KV_GROUP + j, 1), :] > 0.0,
                                              g * KV_GROUP + j < fb) for j in range(KV_GROUP)]
                    m_old = m_ref[hp][...]
                    m_new = m_old
                    for j in range(KV_GROUP):
                        m_new = jnp.maximum(m_new, jnp.where(picked[j], cmax[j], NEG_BIG))
                    acc_ref[hp][...] = jnp.exp2(m_old - m_new) * acc_ref[hp][...]
                    refs.extend(jnp.where(picked[j], m_new, -NEG_BIG) for j in range(KV_GROUP))
                m_ref[hp][...] = m_new
            p = jnp.exp2(buf[jl * blk:(jl + 1) * blk, :] - refs[jl]).astype(BF16)
            pv = jnp.dot(vt_ref[0, vsl(hp), rows], p, preferred_element_type=F32)
            if g is None and jl == 0:
                acc_ref[hp][...] = pv
            else:
                acc_ref[hp][...] = acc_ref[hp][...] + pv

        return [functools.partial(step, jl, rows) for jl, rows in enumerate(item_blocks(g))]

    def interleave(w_steps, s_steps):
        for n in range(max(len(w_steps), len(s_steps))):
            if n < len(s_steps):
                s_steps[n]()
            if n < len(w_steps):
                w_steps[n]()

    qts = [qt_ref[0, hsl(hp), :] for hp in heads]
    for hp in heads:
        select_blocks(hp, qts[hp])
    interleave([], score_steps(0, qts[0], None))
    for hp in heads:
        nhp = (hp + 1) % ATTN_HEADS
        interleave(weight_steps(hp, None), score_steps(nhp, qts[nhp], None if nhp else 0))

    ngroups = (fb + KV_GROUP - 1) // KV_GROUP

    def past_group(g, _, last=False):
        for hp in heads:
            nhp = (hp + 1) % ATTN_HEADS
            ahead = [] if (last and not nhp) else score_steps(nhp, qts[nhp], g if nhp else g + 1)
            interleave(weight_steps(hp, g), ahead)
        return 0

    lax.fori_loop(0, ngroups - 1, past_group, 0)

    @pl.when(ngroups > 0)
    def _():
        past_group(ngroups - 1, 0, last=True)

    for hp in heads:
        ot = acc_ref[hp][:HEAD_DIM, :] / acc_ref[hp][HEAD_DIM:HEAD_DIM + 1, :]
        o_ref[0, :, hsl(hp)] = ot.T.astype(o_ref.dtype)


def _moba_attention(qt, k, vt):
    b, l, d = k.shape
    nb = l // MOBA_BLOCK
    assert nb % KV_GROUP == 0 and nb % Q_BLOCKS == 0
    hw = ATTN_HEADS * HEAD_DIM
    qw = Q_BLOCKS * MOBA_BLOCK
    return pl.pallas_call(
        _moba_kernel,
        grid=(b, N_HEADS // ATTN_HEADS, nb // Q_BLOCKS),
        in_specs=[pl.BlockSpec((1, hw, qw), lambda bi, h, i: (bi, h, i)),
                  pl.BlockSpec((1, l, hw), lambda bi, h, i: (bi, 0, h)),
                  pl.BlockSpec((1, ATTN_HEADS * V_ROWS, l), lambda bi, h, i: (bi, h, 0))],
        out_specs=pl.BlockSpec((1, qw, hw), lambda bi, h, i: (bi, i, h)),
        out_shape=jax.ShapeDtypeStruct((b, l, d), BF16),
        scratch_shapes=[pltpu.VMEM(shape, F32)
                        for shape in ((nb, HEAD_DIM), (nb, qw), (1, qw), (V_ROWS, qw),
                                      (KV_GROUP * MOBA_BLOCK, qw), (KV_GROUP, qw))
                        for _ in range(ATTN_HEADS)],
        compiler_params=_cparams("parallel", "parallel", "arbitrary"),
        name="moba_attn",
    )(qt, k, vt)


def _merge_mlp_kernel(x_ref, oa_ref, sga_ref, sgb_ref, y_ref, wglu_ref, wout_ref, g1_ref, wup_ref, wdn_ref, g2_ref,
                      o_ref, *, final_norm):
    d = x_ref.shape[-1]
    ff = wup_ref.shape[1]
    y = jnp.concatenate([y_ref[s, 0] for s in range(y_ref.shape[0])], axis=1)
    hgl = jnp.dot(y.astype(BF16), wglu_ref[...], preferred_element_type=F32)
    ob = hgl[:, :d] * jax.nn.sigmoid(hgl[:, d:])
    mixed = sga_ref[0].astype(F32) * oa_ref[0].astype(F32) + sgb_ref[0].astype(F32) * ob
    x = x_ref[0] + jnp.dot(mixed.astype(BF16), wout_ref[...], preferred_element_type=F32)
    ms = jnp.mean(x * x, axis=-1, keepdims=True)
    h = (x * lax.rsqrt(ms + RMS_EPS) * g1_ref[...]).astype(BF16)
    acc = x
    for c0 in range(0, ff, d):
        a = jnp.maximum(jnp.dot(h, wup_ref[:, c0:c0 + d], preferred_element_type=F32), 0.0)
        acc = acc + jnp.dot((a * a).astype(BF16), wdn_ref[c0:c0 + d, :], preferred_element_type=F32)
    if final_norm:
        ms2 = jnp.mean(acc * acc, axis=-1, keepdims=True)
        acc = acc * lax.rsqrt(ms2 + RMS_EPS) * g2_ref[...]
    o_ref[0] = acc


def _merge_mlp(x, oa, sga, sgb, y, w_glu, w_out, g1, w_up, w_down, g2, final_norm):
    b, l, d = x.shape
    tm = TOKEN_TILE
    ns, lane = y.shape[0], y.shape[-1]
    sw = ns * lane
    ff = w_up.shape[1]
    tok = lambda w: pl.BlockSpec((1, tm, w), lambda bi, i: (bi, i, 0))
    return pl.pallas_call(
        functools.partial(_merge_mlp_kernel, final_norm=final_norm),
        grid=(b, l // tm),
        in_specs=[tok(d), tok(d), tok(d), tok(d), pl.BlockSpec((ns, 1, tm, lane), lambda bi, i: (0, bi, i, 0)),
                  _resident_spec((sw, 2 * d)), _resident_spec((d, d)), _const_spec((1, d)),
                  _resident_spec((d, ff)), _resident_spec((ff, d)), _const_spec((1, d))],
        out_specs=tok(d),
        out_shape=jax.ShapeDtypeStruct((b, l, d), F32),
        compiler_params=_cparams("parallel", "parallel"),
        name="merge_mlp",
    )(x, oa, sga, sgb, y, w_glu.astype(BF16), w_out.astype(BF16), g1.reshape(1, d),
      w_up.astype(BF16), w_down.astype(BF16), g2.reshape(1, d))


def kernel(x, norm_mix_g, w_in, lam_re, lam_im, log_step, b_re, b_im, c_re, c_im, d_skip, w_glu, w_out,
           norm_mlp_g, w_up, w_down, norm_final_g):
    b, l, d = x.shape
    depth = w_in.shape[0]
    assert d == N_HEADS * HEAD_DIM and l % TOKEN_TILE == 0 and l % (SSM_CHUNK * SCAN_ROWS) == 0
    for i in range(depth):
        slab_mats = _ssm_prep(lam_re[i], lam_im[i], log_step[i], b_re[i], b_im[i], c_re[i], c_im[i], d_skip[i])
        qt, k, vt, u, sga, sgb = _in_proj(x, norm_mix_g[i], w_in[i])
        y = _ssm_chunk(u, *slab_mats)
        oa = _moba_attention(qt, k, vt)
        x = _merge_mlp(x, oa, sga, sgb, y, w_glu[i], w_out[i], norm_mlp_g[i], w_up[i], w_down[i], norm_final_g,
                       final_norm=(i == depth - 1))
    return x
```

```python
import functools
import math

import jax
import jax.numpy as jnp
from jax import lax
from jax.experimental import pallas as pl
from jax.experimental.pallas import tpu as pltpu

F32 = jnp.float32
BF16 = jnp.bfloat16

N_HEADS = 8
HEAD_DIM = 128
HALF_DIM = HEAD_DIM // 2
MOBA_BLOCK = 256
MOBA_TOPK = 3
ATTN_HEADS = 2
Q_BLOCKS = 2
KV_GROUP = 4
SCORE_LEAD = 1
BF16_SUBLANES = 16
V_ROWS = HEAD_DIM + BF16_SUBLANES
ROPE_THETA = 10000.0
SSM_GROUP = 16
SSM_STATE = 64
SSM_CHUNK = 8
CHUNK_W = SSM_CHUNK * SSM_GROUP
SCAN_ROWS = 8
LANE = 128
GROUPS_PER_SLAB = LANE // SSM_GROUP
SLAB_W = SSM_CHUNK * LANE
RMS_EPS = 1e-6
NEG_BIG = -1e30
TOKEN_TILE = 512
VMEM_LIMIT = 56 * 1024 * 1024

_NT = (((1,), (1,)), ((), ()))


def _cparams(*sem):
    return pltpu.CompilerParams(dimension_semantics=sem, vmem_limit_bytes=VMEM_LIMIT)


def _const_spec(shape):
    nd = len(shape)
    return pl.BlockSpec(shape, lambda *_: (0,) * nd)


def _resident_spec(shape):
    nd = len(shape)
    return pl.BlockSpec(shape, lambda *_: (0,) * nd, pipeline_mode=pl.Buffered(1))


def _cexp(lr, li, step, e):
    mag = jnp.exp(lr * step * e)
    ang = li * step * e
    return mag * jnp.cos(ang), mag * jnp.sin(ang)


def _spread(rows, cols, target_fn):
    r = lax.broadcasted_iota(jnp.int32, (rows, cols), 0)
    c = lax.broadcasted_iota(jnp.int32, (rows, cols), 1)
    return jnp.where(c == target_fn(r), 1.0, 0.0).astype(F32)


def _ssm_prep_kernel(lrr_ref, lir_ref, ls_ref, btr_ref, bti_ref, ctr_ref, cti_ref, dt_ref,
                     mx_ref, gx_ref, hx_ref, apx_ref, aix_ref, dx_ref):
    n = lrr_ref.shape[-1]
    gl = lax.rem(pl.program_id(0), GROUPS_PER_SLAB)
    hp = lax.Precision.HIGHEST
    place = functools.partial(jnp.dot, precision=hp, preferred_element_type=F32)
    place16 = lambda a, spread: jnp.dot(a.astype(BF16), spread.astype(BF16), preferred_element_type=F32)
    to_slab = _spread(CHUNK_W, SLAB_W, lambda r: lax.shift_right_logical(r, 4) * LANE + gl * SSM_GROUP + (r & 15))
    to_state = _spread(n, GROUPS_PER_SLAB * n, lambda r: gl * n + r)
    to_state2 = _spread(2 * n, 2 * GROUPS_PER_SLAB * n,
                        lambda r: jnp.where(r >= n, GROUPS_PER_SLAB * n - n, 0) + gl * n + r)
    step = jnp.exp(ls_ref[0])
    lr_r, li_r = lrr_ref[0], lir_ref[0]
    prow = lax.broadcasted_iota(jnp.int32, (SSM_CHUNK + SCAN_ROWS, 1), 0)
    pw_r, pw_i = _cexp(lr_r, li_r, step,
                       jnp.where(prow <= SSM_CHUNK, prow, (prow - SSM_CHUNK + 1) * SSM_CHUNK).astype(F32))
    tau = lax.shift_right_logical(lax.broadcasted_iota(jnp.int32, pw_r.shape[:1] + (CHUNK_W,), 1), 4)
    pick = lambda shift: jnp.where(tau + shift == lax.broadcasted_iota(jnp.int32, tau.shape, 0), 1.0, 0.0)
    on_sublanes = lambda p, shift: lax.dot_general(p, pick(shift), (((0,), (0,)), ((), ())), precision=hp,
                                                   preferred_element_type=F32)
    ar, ai = pw_r[1:2], pw_i[1:2]
    den = lr_r * lr_r + li_r * li_r
    nr, ni = ar - 1.0, ai
    cr = (nr * lr_r + ni * li_r) / den
    ci = (ni * lr_r - nr * li_r) / den
    btr, bti = btr_ref[0], bti_ref[0]
    bbr = cr * btr - ci * bti
    bbi = cr * bti + ci * btr
    rows_of = lambda p: jnp.concatenate(
        [jnp.broadcast_to(p[SSM_CHUNK - 1 - s:SSM_CHUNK - s], (SSM_GROUP, n)) for s in range(SSM_CHUNK)], axis=0)
    pr, pi_ = rows_of(pw_r), rows_of(pw_i)
    gcat = jnp.concatenate([pr * bbr - pi_ * bbi, pr * bbi + pi_ * bbr], axis=1)
    gwide = place16(gcat, to_state2).astype(gx_ref.dtype)
    row0 = pl.multiple_of(gl * SSM_GROUP, SSM_GROUP)
    for s in range(SSM_CHUNK):
        gx_ref[0, pl.ds(s * LANE + row0, SSM_GROUP), :] = gwide[s * SSM_GROUP:(s + 1) * SSM_GROUP]
    ctr, cti = ctr_ref[0], cti_ref[0]
    wr, wi = on_sublanes(pw_r, 0), on_sublanes(pw_i, 0)
    p_r = ctr * wr - cti * wi
    p_i = ctr * wi + cti * wr
    r0 = place(bbr[:SSM_GROUP], p_r) - place(bbi[:SSM_GROUP], p_i)
    r0w = place16(r0, to_slab)
    lane = lax.broadcasted_iota(jnp.int32, (SSM_GROUP, SLAB_W), 1)
    for s in range(SSM_CHUNK):
        blk = r0w if s == 0 else pltpu.roll(r0w, LANE * s, axis=1)
        blk = jnp.where(lane >= LANE * s, blk, 0.0)
        mx_ref[0, pl.ds(s * LANE + row0, SSM_GROUP), :] = blk.astype(mx_ref.dtype)
    hr, hi = on_sublanes(pw_r, 1), on_sublanes(pw_i, 1)
    st0 = pl.multiple_of(gl * n, n)
    hx_ref[0, pl.ds(st0, n), :] = place16(ctr * hr - cti * hi, to_slab).astype(hx_ref.dtype)
    hx_ref[0, pl.ds(GROUPS_PER_SLAB * n + st0, n), :] = place16(-(ctr * hi + cti * hr), to_slab).astype(hx_ref.dtype)
    qr, qi = pw_r[SSM_CHUNK:], pw_i[SSM_CHUNK:]
    shared = ((apx_ref, place(qr, to_state)), (aix_ref, place(qi, to_state)), (dx_ref, place(dt_ref[0], to_slab)))

    @pl.when(gl == 0)
    def _():
        for ref, val in shared:
            ref[0] = val

    @pl.when(gl != 0)
    def _():
        for ref, val in shared:
            ref[0] = ref[0] + val


def _ssm_prep(lam_re, lam_im, log_step, b_re, b_im, c_re, c_im, d_skip):
    g, n = lam_re.shape
    ns = g // GROUPS_PER_SLAB
    sl = GROUPS_PER_SLAB * n
    bt = lambda b: jnp.tile(jnp.swapaxes(b, 1, 2), (1, SSM_CHUNK, 1))
    ct = lambda m: jnp.tile(jnp.swapaxes(m, 1, 2), (1, 1, SSM_CHUNK))
    args = (lam_re.reshape(g, 1, n), lam_im.reshape(g, 1, n), log_step.reshape(g, 1, 1),
            bt(b_re), bt(b_im), ct(c_re), ct(c_im),
            jnp.tile(d_skip.astype(F32), (1, SSM_CHUNK)).reshape(g, 1, CHUNK_W))
    spec = lambda shp: pl.BlockSpec((1,) + shp, lambda i: (i, 0, 0))
    slab = lambda shp: pl.BlockSpec((1,) + shp, lambda i: (i // GROUPS_PER_SLAB, 0, 0))
    in_specs = [spec((1, n)), spec((1, n)), spec((1, 1)),
                spec((CHUNK_W, n)), spec((CHUNK_W, n)), spec((n, CHUNK_W)), spec((n, CHUNK_W)), spec((1, CHUNK_W))]
    out_shapes = [((SLAB_W, SLAB_W), BF16), ((SLAB_W, 2 * sl), BF16), ((2 * sl, SLAB_W), BF16),
                  ((SCAN_ROWS, sl), F32), ((SCAN_ROWS, sl), F32), ((1, SLAB_W), F32)]
    return pl.pallas_call(
        _ssm_prep_kernel,
        grid=(g,),
        in_specs=in_specs,
        out_specs=[slab(s) for s, _ in out_shapes],
        out_shape=[jax.ShapeDtypeStruct((ns,) + s, d) for s, d in out_shapes],
        compiler_params=_cparams("arbitrary"),
        name="ssm_prep",
    )(*args)


def _in_proj_kernel(x_ref, g_ref, wqt_ref, wk_ref, wvt_ref, wu_ref, wg_ref,
                    cos_ref, sin_ref, cost_ref, sint_ref,
                    qt_ref, k_ref, vt_ref, u_ref, sga_ref, sgb_ref):
    x = x_ref[0]
    d = x.shape[-1]
    ms = jnp.mean(x * x, axis=-1, keepdims=True)
    h = (x * lax.rsqrt(ms + RMS_EPS) * g_ref[...]).astype(BF16)
    cos, sin = cos_ref[...], sin_ref[...]
    for c0 in range(0, d, 512):
        p = jnp.dot(h, wk_ref[:, c0:c0 + 512], preferred_element_type=F32)
        for hh in range(512 // HEAD_DIM):
            sl = p[:, hh * HEAD_DIM:(hh + 1) * HEAD_DIM]
            r = sl * cos + pltpu.roll(sl, HALF_DIM, axis=1) * sin
            k_ref[0, :, c0 + hh * HEAD_DIM:c0 + (hh + 1) * HEAD_DIM] = r.astype(BF16)
    cost, sint = cost_ref[...], sint_ref[...]
    scale = HEAD_DIM ** -0.5 * math.log2(math.e)
    for c0 in range(0, d, 256):
        pt = lax.dot_general(wqt_ref[c0:c0 + 256, :], h, _NT, preferred_element_type=F32)
        for hh in range(256 // HEAD_DIM):
            r0 = hh * HEAD_DIM
            x1 = pt[r0:r0 + HALF_DIM]
            x2 = pt[r0 + HALF_DIM:r0 + HEAD_DIM]
            qt_ref[0, c0 + r0:c0 + r0 + HALF_DIM, :] = ((x1 * cost - x2 * sint) * scale).astype(BF16)
            qt_ref[0, c0 + r0 + HALF_DIM:c0 + r0 + HEAD_DIM, :] = ((x2 * cost + x1 * sint) * scale).astype(BF16)
    ones = jnp.ones((BF16_SUBLANES, x.shape[0]), BF16)
    for c0 in range(0, d, 256):
        pv = lax.dot_general(wvt_ref[c0:c0 + 256, :], h, _NT, preferred_element_type=F32).astype(BF16)
        for hh in range(256 // HEAD_DIM):
            r0 = (c0 // HEAD_DIM + hh) * V_ROWS
            vt_ref[0, r0:r0 + HEAD_DIM, :] = pv[hh * HEAD_DIM:(hh + 1) * HEAD_DIM]
            vt_ref[0, r0 + HEAD_DIM:r0 + V_ROWS, :] = ones
    pu = jnp.dot(h, wu_ref[...], preferred_element_type=F32)
    for s in range(pu.shape[1] // LANE):
        u_ref[s, 0] = pu[:, s * LANE:(s + 1) * LANE]
    for c0 in range(0, 2 * d, 512):
        sg = jax.nn.sigmoid(jnp.dot(h, wg_ref[:, c0:c0 + 512], preferred_element_type=F32)).astype(BF16)
        if c0 < d:
            sga_ref[0, :, c0:c0 + 512] = sg
        else:
            sgb_ref[0, :, c0 - d:c0 - d + 512] = sg


def _in_proj(x, g, w_in):
    b, l, d = x.shape
    tm = TOKEN_TILE
    sw = w_in.shape[1] - 5 * d
    wqt = w_in[:, :d].T.astype(BF16)
    wk = w_in[:, d:2 * d].astype(BF16)
    wvt = w_in[:, 2 * d:3 * d].T.astype(BF16)
    wu = w_in[:, 3 * d:3 * d + sw].astype(BF16)
    wg = w_in[:, 3 * d + sw:].astype(BF16)
    inv_freq = ROPE_THETA ** (-jnp.arange(HALF_DIM, dtype=F32) / HALF_DIM)
    ang = jnp.arange(l).astype(F32)[:, None] * inv_freq[None, :]
    cos, sin = jnp.cos(ang), jnp.sin(ang)
    cos2 = jnp.concatenate([cos, cos], axis=1)
    sin2 = jnp.concatenate([-sin, sin], axis=1)
    tok = lambda w: pl.BlockSpec((1, tm, w), lambda bi, i: (bi, i, 0))
    feat = lambda w: pl.BlockSpec((1, w, tm), lambda bi, i: (bi, 0, i))
    sds = jax.ShapeDtypeStruct
    return pl.pallas_call(
        _in_proj_kernel,
        grid=(b, l // tm),
        in_specs=[tok(d), _const_spec((1, d)), _const_spec((d, d)), _const_spec((d, d)), _const_spec((d, d)),
                  _const_spec((d, sw)), _const_spec((d, 2 * d)),
                  pl.BlockSpec((tm, HEAD_DIM), lambda bi, i: (i, 0)),
                  pl.BlockSpec((tm, HEAD_DIM), lambda bi, i: (i, 0)),
                  pl.BlockSpec((HALF_DIM, tm), lambda bi, i: (0, i)),
                  pl.BlockSpec((HALF_DIM, tm), lambda bi, i: (0, i))],
        out_specs=[feat(d), tok(d), feat(N_HEADS * V_ROWS),
                   pl.BlockSpec((sw // LANE, 1, tm, LANE), lambda bi, i: (0, bi, i, 0)), tok(d), tok(d)],
        out_shape=[sds((b, d, l), BF16), sds((b, l, d), BF16), sds((b, N_HEADS * V_ROWS, l), BF16),
                   sds((sw // LANE, b, l, LANE), F32), sds((b, l, d), BF16), sds((b, l, d), BF16)],
        compiler_params=_cparams("parallel", "parallel"),
        name="in_proj",
    )(x, g.reshape(1, d), wqt, wk, wvt, wu, wg, cos2, sin2, cos.T, sin.T)


def _ssm_chunk_kernel(u_ref, mx_ref, gx_ref, hx_ref, apr_ref, api_ref, d_ref,
                      y_ref, ere_ref, eim_ref, xre_ref, xim_ref):
    nk = u_ref.shape[2] // SSM_CHUNK
    sl = ere_ref.shape[1]
    uf = jnp.concatenate([u_ref[0, 0, pl.ds(t, nk, stride=SSM_CHUNK), :] for t in range(SSM_CHUNK)], axis=1)
    u = uf.astype(BF16)
    dot = functools.partial(jnp.dot, preferred_element_type=F32)
    e = dot(u, gx_ref[0])
    ere_ref[...] = e[:, :sl]
    eim_ref[...] = e[:, sl:]
    apr, api = apr_ref[0], api_ref[0]
    row = lax.broadcasted_iota(jnp.int32, apr.shape, 0)
    lvl = [(dd, jnp.broadcast_to(apr[dd - 1:dd], apr.shape), jnp.broadcast_to(api[dd - 1:dd], apr.shape))
           for dd in (1, 2, 4)]

    def tile(t, carry):
        cr, ci = carry
        r0 = pl.multiple_of(t * SCAN_ROWS, SCAN_ROWS)
        zr = ere_ref[pl.ds(r0, SCAN_ROWS), :]
        zi = eim_ref[pl.ds(r0, SCAN_ROWS), :]
        for dd, adr, adi in lvl:
            sr = jnp.where(row >= dd, pltpu.roll(zr, dd, axis=0), 0.0)
            si = jnp.where(row >= dd, pltpu.roll(zi, dd, axis=0), 0.0)
            zr, zi = zr + (adr * sr - adi * si), zi + (adr * si + adi * sr)
        sr = apr * cr - api * ci + zr
        si = apr * ci + api * cr + zi
        xre_ref[pl.ds(r0, SCAN_ROWS), :] = jnp.where(row >= 1, pltpu.roll(sr, 1, axis=0), cr)
        xim_ref[pl.ds(r0, SCAN_ROWS), :] = jnp.where(row >= 1, pltpu.roll(si, 1, axis=0), ci)
        last = SCAN_ROWS - 1
        return (jnp.broadcast_to(sr[last:last + 1], sr.shape), jnp.broadcast_to(si[last:last + 1], si.shape))

    zero = jnp.zeros(apr.shape, F32)
    lax.fori_loop(0, nk // SCAN_ROWS, tile, (zero, zero))
    x0 = jnp.concatenate([xre_ref[...], xim_ref[...]], axis=1).astype(BF16)
    y = jax.nn.gelu(dot(u, mx_ref[0]) + dot(x0, hx_ref[0]) + d_ref[0] * uf)
    for t in range(SSM_CHUNK):
        y_ref[0, 0, pl.ds(t, nk, stride=SSM_CHUNK), :] = y[:, t * LANE:(t + 1) * LANE]


def _ssm_chunk(u4, mx, gx, hx, apx, aix, dx):
    ns, b, l, lane = u4.shape
    nk = l // SSM_CHUNK
    sl = apx.shape[-1]
    per_slab = lambda r, c: pl.BlockSpec((1, r, c), lambda s, bi: (s, 0, 0))
    rows = pl.BlockSpec((1, 1, l, lane), lambda s, bi: (s, bi, 0, 0))
    return pl.pallas_call(
        _ssm_chunk_kernel,
        grid=(ns, b),
        in_specs=[rows, per_slab(SLAB_W, SLAB_W), per_slab(SLAB_W, 2 * sl), per_slab(2 * sl, SLAB_W),
                  per_slab(SCAN_ROWS, sl), per_slab(SCAN_ROWS, sl), per_slab(1, SLAB_W)],
        out_specs=rows,
        out_shape=jax.ShapeDtypeStruct(u4.shape, F32),
        scratch_shapes=[pltpu.VMEM((nk, sl), F32) for _ in range(4)],
        compiler_params=_cparams("parallel", "parallel"),
        name="ssm_chunk",
    )(u4, mx, gx, hx, apx, aix, dx)


def _moba_kernel(qt_ref, k_ref, vt_ref, o_ref, *scratch):
    i = pl.program_id(2)
    blk = MOBA_BLOCK
    qw = Q_BLOCKS * blk
    nb = k_ref.shape[1] // blk
    fb = i * Q_BLOCKS
    heads = range(ATTN_HEADS)
    hsl = lambda hp: slice(hp * HEAD_DIM, (hp + 1) * HEAD_DIM)
    kmean_ref, sel_ref, m_ref, acc_ref, s_ref, cm_ref = (
        scratch[r * ATTN_HEADS:(r + 1) * ATTN_HEADS] for r in range(6))

    @pl.when(i == 0)
    def _():
        for hp in heads:
            for j in range(nb):
                kb = k_ref[0, j * blk:(j + 1) * blk, hsl(hp)].astype(F32)
                kmean_ref[hp][j:j + 1, :] = jnp.mean(kb, axis=0, keepdims=True)

    vsl = lambda hp: slice(hp * V_ROWS, (hp + 1) * V_ROWS)

    def select_blocks(hp, qt):
        gate = jnp.dot(kmean_ref[hp][...], qt.astype(F32), precision=lax.Precision.HIGHEST,
                       preferred_element_type=F32)
        brow = lax.broadcasted_iota(jnp.int32, gate.shape, 0)
        own = fb + lax.broadcasted_iota(jnp.int32, (1, qw), 1) // blk
        work = jnp.where(brow < own, gate, -jnp.inf)
        sel = jnp.zeros(gate.shape, F32)
        for _ in range(MOBA_TOPK):
            mx = jnp.max(work, axis=0, keepdims=True)
            idx = jnp.min(jnp.where(work == mx, brow, nb), axis=0, keepdims=True)
            hit = brow == idx
            sel = jnp.where(hit, jnp.where(idx < own, 1.0, 0.0), sel)
            work = jnp.where(hit, -jnp.inf, work)
        sel_ref[hp][...] = sel

    def item_blocks(g):
        first, count = (fb, Q_BLOCKS) if g is None else (g * KV_GROUP, KV_GROUP)
        if isinstance(first, int):
            return [slice((first + jl) * blk, (first + jl + 1) * blk) for jl in range(count)]
        return [pl.ds(pl.multiple_of((first + jl) * blk, blk), blk) for jl in range(count)]

    def score_steps(hp, qt, g):
        buf = s_ref[hp]

        def step(jl, rows):
            s = jnp.dot(k_ref[0, rows, hsl(hp)], qt, preferred_element_type=F32)
            if g is None:
                krow = lax.broadcasted_iota(jnp.int32, s.shape, 0)
                qcol = lax.broadcasted_iota(jnp.int32, s.shape, 1)
                picked = sel_ref[hp][pl.ds(fb + jl, 1), :] > 0.0
                s = jnp.where(qcol >= (jl + 1) * blk, jnp.where(picked, s, NEG_BIG),
                              jnp.where(krow + jl * blk <= qcol, s, NEG_BIG))
            buf[jl * blk:(jl + 1) * blk, :] = s
            cm_ref[hp][jl:jl + 1, :] = jnp.max(s, axis=0, keepdims=True)

        return [functools.partial(step, jl, rows) for jl, rows in enumerate(item_blocks(g))]

    def weight_steps(hp, g):
        buf = s_ref[hp]
        refs = []

        def step(jl, rows):
            if jl == 0:
                cmax = [cm_ref[hp][j:j + 1, :] for j in range(Q_BLOCKS if g is None else KV_GROUP)]
                if g is None:
                    m_new = functools.reduce(jnp.maximum, cmax)
                    refs.extend(m_new for _ in cmax)
                else:
                    picked = [jnp.logical_and(sel_ref[hp][pl.ds(g * KV_GROUP + j, 1), :] > 0.0,
                                              g * KV_GROUP + j < fb) for j in range(KV_GROUP)]
                    m_old = m_ref[hp][...]
                    m_new = m_old
                    for j in range(KV_GROUP):
                        m_new = jnp.maximum(m_new, jnp.where(picked[j], cmax[j], NEG_BIG))
                    acc_ref[hp][...] = jnp.exp2(m_old - m_new) * acc_ref[hp][...]
                    refs.extend(jnp.where(picked[j], m_new, -NEG_BIG) for j in range(KV_GROUP))
                m_ref[hp][...] = m_new
            p = jnp.exp2(buf[jl * blk:(jl + 1) * blk, :] - refs[jl]).astype(BF16)
            pv = jnp.dot(vt_ref[0, vsl(hp), rows], p, preferred_element_type=F32)
            if g is None and jl == 0:
                acc_ref[hp][...] = pv
            else:
                acc_ref[hp][...] = acc_ref[hp][...] + pv

        return [functools.partial(step, jl, rows) for jl, rows in enumerate(item_blocks(g))]

    def interleave(w_steps, s_steps):
        for n in range(max(len(w_steps) + SCORE_LEAD, len(s_steps))):
            if n < len(s_steps):
                s_steps[n]()
            if SCORE_LEAD <= n < len(w_steps) + SCORE_LEAD:
                w_steps[n - SCORE_LEAD]()

    qts = [qt_ref[0, hsl(hp), :] for hp in heads]
    for hp in heads:
        select_blocks(hp, qts[hp])
    interleave([], score_steps(0, qts[0], None))
    for hp in heads:
        nhp = (hp + 1) % ATTN_HEADS
        interleave(weight_steps(hp, None), score_steps(nhp, qts[nhp], None if nhp else 0))

    ngroups = (fb + KV_GROUP - 1) // KV_GROUP

    def past_group(g, _, last=False):
        for hp in heads:
            nhp = (hp + 1) % ATTN_HEADS
            ahead = [] if (last and not nhp) else score_steps(nhp, qts[nhp], g if nhp else g + 1)
            interleave(weight_steps(hp, g), ahead)
        return 0

    lax.fori_loop(0, ngroups - 1, past_group, 0)

    @pl.when(ngroups > 0)
    def _():
        past_group(ngroups - 1, 0, last=True)

    for hp in heads:
        ot = acc_ref[hp][:HEAD_DIM, :] / acc_ref[hp][HEAD_DIM:HEAD_DIM + 1, :]
        o_ref[0, :, hsl(hp)] = ot.T.astype(o_ref.dtype)


def _moba_attention(qt, k, vt):
    b, l, d = k.shape
    nb = l // MOBA_BLOCK
    assert nb % KV_GROUP == 0 and nb % Q_BLOCKS == 0
    hw = ATTN_HEADS * HEAD_DIM
    qw = Q_BLOCKS * MOBA_BLOCK
    return pl.pallas_call(
        _moba_kernel,
        grid=(b, N_HEADS // ATTN_HEADS, nb // Q_BLOCKS),
        in_specs=[pl.BlockSpec((1, hw, qw), lambda bi, h, i: (bi, h, i)),
                  pl.BlockSpec((1, l, hw), lambda bi, h, i: (bi, 0, h)),
                  pl.BlockSpec((1, ATTN_HEADS * V_ROWS, l), lambda bi, h, i: (bi, h, 0))],
        out_specs=pl.BlockSpec((1, qw, hw), lambda bi, h, i: (bi, i, h)),
        out_shape=jax.ShapeDtypeStruct((b, l, d), BF16),
        scratch_shapes=[pltpu.VMEM(shape, F32)
                        for shape in ((nb, HEAD_DIM), (nb, qw), (1, qw), (V_ROWS, qw),
                                      (KV_GROUP * MOBA_BLOCK, qw), (KV_GROUP, qw))
                        for _ in range(ATTN_HEADS)],
        compiler_params=_cparams("parallel", "parallel", "arbitrary"),
        name="moba_attn",
    )(qt, k, vt)


def _merge_mlp_kernel(x_ref, oa_ref, sga_ref, sgb_ref, y_ref, wglu_ref, wout_ref, g1_ref, wup_ref, wdn_ref, g2_ref,
                      o_ref, *, final_norm):
    d = x_ref.shape[-1]
    ff = wup_ref.shape[1]
    y = jnp.concatenate([y_ref[s, 0] for s in range(y_ref.shape[0])], axis=1)
    hgl = jnp.dot(y.astype(BF16), wglu_ref[...], preferred_element_type=F32)
    ob = hgl[:, :d] * jax.nn.sigmoid(hgl[:, d:])
    mixed = sga_ref[0].astype(F32) * oa_ref[0].astype(F32) + sgb_ref[0].astype(F32) * ob
    x = x_ref[0] + jnp.dot(mixed.astype(BF16), wout_ref[...], preferred_element_type=F32)
    ms = jnp.mean(x * x, axis=-1, keepdims=True)
    h = (x * lax.rsqrt(ms + RMS_EPS) * g1_ref[...]).astype(BF16)
    acc = x
    for c0 in range(0, ff, d):
        a = jnp.maximum(jnp.dot(h, wup_ref[:, c0:c0 + d], preferred_element_type=F32), 0.0)
        acc = acc + jnp.dot((a * a).astype(BF16), wdn_ref[c0:c0 + d, :], preferred_element_type=F32)
    if final_norm:
        ms2 = jnp.mean(acc * acc, axis=-1, keepdims=True)
        acc = acc * lax.rsqrt(ms2 + RMS_EPS) * g2_ref[...]
    o_ref[0] = acc


def _merge_mlp(x, oa, sga, sgb, y, w_glu, w_out, g1, w_up, w_down, g2, final_norm):
    b, l, d = x.shape
    tm = TOKEN_TILE
    ns, lane = y.shape[0], y.shape[-1]
    sw = ns * lane
    ff = w_up.shape[1]
    tok = lambda w: pl.BlockSpec((1, tm, w), lambda bi, i: (bi, i, 0))
    return pl.pallas_call(
        functools.partial(_merge_mlp_kernel, final_norm=final_norm),
        grid=(b, l // tm),
        in_specs=[tok(d), tok(d), tok(d), tok(d), pl.BlockSpec((ns, 1, tm, lane), lambda bi, i: (0, bi, i, 0)),
                  _resident_spec((sw, 2 * d)), _resident_spec((d, d)), _const_spec((1, d)),
                  _resident_spec((d, ff)), _resident_spec((ff, d)), _const_spec((1, d))],
        out_specs=tok(d),
        out_shape=jax.ShapeDtypeStruct((b, l, d), F32),
        compiler_params=_cparams("parallel", "parallel"),
        name="merge_mlp",
    )(x, oa, sga, sgb, y, w_glu.astype(BF16), w_out.astype(BF16), g1.reshape(1, d),
      w_up.astype(BF16), w_down.astype(BF16), g2.reshape(1, d))


def kernel(x, norm_mix_g, w_in, lam_re, lam_im, log_step, b_re, b_im, c_re, c_im, d_skip, w_glu, w_out,
           norm_mlp_g, w_up, w_down, norm_final_g):
    b, l, d = x.shape
    depth = w_in.shape[0]
    assert d == N_HEADS * HEAD_DIM and l % TOKEN_TILE == 0 and l % (SSM_CHUNK * SCAN_ROWS) == 0
    for i in range(depth):
        slab_mats = _ssm_prep(lam_re[i], lam_im[i], log_step[i], b_re[i], b_im[i], c_re[i], c_im[i], d_skip[i])
        qt, k, vt, u, sga, sgb = _in_proj(x, norm_mix_g[i], w_in[i])
        y = _ssm_chunk(u, *slab_mats)
        oa = _moba_attention(qt, k, vt)
        x = _merge_mlp(x, oa, sga, sgb, y, w_glu[i], w_out[i], norm_mlp_g[i], w_up[i], w_down[i], norm_final_g,
                       final_norm=(i == depth - 1))
    return x
```

```python
import functools
import math

import jax
import jax.numpy as jnp
from jax import lax
from jax.experimental import pallas as pl
from jax.experimental.pallas import tpu as pltpu

F32 = jnp.float32
BF16 = jnp.bfloat16

N_HEADS = 8
HEAD_DIM = 128
HALF_DIM = HEAD_DIM // 2
MOBA_BLOCK = 256
MOBA_TOPK = 3
ATTN_HEADS = 2
Q_BLOCKS = 4
KV_GROUP = 4
SCORE_LEAD = 1
BF16_SUBLANES = 16
V_ROWS = HEAD_DIM + BF16_SUBLANES
ROPE_THETA = 10000.0
SSM_GROUP = 16
SSM_STATE = 64
SSM_CHUNK = 8
CHUNK_W = SSM_CHUNK * SSM_GROUP
SCAN_ROWS = 8
LANE = 128
GROUPS_PER_SLAB = LANE // SSM_GROUP
SLAB_W = SSM_CHUNK * LANE
RMS_EPS = 1e-6
NEG_BIG = -1e30
TOKEN_TILE = 512
VMEM_LIMIT = 56 * 1024 * 1024

_NT = (((1,), (1,)), ((), ()))


def _cparams(*sem):
    return pltpu.CompilerParams(dimension_semantics=sem, vmem_limit_bytes=VMEM_LIMIT)


def _const_spec(shape):
    nd = len(shape)
    return pl.BlockSpec(shape, lambda *_: (0,) * nd)


def _resident_spec(shape):
    nd = len(shape)
    return pl.BlockSpec(shape, lambda *_: (0,) * nd, pipeline_mode=pl.Buffered(1))


def _cexp(lr, li, step, e):
    mag = jnp.exp(lr * step * e)
    ang = li * step * e
    return mag * jnp.cos(ang), mag * jnp.sin(ang)


def _spread(rows, cols, target_fn):
    r = lax.broadcasted_iota(jnp.int32, (rows, cols), 0)
    c = lax.broadcasted_iota(jnp.int32, (rows, cols), 1)
    return jnp.where(c == target_fn(r), 1.0, 0.0).astype(F32)


def _ssm_prep_kernel(lrr_ref, lir_ref, ls_ref, btr_ref, bti_ref, ctr_ref, cti_ref, dt_ref,
                     mx_ref, gx_ref, hx_ref, apx_ref, aix_ref, dx_ref):
    n = lrr_ref.shape[-1]
    gl = lax.rem(pl.program_id(0), GROUPS_PER_SLAB)
    hp = lax.Precision.HIGHEST
    place = functools.partial(jnp.dot, precision=hp, preferred_element_type=F32)
    place16 = lambda a, spread: jnp.dot(a.astype(BF16), spread.astype(BF16), preferred_element_type=F32)
    to_slab = _spread(CHUNK_W, SLAB_W, lambda r: lax.shift_right_logical(r, 4) * LANE + gl * SSM_GROUP + (r & 15))
    to_state = _spread(n, GROUPS_PER_SLAB * n, lambda r: gl * n + r)
    to_state2 = _spread(2 * n, 2 * GROUPS_PER_SLAB * n,
                        lambda r: jnp.where(r >= n, GROUPS_PER_SLAB * n - n, 0) + gl * n + r)
    step = jnp.exp(ls_ref[0])
    lr_r, li_r = lrr_ref[0], lir_ref[0]
    prow = lax.broadcasted_iota(jnp.int32, (SSM_CHUNK + SCAN_ROWS, 1), 0)
    pw_r, pw_i = _cexp(lr_r, li_r, step,
                       jnp.where(prow <= SSM_CHUNK, prow, (prow - SSM_CHUNK + 1) * SSM_CHUNK).astype(F32))
    tau = lax.shift_right_logical(lax.broadcasted_iota(jnp.int32, pw_r.shape[:1] + (CHUNK_W,), 1), 4)
    pick = lambda shift: jnp.where(tau + shift == lax.broadcasted_iota(jnp.int32, tau.shape, 0), 1.0, 0.0)
    on_sublanes = lambda p, shift: lax.dot_general(p, pick(shift), (((0,), (0,)), ((), ())), precision=hp,
                                                   preferred_element_type=F32)
    ar, ai = pw_r[1:2], pw_i[1:2]
    den = lr_r * lr_r + li_r * li_r
    nr, ni = ar - 1.0, ai
    cr = (nr * lr_r + ni * li_r) / den
    ci = (ni * lr_r - nr * li_r) / den
    btr, bti = btr_ref[0], bti_ref[0]
    bbr = cr * btr - ci * bti
    bbi = cr * bti + ci * btr
    rows_of = lambda p: jnp.concatenate(
        [jnp.broadcast_to(p[SSM_CHUNK - 1 - s:SSM_CHUNK - s], (SSM_GROUP, n)) for s in range(SSM_CHUNK)], axis=0)
    pr, pi_ = rows_of(pw_r), rows_of(pw_i)
    gcat = jnp.concatenate([pr * bbr - pi_ * bbi, pr * bbi + pi_ * bbr], axis=1)
    gwide = place16(gcat, to_state2).astype(gx_ref.dtype)
    row0 = pl.multiple_of(gl * SSM_GROUP, SSM_GROUP)
    for s in range(SSM_CHUNK):
        gx_ref[0, pl.ds(s * LANE + row0, SSM_GROUP), :] = gwide[s * SSM_GROUP:(s + 1) * SSM_GROUP]
    ctr, cti = ctr_ref[0], cti_ref[0]
    wr, wi = on_sublanes(pw_r, 0), on_sublanes(pw_i, 0)
    p_r = ctr * wr - cti * wi
    p_i = ctr * wi + cti * wr
    r0 = place(bbr[:SSM_GROUP], p_r) - place(bbi[:SSM_GROUP], p_i)
    r0w = place16(r0, to_slab)
    lane = lax.broadcasted_iota(jnp.int32, (SSM_GROUP, SLAB_W), 1)
    for s in range(SSM_CHUNK):
        blk = r0w if s == 0 else pltpu.roll(r0w, LANE * s, axis=1)
        blk = jnp.where(lane >= LANE * s, blk, 0.0)
        mx_ref[0, pl.ds(s * LANE + row0, SSM_GROUP), :] = blk.astype(mx_ref.dtype)
    hr, hi = on_sublanes(pw_r, 1), on_sublanes(pw_i, 1)
    st0 = pl.multiple_of(gl * n, n)
    hx_ref[0, pl.ds(st0, n), :] = place16(ctr * hr - cti * hi, to_slab).astype(hx_ref.dtype)
    hx_ref[0, pl.ds(GROUPS_PER_SLAB * n + st0, n), :] = place16(-(ctr * hi + cti * hr), to_slab).astype(hx_ref.dtype)
    qr, qi = pw_r[SSM_CHUNK:], pw_i[SSM_CHUNK:]
    shared = ((apx_ref, place(qr, to_state)), (aix_ref, place(qi, to_state)), (dx_ref, place(dt_ref[0], to_slab)))

    @pl.when(gl == 0)
    def _():
        for ref, val in shared:
            ref[0] = val

    @pl.when(gl != 0)
    def _():
        for ref, val in shared:
            ref[0] = ref[0] + val


def _ssm_prep(lam_re, lam_im, log_step, b_re, b_im, c_re, c_im, d_skip):
    g, n = lam_re.shape
    ns = g // GROUPS_PER_SLAB
    sl = GROUPS_PER_SLAB * n
    bt = lambda b: jnp.tile(jnp.swapaxes(b, 1, 2), (1, SSM_CHUNK, 1))
    ct = lambda m: jnp.tile(jnp.swapaxes(m, 1, 2), (1, 1, SSM_CHUNK))
    args = (lam_re.reshape(g, 1, n), lam_im.reshape(g, 1, n), log_step.reshape(g, 1, 1),
            bt(b_re), bt(b_im), ct(c_re), ct(c_im),
            jnp.tile(d_skip.astype(F32), (1, SSM_CHUNK)).reshape(g, 1, CHUNK_W))
    spec = lambda shp: pl.BlockSpec((1,) + shp, lambda i: (i, 0, 0))
    slab = lambda shp: pl.BlockSpec((1,) + shp, lambda i: (i // GROUPS_PER_SLAB, 0, 0))
    in_specs = [spec((1, n)), spec((1, n)), spec((1, 1)),
                spec((CHUNK_W, n)), spec((CHUNK_W, n)), spec((n, CHUNK_W)), spec((n, CHUNK_W)), spec((1, CHUNK_W))]
    out_shapes = [((SLAB_W, SLAB_W), BF16), ((SLAB_W, 2 * sl), BF16), ((2 * sl, SLAB_W), BF16),
                  ((SCAN_ROWS, sl), F32), ((SCAN_ROWS, sl), F32), ((1, SLAB_W), F32)]
    return pl.pallas_call(
        _ssm_prep_kernel,
        grid=(g,),
        in_specs=in_specs,
        out_specs=[slab(s) for s, _ in out_shapes],
        out_shape=[jax.ShapeDtypeStruct((ns,) + s, d) for s, d in out_shapes],
        compiler_params=_cparams("arbitrary"),
        name="ssm_prep",
    )(*args)


def _in_proj_kernel(x_ref, g_ref, wqt_ref, wk_ref, wvt_ref, wu_ref, wg_ref,
                    cos_ref, sin_ref, cost_ref, sint_ref,
                    qt_ref, k_ref, vt_ref, u_ref, sga_ref, sgb_ref):
    x = x_ref[0]
    d = x.shape[-1]
    ms = jnp.mean(x * x, axis=-1, keepdims=True)
    h = (x * lax.rsqrt(ms + RMS_EPS) * g_ref[...]).astype(BF16)
    for c0 in range(0, 2 * d, 512):
        sg = jax.nn.sigmoid(jnp.dot(h, wg_ref[:, c0:c0 + 512], preferred_element_type=F32)).astype(BF16)
        if c0 < d:
            sga_ref[0, :, c0:c0 + 512] = sg
        else:
            sgb_ref[0, :, c0 - d:c0 - d + 512] = sg
    cos, sin = cos_ref[...], sin_ref[...]
    for c0 in range(0, d, 512):
        p = jnp.dot(h, wk_ref[:, c0:c0 + 512], preferred_element_type=F32)
        for hh in range(512 // HEAD_DIM):
            sl = p[:, hh * HEAD_DIM:(hh + 1) * HEAD_DIM]
            r = sl * cos + pltpu.roll(sl, HALF_DIM, axis=1) * sin
            k_ref[0, :, c0 + hh * HEAD_DIM:c0 + (hh + 1) * HEAD_DIM] = r.astype(BF16)
    cost, sint = cost_ref[...], sint_ref[...]
    scale = HEAD_DIM ** -0.5 * math.log2(math.e)
    for c0 in range(0, d, 256):
        pt = lax.dot_general(wqt_ref[c0:c0 + 256, :], h, _NT, preferred_element_type=F32)
        for hh in range(256 // HEAD_DIM):
            r0 = hh * HEAD_DIM
            x1 = pt[r0:r0 + HALF_DIM]
            x2 = pt[r0 + HALF_DIM:r0 + HEAD_DIM]
            qt_ref[0, c0 + r0:c0 + r0 + HALF_DIM, :] = ((x1 * cost - x2 * sint) * scale).astype(BF16)
            qt_ref[0, c0 + r0 + HALF_DIM:c0 + r0 + HEAD_DIM, :] = ((x2 * cost + x1 * sint) * scale).astype(BF16)
    ones = jnp.ones((BF16_SUBLANES, x.shape[0]), BF16)
    for c0 in range(0, d, 256):
        pv = lax.dot_general(wvt_ref[c0:c0 + 256, :], h, _NT, preferred_element_type=F32).astype(BF16)
        for hh in range(256 // HEAD_DIM):
            r0 = (c0 // HEAD_DIM + hh) * V_ROWS
            vt_ref[0, r0:r0 + HEAD_DIM, :] = pv[hh * HEAD_DIM:(hh + 1) * HEAD_DIM]
            vt_ref[0, r0 + HEAD_DIM:r0 + V_ROWS, :] = ones
    pu = jnp.dot(h, wu_ref[...], preferred_element_type=F32)
    for s in range(pu.shape[1] // LANE):
        u_ref[s, 0] = pu[:, s * LANE:(s + 1) * LANE]


def _in_proj(x, g, w_in):
    b, l, d = x.shape
    tm = TOKEN_TILE
    sw = w_in.shape[1] - 5 * d
    wqt = w_in[:, :d].T.astype(BF16)
    wk = w_in[:, d:2 * d].astype(BF16)
    wvt = w_in[:, 2 * d:3 * d].T.astype(BF16)
    wu = w_in[:, 3 * d:3 * d + sw].astype(BF16)
    wg = w_in[:, 3 * d + sw:].astype(BF16)
    inv_freq = ROPE_THETA ** (-jnp.arange(HALF_DIM, dtype=F32) / HALF_DIM)
    ang = jnp.arange(l).astype(F32)[:, None] * inv_freq[None, :]
    cos, sin = jnp.cos(ang), jnp.sin(ang)
    cos2 = jnp.concatenate([cos, cos], axis=1)
    sin2 = jnp.concatenate([-sin, sin], axis=1)
    tok = lambda w: pl.BlockSpec((1, tm, w), lambda bi, i: (bi, i, 0))
    feat = lambda w: pl.BlockSpec((1, w, tm), lambda bi, i: (bi, 0, i))
    sds = jax.ShapeDtypeStruct
    return pl.pallas_call(
        _in_proj_kernel,
        grid=(b, l // tm),
        in_specs=[tok(d), _const_spec((1, d)), _const_spec((d, d)), _const_spec((d, d)), _const_spec((d, d)),
                  _const_spec((d, sw)), _const_spec((d, 2 * d)),
                  pl.BlockSpec((tm, HEAD_DIM), lambda bi, i: (i, 0)),
                  pl.BlockSpec((tm, HEAD_DIM), lambda bi, i: (i, 0)),
                  pl.BlockSpec((HALF_DIM, tm), lambda bi, i: (0, i)),
                  pl.BlockSpec((HALF_DIM, tm), lambda bi, i: (0, i))],
        out_specs=[feat(d), tok(d), feat(N_HEADS * V_ROWS),
                   pl.BlockSpec((sw // LANE, 1, tm, LANE), lambda bi, i: (0, bi, i, 0)), tok(d), tok(d)],
        out_shape=[sds((b, d, l), BF16), sds((b, l, d), BF16), sds((b, N_HEADS * V_ROWS, l), BF16),
                   sds((sw // LANE, b, l, LANE), F32), sds((b, l, d), BF16), sds((b, l, d), BF16)],
        compiler_params=_cparams("parallel", "parallel"),
        name="in_proj",
    )(x, g.reshape(1, d), wqt, wk, wvt, wu, wg, cos2, sin2, cos.T, sin.T)


def _ssm_chunk_kernel(u_ref, mx_ref, gx_ref, hx_ref, apr_ref, api_ref, d_ref,
                      y_ref, ere_ref, eim_ref, xre_ref, xim_ref):
    nk = u_ref.shape[2] // SSM_CHUNK
    sl = ere_ref.shape[1]
    uf = jnp.concatenate([u_ref[0, 0, pl.ds(t, nk, stride=SSM_CHUNK), :] for t in range(SSM_CHUNK)], axis=1)
    u = uf.astype(BF16)
    dot = functools.partial(jnp.dot, preferred_element_type=F32)
    e = dot(u, gx_ref[0])
    ere_ref[...] = e[:, :sl]
    eim_ref[...] = e[:, sl:]
    apr, api = apr_ref[0], api_ref[0]
    row = lax.broadcasted_iota(jnp.int32, apr.shape, 0)
    lvl = [(dd, jnp.broadcast_to(apr[dd - 1:dd], apr.shape), jnp.broadcast_to(api[dd - 1:dd], apr.shape))
           for dd in (1, 2, 4)]

    def tile(t, carry):
        cr, ci = carry
        r0 = pl.multiple_of(t * SCAN_ROWS, SCAN_ROWS)
        zr = ere_ref[pl.ds(r0, SCAN_ROWS), :]
        zi = eim_ref[pl.ds(r0, SCAN_ROWS), :]
        for dd, adr, adi in lvl:
            sr = jnp.where(row >= dd, pltpu.roll(zr, dd, axis=0), 0.0)
            si = jnp.where(row >= dd, pltpu.roll(zi, dd, axis=0), 0.0)
            zr, zi = zr + (adr * sr - adi * si), zi + (adr * si + adi * sr)
        sr = apr * cr - api * ci + zr
        si = apr * ci + api * cr + zi
        xre_ref[pl.ds(r0, SCAN_ROWS), :] = jnp.where(row >= 1, pltpu.roll(sr, 1, axis=0), cr)
        xim_ref[pl.ds(r0, SCAN_ROWS), :] = jnp.where(row >= 1, pltpu.roll(si, 1, axis=0), ci)
        last = SCAN_ROWS - 1
        return (jnp.broadcast_to(sr[last:last + 1], sr.shape), jnp.broadcast_to(si[last:last + 1], si.shape))

    zero = jnp.zeros(apr.shape, F32)
    lax.fori_loop(0, nk // SCAN_ROWS, tile, (zero, zero))
    x0 = jnp.concatenate([xre_ref[...], xim_ref[...]], axis=1).astype(BF16)
    y = jax.nn.gelu(dot(u, mx_ref[0]) + dot(x0, hx_ref[0]) + d_ref[0] * uf)
    for t in range(SSM_CHUNK):
        y_ref[0, 0, pl.ds(t, nk, stride=SSM_CHUNK), :] = y[:, t * LANE:(t + 1) * LANE]


def _ssm_chunk(u4, mx, gx, hx, apx, aix, dx):
    ns, b, l, lane = u4.shape
    nk = l // SSM_CHUNK
    sl = apx.shape[-1]
    per_slab = lambda r, c: pl.BlockSpec((1, r, c), lambda s, bi: (s, 0, 0))
    rows = pl.BlockSpec((1, 1, l, lane), lambda s, bi: (s, bi, 0, 0))
    return pl.pallas_call(
        _ssm_chunk_kernel,
        grid=(ns, b),
        in_specs=[rows, per_slab(SLAB_W, SLAB_W), per_slab(SLAB_W, 2 * sl), per_slab(2 * sl, SLAB_W),
                  per_slab(SCAN_ROWS, sl), per_slab(SCAN_ROWS, sl), per_slab(1, SLAB_W)],
        out_specs=rows,
        out_shape=jax.ShapeDtypeStruct(u4.shape, F32),
        scratch_shapes=[pltpu.VMEM((nk, sl), F32) for _ in range(4)],
        compiler_params=_cparams("parallel", "parallel"),
        name="ssm_chunk",
    )(u4, mx, gx, hx, apx, aix, dx)


def _moba_kernel(qt_ref, k_ref, vt_ref, o_ref, *scratch):
    i = pl.program_id(2)
    blk = MOBA_BLOCK
    qw = Q_BLOCKS * blk
    nb = k_ref.shape[1] // blk
    fb = i * Q_BLOCKS
    heads = range(ATTN_HEADS)
    hsl = lambda hp: slice(hp * HEAD_DIM, (hp + 1) * HEAD_DIM)
    kmean_ref, sel_ref, m_ref, acc_ref, s_ref, cm_ref, ksplit_ref = (
        scratch[r * ATTN_HEADS:(r + 1) * ATTN_HEADS] for r in range(7))

    @pl.when(i == 0)
    def _():
        for hp in heads:
            for j in range(nb):
                kb = k_ref[0, j * blk:(j + 1) * blk, hsl(hp)].astype(F32)
                kmean_ref[hp][j:j + 1, :] = jnp.mean(kb, axis=0, keepdims=True)
            rest = kmean_ref[hp][...]
            for part in range(3):
                piece = rest.astype(BF16)
                ksplit_ref[hp][part * nb:(part + 1) * nb, :] = piece
                rest = rest - piece.astype(F32)

    vsl = lambda hp: slice(hp * V_ROWS, (hp + 1) * V_ROWS)

    def select_blocks(hp, qt):
        g3 = jnp.dot(ksplit_ref[hp][...], qt, preferred_element_type=F32)
        gate = g3[:nb] + g3[nb:2 * nb] + g3[2 * nb:]
        brow = lax.broadcasted_iota(jnp.int32, gate.shape, 0)
        own = fb + lax.broadcasted_iota(jnp.int32, (1, qw), 1) // blk
        work = jnp.where(brow < own, gate, -jnp.inf)
        sel = jnp.zeros(gate.shape, F32)
        for _ in range(MOBA_TOPK):
            mx = jnp.max(work, axis=0, keepdims=True)
            idx = jnp.min(jnp.where(work == mx, brow, nb), axis=0, keepdims=True)
            hit = brow == idx
            sel = jnp.where(hit, jnp.where(idx < own, 1.0, 0.0), sel)
            work = jnp.where(hit, -jnp.inf, work)
        sel_ref[hp][...] = sel

    def item_blocks(g):
        first, count = (fb, Q_BLOCKS) if g is None else (g * KV_GROUP, KV_GROUP)
        if isinstance(first, int):
            return [slice((first + jl) * blk, (first + jl + 1) * blk) for jl in range(count)]
        return [pl.ds(pl.multiple_of((first + jl) * blk, blk), blk) for jl in range(count)]

    def score_steps(hp, qt, g):
        buf = s_ref[hp]

        def step(jl, rows):
            s = jnp.dot(k_ref[0, rows, hsl(hp)], qt, preferred_element_type=F32)
            if g is None:
                krow = lax.broadcasted_iota(jnp.int32, s.shape, 0)
                qcol = lax.broadcasted_iota(jnp.int32, s.shape, 1)
                picked = sel_ref[hp][pl.ds(fb + jl, 1), :] > 0.0
                s = jnp.where(qcol >= (jl + 1) * blk, jnp.where(picked, s, NEG_BIG),
                              jnp.where(krow + jl * blk <= qcol, s, NEG_BIG))
            buf[jl * blk:(jl + 1) * blk, :] = s
            cm_ref[hp][jl:jl + 1, :] = jnp.max(s, axis=0, keepdims=True)

        return [functools.partial(step, jl, rows) for jl, rows in enumerate(item_blocks(g))]

    def weight_steps(hp, g):
        buf = s_ref[hp]
        refs = []

        def step(jl, rows):
            if jl == 0:
                cmax = [cm_ref[hp][j:j + 1, :] for j in range(Q_BLOCKS if g is None else KV_GROUP)]
                if g is None:
                    m_new = functools.reduce(jnp.maximum, cmax)
                    refs.extend(m_new for _ in cmax)
                else:
                    picked = [jnp.logical_and(sel_ref[hp][pl.ds(g * KV_GROUP + j, 1), :] > 0.0,
                                              g * KV_GROUP + j < fb) for j in range(KV_GROUP)]
                    m_old = m_ref[hp][...]
                    m_new = m_old
                    for j in range(KV_GROUP):
                        m_new = jnp.maximum(m_new, jnp.where(picked[j], cmax[j], NEG_BIG))
                    acc_ref[hp][...] = jnp.exp2(m_old - m_new) * acc_ref[hp][...]
                    refs.extend(jnp.where(picked[j], m_new, -NEG_BIG) for j in range(KV_GROUP))
                m_ref[hp][...] = m_new
            p = jnp.exp2(buf[jl * blk:(jl + 1) * blk, :] - refs[jl]).astype(BF16)
            pv = jnp.dot(vt_ref[0, vsl(hp), rows], p, preferred_element_type=F32)
            if g is None and jl == 0:
                acc_ref[hp][...] = pv
            else:
                acc_ref[hp][...] = acc_ref[hp][...] + pv

        return [functools.partial(step, jl, rows) for jl, rows in enumerate(item_blocks(g))]

    def interleave(w_steps, s_steps):
        for n in range(max(len(w_steps) + SCORE_LEAD, len(s_steps))):
            if n < len(s_steps):
                s_steps[n]()
            if SCORE_LEAD <= n < len(w_steps) + SCORE_LEAD:
                w_steps[n - SCORE_LEAD]()

    qts = [qt_ref[0, hsl(hp), :] for hp in heads]
    for hp in heads:
        select_blocks(hp, qts[hp])
    interleave([], score_steps(0, qts[0], None))
    for hp in heads:
        nhp = (hp + 1) % ATTN_HEADS
        interleave(weight_steps(hp, None), score_steps(nhp, qts[nhp], None if nhp else 0))

    ngroups = (fb + KV_GROUP - 1) // KV_GROUP

    def past_group(g, _, last=False):
        for hp in heads:
            nhp = (hp + 1) % ATTN_HEADS
            ahead = [] if (last and not nhp) else score_steps(nhp, qts[nhp], g if nhp else g + 1)
            interleave(weight_steps(hp, g), ahead)
        return 0

    lax.fori_loop(0, ngroups - 1, past_group, 0)

    @pl.when(ngroups > 0)
    def _():
        past_group(ngroups - 1, 0, last=True)

    for hp in heads:
        ot = acc_ref[hp][:HEAD_DIM, :] / acc_ref[hp][HEAD_DIM:HEAD_DIM + 1, :]
        o_ref[0, :, hsl(hp)] = ot.T.astype(o_ref.dtype)


def _moba_attention(qt, k, vt):
    b, l, d = k.shape
    nb = l // MOBA_BLOCK
    assert nb % KV_GROUP == 0 and nb % Q_BLOCKS == 0
    hw = ATTN_HEADS * HEAD_DIM
    qw = Q_BLOCKS * MOBA_BLOCK
    return pl.pallas_call(
        _moba_kernel,
        grid=(b, N_HEADS // ATTN_HEADS, nb // Q_BLOCKS),
        in_specs=[pl.BlockSpec((1, hw, qw), lambda bi, h, i: (bi, h, i)),
                  pl.BlockSpec((1, l, hw), lambda bi, h, i: (bi, 0, h)),
                  pl.BlockSpec((1, ATTN_HEADS * V_ROWS, l), lambda bi, h, i: (bi, h, 0))],
        out_specs=pl.BlockSpec((1, qw, hw), lambda bi, h, i: (bi, i, h)),
        out_shape=jax.ShapeDtypeStruct((b, l, d), BF16),
        scratch_shapes=[pltpu.VMEM(shape, dtype)
                        for shape, dtype in (((nb, HEAD_DIM), F32), ((nb, qw), F32), ((1, qw), F32),
                                             ((V_ROWS, qw), F32), ((KV_GROUP * MOBA_BLOCK, qw), F32),
                                             ((KV_GROUP, qw), F32), ((3 * nb, HEAD_DIM), BF16))
                        for _ in range(ATTN_HEADS)],
        compiler_params=_cparams("parallel", "parallel", "arbitrary"),
        name="moba_attn",
    )(qt, k, vt)


def _merge_mlp_kernel(x_ref, oa_ref, sga_ref, sgb_ref, y_ref, wglu_ref, wout_ref, g1_ref, wup_ref, wdn_ref, g2_ref,
                      o_ref, *, final_norm):
    d = x_ref.shape[-1]
    ff = wup_ref.shape[1]
    y = jnp.concatenate([y_ref[s, 0] for s in range(y_ref.shape[0])], axis=1)
    hgl = jnp.dot(y.astype(BF16), wglu_ref[...], preferred_element_type=F32)
    ob = hgl[:, :d] * jax.nn.sigmoid(hgl[:, d:])
    mixed = sga_ref[0].astype(F32) * oa_ref[0].astype(F32) + sgb_ref[0].astype(F32) * ob
    x = x_ref[0] + jnp.dot(mixed.astype(BF16), wout_ref[...], preferred_element_type=F32)
    ms = jnp.mean(x * x, axis=-1, keepdims=True)
    h = (x * lax.rsqrt(ms + RMS_EPS) * g1_ref[...]).astype(BF16)
    acc = x
    for c0 in range(0, ff, d):
        a = jnp.maximum(jnp.dot(h, wup_ref[:, c0:c0 + d], preferred_element_type=F32), 0.0)
        acc = acc + jnp.dot((a * a).astype(BF16), wdn_ref[c0:c0 + d, :], preferred_element_type=F32)
    if final_norm:
        ms2 = jnp.mean(acc * acc, axis=-1, keepdims=True)
        acc = acc * lax.rsqrt(ms2 + RMS_EPS) * g2_ref[...]
    o_ref[0] = acc


def _merge_mlp(x, oa, sga, sgb, y, w_glu, w_out, g1, w_up, w_down, g2, final_norm):
    b, l, d = x.shape
    tm = TOKEN_TILE
    ns, lane = y.shape[0], y.shape[-1]
    sw = ns * lane
    ff = w_up.shape[1]
    tok = lambda w: pl.BlockSpec((1, tm, w), lambda bi, i: (bi, i, 0))
    return pl.pallas_call(
        functools.partial(_merge_mlp_kernel, final_norm=final_norm),
        grid=(b, l // tm),
        in_specs=[tok(d), tok(d), tok(d), tok(d), pl.BlockSpec((ns, 1, tm, lane), lambda bi, i: (0, bi, i, 0)),
                  _resident_spec((sw, 2 * d)), _resident_spec((d, d)), _const_spec((1, d)),
                  _resident_spec((d, ff)), _resident_spec((ff, d)), _const_spec((1, d))],
        out_specs=tok(d),
        out_shape=jax.ShapeDtypeStruct((b, l, d), F32),
        compiler_params=_cparams("parallel", "parallel"),
        name="merge_mlp",
    )(x, oa, sga, sgb, y, w_glu.astype(BF16), w_out.astype(BF16), g1.reshape(1, d),
      w_up.astype(BF16), w_down.astype(BF16), g2.reshape(1, d))


def kernel(x, norm_mix_g, w_in, lam_re, lam_im, log_step, b_re, b_im, c_re, c_im, d_skip, w_glu, w_out,
           norm_mlp_g, w_up, w_down, norm_final_g):
    b, l, d = x.shape
    depth = w_in.shape[0]
    assert d == N_HEADS * HEAD_DIM and l % TOKEN_TILE == 0 and l % (SSM_CHUNK * SCAN_ROWS) == 0
    for i in range(depth):
        slab_mats = _ssm_prep(lam_re[i], lam_im[i], log_step[i], b_re[i], b_im[i], c_re[i], c_im[i], d_skip[i])
        qt, k, vt, u, sga, sgb = _in_proj(x, norm_mix_g[i], w_in[i])
        y = _ssm_chunk(u, *slab_mats)
        oa = _moba_attention(qt, k, vt)
        x = _merge_mlp(x, oa, sga, sgb, y, w_glu[i], w_out[i], norm_mlp_g[i], w_up[i], w_down[i], norm_final_g,
                       final_norm=(i == depth - 1))
    return x
```

```python
import functools
import math

import jax
import jax.numpy as jnp
from jax import lax
from jax.experimental import pallas as pl
from jax.experimental.pallas import tpu as pltpu

F32 = jnp.float32
BF16 = jnp.bfloat16

N_HEADS = 8
HEAD_DIM = 128
HALF_DIM = HEAD_DIM // 2
MOBA_BLOCK = 256
MOBA_TOPK = 3
ATTN_HEADS = 2
Q_BLOCKS = 4
KV_GROUP = 4
SCORE_LEAD = 1
BF16_SUBLANES = 16
V_ROWS = HEAD_DIM + BF16_SUBLANES
ROPE_THETA = 10000.0
SSM_GROUP = 16
SSM_STATE = 64
SSM_CHUNK = 8
CHUNK_W = SSM_CHUNK * SSM_GROUP
SCAN_ROWS = 8
LANE = 128
GROUPS_PER_SLAB = LANE // SSM_GROUP
SLAB_W = SSM_CHUNK * LANE
RMS_EPS = 1e-6
NEG_BIG = -1e30
TOKEN_TILE = 512
VMEM_LIMIT = 56 * 1024 * 1024

_NT = (((1,), (1,)), ((), ()))


def _cparams(*sem):
    return pltpu.CompilerParams(dimension_semantics=sem, vmem_limit_bytes=VMEM_LIMIT)


def _const_spec(shape):
    nd = len(shape)
    return pl.BlockSpec(shape, lambda *_: (0,) * nd)


def _resident_spec(shape):
    nd = len(shape)
    return pl.BlockSpec(shape, lambda *_: (0,) * nd, pipeline_mode=pl.Buffered(1))


def _cexp(lr, li, step, e):
    mag = jnp.exp(lr * step * e)
    ang = li * step * e
    return mag * jnp.cos(ang), mag * jnp.sin(ang)


def _spread(rows, cols, target_fn):
    r = lax.broadcasted_iota(jnp.int32, (rows, cols), 0)
    c = lax.broadcasted_iota(jnp.int32, (rows, cols), 1)
    return jnp.where(c == target_fn(r), 1.0, 0.0).astype(F32)


def _ssm_prep_kernel(lrr_ref, lir_ref, ls_ref, btr_ref, bti_ref, ctr_ref, cti_ref, dt_ref,
                     mx_ref, gx_ref, hx_ref, apx_ref, aix_ref, dx_ref):
    n = lrr_ref.shape[-1]
    gl = lax.rem(pl.program_id(0), GROUPS_PER_SLAB)
    hp = lax.Precision.HIGHEST
    place = functools.partial(jnp.dot, precision=hp, preferred_element_type=F32)
    place16 = lambda a, spread: jnp.dot(a.astype(BF16), spread.astype(BF16), preferred_element_type=F32)
    to_slab = _spread(CHUNK_W, SLAB_W, lambda r: lax.shift_right_logical(r, 4) * LANE + gl * SSM_GROUP + (r & 15))
    to_state = _spread(n, GROUPS_PER_SLAB * n, lambda r: gl * n + r)
    to_state2 = _spread(2 * n, 2 * GROUPS_PER_SLAB * n,
                        lambda r: jnp.where(r >= n, GROUPS_PER_SLAB * n - n, 0) + gl * n + r)
    step = jnp.exp(ls_ref[0])
    lr_r, li_r = lrr_ref[0], lir_ref[0]
    prow = lax.broadcasted_iota(jnp.int32, (SSM_CHUNK + SCAN_ROWS, 1), 0)
    pw_r, pw_i = _cexp(lr_r, li_r, step,
                       jnp.where(prow <= SSM_CHUNK, prow, (prow - SSM_CHUNK + 1) * SSM_CHUNK).astype(F32))
    tau = lax.shift_right_logical(lax.broadcasted_iota(jnp.int32, pw_r.shape[:1] + (CHUNK_W,), 1), 4)
    pick = lambda shift: jnp.where(tau + shift == lax.broadcasted_iota(jnp.int32, tau.shape, 0), 1.0, 0.0)
    on_sublanes = lambda p, shift: lax.dot_general(p, pick(shift), (((0,), (0,)), ((), ())), precision=hp,
                                                   preferred_element_type=F32)
    ar, ai = pw_r[1:2], pw_i[1:2]
    den = lr_r * lr_r + li_r * li_r
    nr, ni = ar - 1.0, ai
    cr = (nr * lr_r + ni * li_r) / den
    ci = (ni * lr_r - nr * li_r) / den
    btr, bti = btr_ref[0], bti_ref[0]
    bbr = cr * btr - ci * bti
    bbi = cr * bti + ci * btr
    rows_of = lambda p: jnp.concatenate(
        [jnp.broadcast_to(p[SSM_CHUNK - 1 - s:SSM_CHUNK - s], (SSM_GROUP, n)) for s in range(SSM_CHUNK)], axis=0)
    pr, pi_ = rows_of(pw_r), rows_of(pw_i)
    gcat = jnp.concatenate([pr * bbr - pi_ * bbi, pr * bbi + pi_ * bbr], axis=1)
    gwide = place16(gcat, to_state2).astype(gx_ref.dtype)
    row0 = pl.multiple_of(gl * SSM_GROUP, SSM_GROUP)
    for s in range(SSM_CHUNK):
        gx_ref[0, pl.ds(s * LANE + row0, SSM_GROUP), :] = gwide[s * SSM_GROUP:(s + 1) * SSM_GROUP]
    ctr, cti = ctr_ref[0], cti_ref[0]
    wr, wi = on_sublanes(pw_r, 0), on_sublanes(pw_i, 0)
    p_r = ctr * wr - cti * wi
    p_i = ctr * wi + cti * wr
    r0 = place(bbr[:SSM_GROUP], p_r) - place(bbi[:SSM_GROUP], p_i)
    r0w = place16(r0, to_slab)
    lane = lax.broadcasted_iota(jnp.int32, (SSM_GROUP, SLAB_W), 1)
    for s in range(SSM_CHUNK):
        blk = r0w if s == 0 else pltpu.roll(r0w, LANE * s, axis=1)
        blk = jnp.where(lane >= LANE * s, blk, 0.0)
        mx_ref[0, pl.ds(s * LANE + row0, SSM_GROUP), :] = blk.astype(mx_ref.dtype)
    hr, hi = on_sublanes(pw_r, 1), on_sublanes(pw_i, 1)
    st0 = pl.multiple_of(gl * n, n)
    hx_ref[0, pl.ds(st0, n), :] = place16(ctr * hr - cti * hi, to_slab).astype(hx_ref.dtype)
    hx_ref[0, pl.ds(GROUPS_PER_SLAB * n + st0, n), :] = place16(-(ctr * hi + cti * hr), to_slab).astype(hx_ref.dtype)
    qr, qi = pw_r[SSM_CHUNK:], pw_i[SSM_CHUNK:]
    shared = ((apx_ref, place(qr, to_state)), (aix_ref, place(qi, to_state)), (dx_ref, place(dt_ref[0], to_slab)))

    @pl.when(gl == 0)
    def _():
        for ref, val in shared:
            ref[0] = val

    @pl.when(gl != 0)
    def _():
        for ref, val in shared:
            ref[0] = ref[0] + val


def _ssm_prep(lam_re, lam_im, log_step, b_re, b_im, c_re, c_im, d_skip):
    g, n = lam_re.shape
    ns = g // GROUPS_PER_SLAB
    sl = GROUPS_PER_SLAB * n
    bt = lambda b: jnp.tile(jnp.swapaxes(b, 1, 2), (1, SSM_CHUNK, 1))
    ct = lambda m: jnp.tile(jnp.swapaxes(m, 1, 2), (1, 1, SSM_CHUNK))
    args = (lam_re.reshape(g, 1, n), lam_im.reshape(g, 1, n), log_step.reshape(g, 1, 1),
            bt(b_re), bt(b_im), ct(c_re), ct(c_im),
            jnp.tile(d_skip.astype(F32), (1, SSM_CHUNK)).reshape(g, 1, CHUNK_W))
    spec = lambda shp: pl.BlockSpec((1,) + shp, lambda i: (i, 0, 0))
    slab = lambda shp: pl.BlockSpec((1,) + shp, lambda i: (i // GROUPS_PER_SLAB, 0, 0))
    in_specs = [spec((1, n)), spec((1, n)), spec((1, 1)),
                spec((CHUNK_W, n)), spec((CHUNK_W, n)), spec((n, CHUNK_W)), spec((n, CHUNK_W)), spec((1, CHUNK_W))]
    out_shapes = [((SLAB_W, SLAB_W), BF16), ((SLAB_W, 2 * sl), BF16), ((2 * sl, SLAB_W), BF16),
                  ((SCAN_ROWS, sl), F32), ((SCAN_ROWS, sl), F32), ((1, SLAB_W), F32)]
    return pl.pallas_call(
        _ssm_prep_kernel,
        grid=(g,),
        in_specs=in_specs,
        out_specs=[slab(s) for s, _ in out_shapes],
        out_shape=[jax.ShapeDtypeStruct((ns,) + s, d) for s, d in out_shapes],
        compiler_params=_cparams("arbitrary"),
        name="ssm_prep",
    )(*args)


def _in_proj_kernel(x_ref, g_ref, wqt_ref, wk_ref, wvt_ref, wu_ref, wg_ref,
                    cos_ref, sin_ref, cost_ref, sint_ref,
                    qt_ref, k_ref, vt_ref, u_ref, sga_ref, sgb_ref):
    x = x_ref[0]
    d = x.shape[-1]
    ms = jnp.mean(x * x, axis=-1, keepdims=True)
    h = (x * lax.rsqrt(ms + RMS_EPS) * g_ref[...]).astype(BF16)
    for c0 in range(0, 2 * d, 512):
        sg = jax.nn.sigmoid(jnp.dot(h, wg_ref[:, c0:c0 + 512], preferred_element_type=F32)).astype(BF16)
        if c0 < d:
            sga_ref[0, :, c0:c0 + 512] = sg
        else:
            sgb_ref[0, :, c0 - d:c0 - d + 512] = sg
    cos, sin = cos_ref[...], sin_ref[...]
    for c0 in range(0, d, 512):
        p = jnp.dot(h, wk_ref[:, c0:c0 + 512], preferred_element_type=F32)
        for hh in range(512 // HEAD_DIM):
            sl = p[:, hh * HEAD_DIM:(hh + 1) * HEAD_DIM]
            r = sl * cos + pltpu.roll(sl, HALF_DIM, axis=1) * sin
            k_ref[0, :, c0 + hh * HEAD_DIM:c0 + (hh + 1) * HEAD_DIM] = r.astype(BF16)
    cost, sint = cost_ref[...], sint_ref[...]
    scale = HEAD_DIM ** -0.5 * math.log2(math.e)
    for c0 in range(0, d, 256):
        pt = lax.dot_general(wqt_ref[c0:c0 + 256, :], h, _NT, preferred_element_type=F32)
        for hh in range(256 // HEAD_DIM):
            r0 = hh * HEAD_DIM
            x1 = pt[r0:r0 + HALF_DIM]
            x2 = pt[r0 + HALF_DIM:r0 + HEAD_DIM]
            qt_ref[0, c0 + r0:c0 + r0 + HALF_DIM, :] = ((x1 * cost - x2 * sint) * scale).astype(BF16)
            qt_ref[0, c0 + r0 + HALF_DIM:c0 + r0 + HEAD_DIM, :] = ((x2 * cost + x1 * sint) * scale).astype(BF16)
    ones = jnp.ones((BF16_SUBLANES, x.shape[0]), BF16)
    for c0 in range(0, d, 256):
        pv = lax.dot_general(wvt_ref[c0:c0 + 256, :], h, _NT, preferred_element_type=F32).astype(BF16)
        for hh in range(256 // HEAD_DIM):
            r0 = (c0 // HEAD_DIM + hh) * V_ROWS
            vt_ref[0, r0:r0 + HEAD_DIM, :] = pv[hh * HEAD_DIM:(hh + 1) * HEAD_DIM]
            vt_ref[0, r0 + HEAD_DIM:r0 + V_ROWS, :] = ones
    pu = jnp.dot(h, wu_ref[...], preferred_element_type=F32)
    for s in range(pu.shape[1] // LANE):
        u_ref[s, 0] = pu[:, s * LANE:(s + 1) * LANE]


def _in_proj(x, g, w_in):
    b, l, d = x.shape
    tm = TOKEN_TILE
    sw = w_in.shape[1] - 5 * d
    wqt = w_in[:, :d].T.astype(BF16)
    wk = w_in[:, d:2 * d].astype(BF16)
    wvt = w_in[:, 2 * d:3 * d].T.astype(BF16)
    wu = w_in[:, 3 * d:3 * d + sw].astype(BF16)
    wg = w_in[:, 3 * d + sw:].astype(BF16)
    inv_freq = ROPE_THETA ** (-jnp.arange(HALF_DIM, dtype=F32) / HALF_DIM)
    ang = jnp.arange(l).astype(F32)[:, None] * inv_freq[None, :]
    cos, sin = jnp.cos(ang), jnp.sin(ang)
    cos2 = jnp.concatenate([cos, cos], axis=1)
    sin2 = jnp.concatenate([-sin, sin], axis=1)
    tok = lambda w: pl.BlockSpec((1, tm, w), lambda bi, i: (bi, i, 0))
    feat = lambda w: pl.BlockSpec((1, w, tm), lambda bi, i: (bi, 0, i))
    sds = jax.ShapeDtypeStruct
    return pl.pallas_call(
        _in_proj_kernel,
        grid=(b, l // tm),
        in_specs=[tok(d), _const_spec((1, d)), _const_spec((d, d)), _const_spec((d, d)), _const_spec((d, d)),
                  _const_spec((d, sw)), _const_spec((d, 2 * d)),
                  pl.BlockSpec((tm, HEAD_DIM), lambda bi, i: (i, 0)),
                  pl.BlockSpec((tm, HEAD_DIM), lambda bi, i: (i, 0)),
                  pl.BlockSpec((HALF_DIM, tm), lambda bi, i: (0, i)),
                  pl.BlockSpec((HALF_DIM, tm), lambda bi, i: (0, i))],
        out_specs=[feat(d), tok(d), feat(N_HEADS * V_ROWS),
                   pl.BlockSpec((sw // LANE, 1, tm, LANE), lambda bi, i: (0, bi, i, 0)), tok(d), tok(d)],
        out_shape=[sds((b, d, l), BF16), sds((b, l, d), BF16), sds((b, N_HEADS * V_ROWS, l), BF16),
                   sds((sw // LANE, b, l, LANE), F32), sds((b, l, d), BF16), sds((b, l, d), BF16)],
        compiler_params=_cparams("parallel", "parallel"),
        name="in_proj",
    )(x, g.reshape(1, d), wqt, wk, wvt, wu, wg, cos2, sin2, cos.T, sin.T)


def _ssm_chunk_kernel(u_ref, mx_ref, gx_ref, hx_ref, apr_ref, api_ref, d_ref,
                      y_ref, ere_ref, eim_ref, xre_ref, xim_ref):
    nk = u_ref.shape[2] // SSM_CHUNK
    sl = ere_ref.shape[1]
    uf = jnp.concatenate([u_ref[0, 0, pl.ds(t, nk, stride=SSM_CHUNK), :] for t in range(SSM_CHUNK)], axis=1)
    u = uf.astype(BF16)
    dot = functools.partial(jnp.dot, preferred_element_type=F32)
    e = dot(u, gx_ref[0])
    ere_ref[...] = e[:, :sl]
    eim_ref[...] = e[:, sl:]
    apr, api = apr_ref[0], api_ref[0]
    row = lax.broadcasted_iota(jnp.int32, apr.shape, 0)
    lvl = [(dd, jnp.broadcast_to(apr[dd - 1:dd], apr.shape), jnp.broadcast_to(api[dd - 1:dd], apr.shape))
           for dd in (1, 2, 4)]

    def tile(t, carry):
        cr, ci = carry
        r0 = pl.multiple_of(t * SCAN_ROWS, SCAN_ROWS)
        zr = ere_ref[pl.ds(r0, SCAN_ROWS), :]
        zi = eim_ref[pl.ds(r0, SCAN_ROWS), :]
        for dd, adr, adi in lvl:
            sr = jnp.where(row >= dd, pltpu.roll(zr, dd, axis=0), 0.0)
            si = jnp.where(row >= dd, pltpu.roll(zi, dd, axis=0), 0.0)
            zr, zi = zr + (adr * sr - adi * si), zi + (adr * si + adi * sr)
        sr = apr * cr - api * ci + zr
        si = apr * ci + api * cr + zi
        xre_ref[pl.ds(r0, SCAN_ROWS), :] = jnp.where(row >= 1, pltpu.roll(sr, 1, axis=0), cr)
        xim_ref[pl.ds(r0, SCAN_ROWS), :] = jnp.where(row >= 1, pltpu.roll(si, 1, axis=0), ci)
        last = SCAN_ROWS - 1
        return (jnp.broadcast_to(sr[last:last + 1], sr.shape), jnp.broadcast_to(si[last:last + 1], si.shape))

    zero = jnp.zeros(apr.shape, F32)
    lax.fori_loop(0, nk // SCAN_ROWS, tile, (zero, zero))
    x0 = jnp.concatenate([xre_ref[...], xim_ref[...]], axis=1).astype(BF16)
    y = jax.nn.gelu(dot(u, mx_ref[0]) + dot(x0, hx_ref[0]) + d_ref[0] * uf)
    for t in range(SSM_CHUNK):
        y_ref[0, 0, pl.ds(t, nk, stride=SSM_CHUNK), :] = y[:, t * LANE:(t + 1) * LANE]


def _ssm_chunk(u4, mx, gx, hx, apx, aix, dx):
    ns, b, l, lane = u4.shape
    nk = l // SSM_CHUNK
    sl = apx.shape[-1]
    per_slab = lambda r, c: pl.BlockSpec((1, r, c), lambda s, bi: (s, 0, 0))
    rows = pl.BlockSpec((1, 1, l, lane), lambda s, bi: (s, bi, 0, 0))
    return pl.pallas_call(
        _ssm_chunk_kernel,
        grid=(ns, b),
        in_specs=[rows, per_slab(SLAB_W, SLAB_W), per_slab(SLAB_W, 2 * sl), per_slab(2 * sl, SLAB_W),
                  per_slab(SCAN_ROWS, sl), per_slab(SCAN_ROWS, sl), per_slab(1, SLAB_W)],
        out_specs=rows,
        out_shape=jax.ShapeDtypeStruct(u4.shape, F32),
        scratch_shapes=[pltpu.VMEM((nk, sl), F32) for _ in range(4)],
        compiler_params=_cparams("parallel", "parallel"),
        name="ssm_chunk",
    )(u4, mx, gx, hx, apx, aix, dx)


def _moba_kernel(qt_ref, k_ref, vt_ref, o_ref, *scratch):
    i = pl.program_id(2)
    blk = MOBA_BLOCK
    qw = Q_BLOCKS * blk
    nb = k_ref.shape[1] // blk
    fb = i * Q_BLOCKS
    heads = range(ATTN_HEADS)
    hsl = lambda hp: slice(hp * HEAD_DIM, (hp + 1) * HEAD_DIM)
    kmean_ref, sel_ref, m_ref, acc_ref, s_ref, cm_ref, ksplit_ref = (
        scratch[r * ATTN_HEADS:(r + 1) * ATTN_HEADS] for r in range(7))

    @pl.when(i == 0)
    def _():
        for hp in heads:
            for j in range(nb):
                kb = k_ref[0, j * blk:(j + 1) * blk, hsl(hp)].astype(F32)
                kmean_ref[hp][j:j + 1, :] = jnp.mean(kb, axis=0, keepdims=True)
            rest = kmean_ref[hp][...]
            for part in range(3):
                piece = rest.astype(BF16)
                ksplit_ref[hp][part * nb:(part + 1) * nb, :] = piece
                rest = rest - piece.astype(F32)

    vsl = lambda hp: slice(hp * V_ROWS, (hp + 1) * V_ROWS)

    def select_blocks(hp, qt):
        g3 = jnp.dot(ksplit_ref[hp][...], qt, preferred_element_type=F32)
        gate = g3[:nb] + g3[nb:2 * nb] + g3[2 * nb:]
        brow = lax.broadcasted_iota(jnp.int32, gate.shape, 0)
        own = fb + lax.broadcasted_iota(jnp.int32, (1, qw), 1) // blk
        work = jnp.where(brow < own, gate, -jnp.inf)
        sel = jnp.zeros(gate.shape, F32)
        for _ in range(MOBA_TOPK):
            mx = jnp.max(work, axis=0, keepdims=True)
            idx = jnp.min(jnp.where(work == mx, brow, nb), axis=0, keepdims=True)
            hit = brow == idx
            sel = jnp.where(hit, jnp.where(idx < own, 1.0, 0.0), sel)
            work = jnp.where(hit, -jnp.inf, work)
        sel_ref[hp][...] = sel

    def item_blocks(g):
        first, count = (fb, Q_BLOCKS) if g is None else (g * KV_GROUP, KV_GROUP)
        if isinstance(first, int):
            return [slice((first + jl) * blk, (first + jl + 1) * blk) for jl in range(count)]
        return [pl.ds(pl.multiple_of((first + jl) * blk, blk), blk) for jl in range(count)]

    def score_steps(hp, qt, g):
        buf = s_ref[hp]

        def step(jl, rows):
            if g is None:
                c0 = jl * blk
                s = jnp.dot(k_ref[0, rows, hsl(hp)], qt[:, c0:], preferred_element_type=F32)
                krow = lax.broadcasted_iota(jnp.int32, s.shape, 0)
                qcol = lax.broadcasted_iota(jnp.int32, s.shape, 1)
                picked = sel_ref[hp][pl.ds(fb + jl, 1), c0:] > 0.0
                s = jnp.where(qcol >= blk, jnp.where(picked, s, NEG_BIG), jnp.where(krow <= qcol, s, NEG_BIG))
                buf[jl * blk:(jl + 1) * blk, c0:] = s
                cmax = jnp.max(s, axis=0, keepdims=True)
                if c0:
                    cmax = jnp.concatenate([jnp.full((1, c0), NEG_BIG, F32), cmax], axis=1)
                cm_ref[hp][jl:jl + 1, :] = cmax
                return
            s = jnp.dot(k_ref[0, rows, hsl(hp)], qt, preferred_element_type=F32)
            buf[jl * blk:(jl + 1) * blk, :] = s
            cm_ref[hp][jl:jl + 1, :] = jnp.max(s, axis=0, keepdims=True)

        return [functools.partial(step, jl, rows) for jl, rows in enumerate(item_blocks(g))]

    def weight_steps(hp, g):
        buf = s_ref[hp]
        refs = []

        def step(jl, rows):
            if jl == 0:
                cmax = [cm_ref[hp][j:j + 1, :] for j in range(Q_BLOCKS if g is None else KV_GROUP)]
                if g is None:
                    m_new = functools.reduce(jnp.maximum, cmax)
                    refs.extend(m_new for _ in cmax)
                else:
                    picked = [jnp.logical_and(sel_ref[hp][pl.ds(g * KV_GROUP + j, 1), :] > 0.0,
                                              g * KV_GROUP + j < fb) for j in range(KV_GROUP)]
                    m_old = m_ref[hp][...]
                    m_new = m_old
                    for j in range(KV_GROUP):
                        m_new = jnp.maximum(m_new, jnp.where(picked[j], cmax[j], NEG_BIG))
                    acc_ref[hp][...] = jnp.exp2(m_old - m_new) * acc_ref[hp][...]
                    refs.extend(jnp.where(picked[j], m_new, -NEG_BIG) for j in range(KV_GROUP))
                m_ref[hp][...] = m_new
            c0 = jl * blk if g is None else 0
            p = jnp.exp2(buf[jl * blk:(jl + 1) * blk, c0:] - refs[jl][:, c0:]).astype(BF16)
            pv = jnp.dot(vt_ref[0, vsl(hp), rows], p, preferred_element_type=F32)
            if g is None and jl == 0:
                acc_ref[hp][...] = pv
            else:
                acc_ref[hp][:, c0:] = acc_ref[hp][:, c0:] + pv

        return [functools.partial(step, jl, rows) for jl, rows in enumerate(item_blocks(g))]

    def interleave(w_steps, s_steps):
        for n in range(max(len(w_steps) + SCORE_LEAD, len(s_steps))):
            if n < len(s_steps):
                s_steps[n]()
            if SCORE_LEAD <= n < len(w_steps) + SCORE_LEAD:
                w_steps[n - SCORE_LEAD]()

    qts = [qt_ref[0, hsl(hp), :] for hp in heads]
    for hp in heads:
        select_blocks(hp, qts[hp])
    interleave([], score_steps(0, qts[0], None))
    for hp in heads:
        nhp = (hp + 1) % ATTN_HEADS
        interleave(weight_steps(hp, None), score_steps(nhp, qts[nhp], None if nhp else 0))

    ngroups = (fb + KV_GROUP - 1) // KV_GROUP

    def past_group(g, _, last=False):
        for hp in heads:
            nhp = (hp + 1) % ATTN_HEADS
            ahead = [] if (last and not nhp) else score_steps(nhp, qts[nhp], g if nhp else g + 1)
            interleave(weight_steps(hp, g), ahead)
        return 0

    lax.fori_loop(0, ngroups - 1, past_group, 0)

    @pl.when(ngroups > 0)
    def _():
        past_group(ngroups - 1, 0, last=True)

    for hp in heads:
        ot = acc_ref[hp][:HEAD_DIM, :] / acc_ref[hp][HEAD_DIM:HEAD_DIM + 1, :]
        o_ref[0, :, hsl(hp)] = ot.T.astype(o_ref.dtype)


def _moba_attention(qt, k, vt):
    b, l, d = k.shape
    nb = l // MOBA_BLOCK
    assert nb % KV_GROUP == 0 and nb % Q_BLOCKS == 0
    hw = ATTN_HEADS * HEAD_DIM
    qw = Q_BLOCKS * MOBA_BLOCK
    return pl.pallas_call(
        _moba_kernel,
        grid=(b, N_HEADS // ATTN_HEADS, nb // Q_BLOCKS),
        in_specs=[pl.BlockSpec((1, hw, qw), lambda bi, h, i: (bi, h, i)),
                  pl.BlockSpec((1, l, hw), lambda bi, h, i: (bi, 0, h)),
                  pl.BlockSpec((1, ATTN_HEADS * V_ROWS, l), lambda bi, h, i: (bi, h, 0))],
        out_specs=pl.BlockSpec((1, qw, hw), lambda bi, h, i: (bi, i, h)),
        out_shape=jax.ShapeDtypeStruct((b, l, d), BF16),
        scratch_shapes=[pltpu.VMEM(shape, dtype)
                        for shape, dtype in (((nb, HEAD_DIM), F32), ((nb, qw), F32), ((1, qw), F32),
                                             ((V_ROWS, qw), F32), ((KV_GROUP * MOBA_BLOCK, qw), F32),
                                             ((KV_GROUP, qw), F32), ((3 * nb, HEAD_DIM), BF16))
                        for _ in range(ATTN_HEADS)],
        compiler_params=_cparams("parallel", "parallel", "arbitrary"),
        name="moba_attn",
    )(qt, k, vt)


def _merge_mlp_kernel(x_ref, oa_ref, sga_ref, sgb_ref, y_ref, wglu_ref, wout_ref, g1_ref, wup_ref, wdn_ref, g2_ref,
                      o_ref, *, final_norm):
    d = x_ref.shape[-1]
    ff = wup_ref.shape[1]
    y = jnp.concatenate([y_ref[s, 0] for s in range(y_ref.shape[0])], axis=1)
    hgl = jnp.dot(y.astype(BF16), wglu_ref[...], preferred_element_type=F32)
    ob = hgl[:, :d] * jax.nn.sigmoid(hgl[:, d:])
    mixed = sga_ref[0].astype(F32) * oa_ref[0].astype(F32) + sgb_ref[0].astype(F32) * ob
    x = x_ref[0] + jnp.dot(mixed.astype(BF16), wout_ref[...], preferred_element_type=F32)
    ms = jnp.mean(x * x, axis=-1, keepdims=True)
    h = (x * lax.rsqrt(ms + RMS_EPS) * g1_ref[...]).astype(BF16)
    acc = x
    for c0 in range(0, ff, d):
        a = jnp.maximum(jnp.dot(h, wup_ref[:, c0:c0 + d], preferred_element_type=F32), 0.0)
        acc = acc + jnp.dot((a * a).astype(BF16), wdn_ref[c0:c0 + d, :], preferred_element_type=F32)
    if final_norm:
        ms2 = jnp.mean(acc * acc, axis=-1, keepdims=True)
        acc = acc * lax.rsqrt(ms2 + RMS_EPS) * g2_ref[...]
    o_ref[0] = acc


def _merge_mlp(x, oa, sga, sgb, y, w_glu, w_out, g1, w_up, w_down, g2, final_norm):
    b, l, d = x.shape
    tm = TOKEN_TILE
    ns, lane = y.shape[0], y.shape[-1]
    sw = ns * lane
    ff = w_up.shape[1]
    tok = lambda w: pl.BlockSpec((1, tm, w), lambda bi, i: (bi, i, 0))
    return pl.pallas_call(
        functools.partial(_merge_mlp_kernel, final_norm=final_norm),
        grid=(b, l // tm),
        in_specs=[tok(d), tok(d), tok(d), tok(d), pl.BlockSpec((ns, 1, tm, lane), lambda bi, i: (0, bi, i, 0)),
                  _resident_spec((sw, 2 * d)), _resident_spec((d, d)), _const_spec((1, d)),
                  _resident_spec((d, ff)), _resident_spec((ff, d)), _const_spec((1, d))],
        out_specs=tok(d),
        out_shape=jax.ShapeDtypeStruct((b, l, d), F32),
        compiler_params=_cparams("parallel", "parallel"),
        name="merge_mlp",
    )(x, oa, sga, sgb, y, w_glu.astype(BF16), w_out.astype(BF16), g1.reshape(1, d),
      w_up.astype(BF16), w_down.astype(BF16), g2.reshape(1, d))


def kernel(x, norm_mix_g, w_in, lam_re, lam_im, log_step, b_re, b_im, c_re, c_im, d_skip, w_glu, w_out,
           norm_mlp_g, w_up, w_down, norm_final_g):
    b, l, d = x.shape
    depth = w_in.shape[0]
    assert d == N_HEADS * HEAD_DIM and l % TOKEN_TILE == 0 and l % (SSM_CHUNK * SCAN_ROWS) == 0
    for i in range(depth):
        slab_mats = _ssm_prep(lam_re[i], lam_im[i], log_step[i], b_re[i], b_im[i], c_re[i], c_im[i], d_skip[i])
        qt, k, vt, u, sga, sgb = _in_proj(x, norm_mix_g[i], w_in[i])
        y = _ssm_chunk(u, *slab_mats)
        oa = _moba_attention(qt, k, vt)
        x = _merge_mlp(x, oa, sga, sgb, y, w_glu[i], w_out[i], norm_mlp_g[i], w_up[i], w_down[i], norm_final_g,
                       final_norm=(i == depth - 1))
    return x
```

```python
import functools
import math

import jax
import jax.numpy as jnp
from jax import lax
from jax.experimental import pallas as pl
from jax.experimental.pallas import tpu as pltpu

F32 = jnp.float32
BF16 = jnp.bfloat16

N_HEADS = 8
HEAD_DIM = 128
HALF_DIM = HEAD_DIM // 2
MOBA_BLOCK = 256
MOBA_TOPK = 3
ATTN_HEADS = 2
Q_BLOCKS = 4
KV_GROUP = 4
SCORE_LEAD = 1
BF16_SUBLANES = 16
V_ROWS = HEAD_DIM + BF16_SUBLANES
ROPE_THETA = 10000.0
SSM_GROUP = 16
SSM_STATE = 64
SSM_CHUNK = 8
CHUNK_W = SSM_CHUNK * SSM_GROUP
SCAN_ROWS = 8
LANE = 128
GROUPS_PER_SLAB = LANE // SSM_GROUP
SLAB_W = SSM_CHUNK * LANE
RMS_EPS = 1e-6
NEG_BIG = -1e30
TOKEN_TILE = 512
IN_PROJ_TILE = 1024
VMEM_LIMIT = 56 * 1024 * 1024

_NT = (((1,), (1,)), ((), ()))


def _cparams(*sem):
    return pltpu.CompilerParams(dimension_semantics=sem, vmem_limit_bytes=VMEM_LIMIT)


def _const_spec(shape):
    nd = len(shape)
    return pl.BlockSpec(shape, lambda *_: (0,) * nd)


def _resident_spec(shape):
    nd = len(shape)
    return pl.BlockSpec(shape, lambda *_: (0,) * nd, pipeline_mode=pl.Buffered(1))


def _cexp(lr, li, step, e):
    mag = jnp.exp(lr * step * e)
    ang = li * step * e
    return mag * jnp.cos(ang), mag * jnp.sin(ang)


def _spread(rows, cols, target_fn):
    r = lax.broadcasted_iota(jnp.int32, (rows, cols), 0)
    c = lax.broadcasted_iota(jnp.int32, (rows, cols), 1)
    return jnp.where(c == target_fn(r), 1.0, 0.0).astype(F32)


def _ssm_prep_kernel(lrr_ref, lir_ref, ls_ref, btr_ref, bti_ref, ctr_ref, cti_ref, dt_ref,
                     mx_ref, gx_ref, hx_ref, apx_ref, aix_ref, dx_ref):
    n = lrr_ref.shape[-1]
    gl = lax.rem(pl.program_id(0), GROUPS_PER_SLAB)
    hp = lax.Precision.HIGHEST
    place = functools.partial(jnp.dot, precision=hp, preferred_element_type=F32)
    place16 = lambda a, spread: jnp.dot(a.astype(BF16), spread.astype(BF16), preferred_element_type=F32)
    to_slab = _spread(CHUNK_W, SLAB_W, lambda r: lax.shift_right_logical(r, 4) * LANE + gl * SSM_GROUP + (r & 15))
    to_state = _spread(n, GROUPS_PER_SLAB * n, lambda r: gl * n + r)
    to_state2 = _spread(2 * n, 2 * GROUPS_PER_SLAB * n,
                        lambda r: jnp.where(r >= n, GROUPS_PER_SLAB * n - n, 0) + gl * n + r)
    step = jnp.exp(ls_ref[0])
    lr_r, li_r = lrr_ref[0], lir_ref[0]
    prow = lax.broadcasted_iota(jnp.int32, (SSM_CHUNK + SCAN_ROWS, 1), 0)
    pw_r, pw_i = _cexp(lr_r, li_r, step,
                       jnp.where(prow <= SSM_CHUNK, prow, (prow - SSM_CHUNK + 1) * SSM_CHUNK).astype(F32))
    tau = lax.shift_right_logical(lax.broadcasted_iota(jnp.int32, pw_r.shape[:1] + (CHUNK_W,), 1), 4)
    pick = lambda shift: jnp.where(tau + shift == lax.broadcasted_iota(jnp.int32, tau.shape, 0), 1.0, 0.0)
    on_sublanes = lambda p, shift: lax.dot_general(p, pick(shift), (((0,), (0,)), ((), ())), precision=hp,
                                                   preferred_element_type=F32)
    ar, ai = pw_r[1:2], pw_i[1:2]
    den = lr_r * lr_r + li_r * li_r
    nr, ni = ar - 1.0, ai
    cr = (nr * lr_r + ni * li_r) / den
    ci = (ni * lr_r - nr * li_r) / den
    btr, bti = btr_ref[0], bti_ref[0]
    bbr = cr * btr - ci * bti
    bbi = cr * bti + ci * btr
    rows_of = lambda p: jnp.concatenate(
        [jnp.broadcast_to(p[SSM_CHUNK - 1 - s:SSM_CHUNK - s], (SSM_GROUP, n)) for s in range(SSM_CHUNK)], axis=0)
    pr, pi_ = rows_of(pw_r), rows_of(pw_i)
    gcat = jnp.concatenate([pr * bbr - pi_ * bbi, pr * bbi + pi_ * bbr], axis=1)
    gwide = place16(gcat, to_state2).astype(gx_ref.dtype)
    row0 = pl.multiple_of(gl * SSM_GROUP, SSM_GROUP)
    for s in range(SSM_CHUNK):
        gx_ref[0, pl.ds(s * LANE + row0, SSM_GROUP), :] = gwide[s * SSM_GROUP:(s + 1) * SSM_GROUP]
    ctr, cti = ctr_ref[0], cti_ref[0]
    wr, wi = on_sublanes(pw_r, 0), on_sublanes(pw_i, 0)
    p_r = ctr * wr - cti * wi
    p_i = ctr * wi + cti * wr
    r0 = place(bbr[:SSM_GROUP], p_r) - place(bbi[:SSM_GROUP], p_i)
    r0w = place16(r0, to_slab)
    lane = lax.broadcasted_iota(jnp.int32, (SSM_GROUP, SLAB_W), 1)
    for s in range(SSM_CHUNK):
        blk = r0w if s == 0 else pltpu.roll(r0w, LANE * s, axis=1)
        blk = jnp.where(lane >= LANE * s, blk, 0.0)
        mx_ref[0, pl.ds(s * LANE + row0, SSM_GROUP), :] = blk.astype(mx_ref.dtype)
    hr, hi = on_sublanes(pw_r, 1), on_sublanes(pw_i, 1)
    st0 = pl.multiple_of(gl * n, n)
    hx_ref[0, pl.ds(st0, n), :] = place16(ctr * hr - cti * hi, to_slab).astype(hx_ref.dtype)
    hx_ref[0, pl.ds(GROUPS_PER_SLAB * n + st0, n), :] = place16(-(ctr * hi + cti * hr), to_slab).astype(hx_ref.dtype)
    qr, qi = pw_r[SSM_CHUNK:], pw_i[SSM_CHUNK:]
    shared = ((apx_ref, place(qr, to_state)), (aix_ref, place(qi, to_state)), (dx_ref, place(dt_ref[0], to_slab)))

    @pl.when(gl == 0)
    def _():
        for ref, val in shared:
            ref[0] = val

    @pl.when(gl != 0)
    def _():
        for ref, val in shared:
            ref[0] = ref[0] + val


def _ssm_prep(lam_re, lam_im, log_step, b_re, b_im, c_re, c_im, d_skip):
    g, n = lam_re.shape
    ns = g // GROUPS_PER_SLAB
    sl = GROUPS_PER_SLAB * n
    bt = lambda b: jnp.tile(jnp.swapaxes(b, 1, 2), (1, SSM_CHUNK, 1))
    ct = lambda m: jnp.tile(jnp.swapaxes(m, 1, 2), (1, 1, SSM_CHUNK))
    args = (lam_re.reshape(g, 1, n), lam_im.reshape(g, 1, n), log_step.reshape(g, 1, 1),
            bt(b_re), bt(b_im), ct(c_re), ct(c_im),
            jnp.tile(d_skip.astype(F32), (1, SSM_CHUNK)).reshape(g, 1, CHUNK_W))
    spec = lambda shp: pl.BlockSpec((1,) + shp, lambda i: (i, 0, 0))
    slab = lambda shp: pl.BlockSpec((1,) + shp, lambda i: (i // GROUPS_PER_SLAB, 0, 0))
    in_specs = [spec((1, n)), spec((1, n)), spec((1, 1)),
                spec((CHUNK_W, n)), spec((CHUNK_W, n)), spec((n, CHUNK_W)), spec((n, CHUNK_W)), spec((1, CHUNK_W))]
    out_shapes = [((SLAB_W, SLAB_W), BF16), ((SLAB_W, 2 * sl), BF16), ((2 * sl, SLAB_W), BF16),
                  ((SCAN_ROWS, sl), F32), ((SCAN_ROWS, sl), F32), ((1, SLAB_W), F32)]
    return pl.pallas_call(
        _ssm_prep_kernel,
        grid=(g,),
        in_specs=in_specs,
        out_specs=[slab(s) for s, _ in out_shapes],
        out_shape=[jax.ShapeDtypeStruct((ns,) + s, d) for s, d in out_shapes],
        compiler_params=_cparams("arbitrary"),
        name="ssm_prep",
    )(*args)


def _in_proj_kernel(x_ref, g_ref, wqt_ref, wk_ref, wvt_ref, wu_ref, wg_ref,
                    cos_ref, sin_ref, cost_ref, sint_ref,
                    qt_ref, k_ref, vt_ref, u_ref, sga_ref, sgb_ref):
    x = x_ref[0]
    d = x.shape[-1]
    ms = jnp.mean(x * x, axis=-1, keepdims=True)
    h = (x * lax.rsqrt(ms + RMS_EPS) * g_ref[...]).astype(BF16)
    for c0 in range(0, 2 * d, 512):
        sg = jax.nn.sigmoid(jnp.dot(h, wg_ref[:, c0:c0 + 512], preferred_element_type=F32)).astype(BF16)
        if c0 < d:
            sga_ref[0, :, c0:c0 + 512] = sg
        else:
            sgb_ref[0, :, c0 - d:c0 - d + 512] = sg
    cos, sin = cos_ref[...], sin_ref[...]
    for c0 in range(0, d, 512):
        p = jnp.dot(h, wk_ref[:, c0:c0 + 512], preferred_element_type=F32)
        for hh in range(512 // HEAD_DIM):
            sl = p[:, hh * HEAD_DIM:(hh + 1) * HEAD_DIM]
            r = sl * cos + pltpu.roll(sl, HALF_DIM, axis=1) * sin
            k_ref[0, :, c0 + hh * HEAD_DIM:c0 + (hh + 1) * HEAD_DIM] = r.astype(BF16)
    cost, sint = cost_ref[...], sint_ref[...]
    scale = HEAD_DIM ** -0.5 * math.log2(math.e)
    for c0 in range(0, d, 256):
        pt = lax.dot_general(wqt_ref[c0:c0 + 256, :], h, _NT, preferred_element_type=F32)
        for hh in range(256 // HEAD_DIM):
            r0 = hh * HEAD_DIM
            x1 = pt[r0:r0 + HALF_DIM]
            x2 = pt[r0 + HALF_DIM:r0 + HEAD_DIM]
            qt_ref[0, c0 + r0:c0 + r0 + HALF_DIM, :] = ((x1 * cost - x2 * sint) * scale).astype(BF16)
            qt_ref[0, c0 + r0 + HALF_DIM:c0 + r0 + HEAD_DIM, :] = ((x2 * cost + x1 * sint) * scale).astype(BF16)
    ones = jnp.ones((BF16_SUBLANES, x.shape[0]), BF16)
    for c0 in range(0, d, 256):
        pv = lax.dot_general(wvt_ref[c0:c0 + 256, :], h, _NT, preferred_element_type=F32).astype(BF16)
        for hh in range(256 // HEAD_DIM):
            r0 = (c0 // HEAD_DIM + hh) * V_ROWS
            vt_ref[0, r0:r0 + HEAD_DIM, :] = pv[hh * HEAD_DIM:(hh + 1) * HEAD_DIM]
            vt_ref[0, r0 + HEAD_DIM:r0 + V_ROWS, :] = ones
    pu = jnp.dot(h, wu_ref[...], preferred_element_type=F32)
    for s in range(pu.shape[1] // LANE):
        u_ref[s, 0] = pu[:, s * LANE:(s + 1) * LANE]


def _in_proj(x, g, w_in):
    b, l, d = x.shape
    tm = IN_PROJ_TILE
    sw = w_in.shape[1] - 5 * d
    wqt = w_in[:, :d].T.astype(BF16)
    wk = w_in[:, d:2 * d].astype(BF16)
    wvt = w_in[:, 2 * d:3 * d].T.astype(BF16)
    wu = w_in[:, 3 * d:3 * d + sw].astype(BF16)
    wg = w_in[:, 3 * d + sw:].astype(BF16)
    inv_freq = ROPE_THETA ** (-jnp.arange(HALF_DIM, dtype=F32) / HALF_DIM)
    ang = jnp.arange(l).astype(F32)[:, None] * inv_freq[None, :]
    cos, sin = jnp.cos(ang), jnp.sin(ang)
    cos2 = jnp.concatenate([cos, cos], axis=1)
    sin2 = jnp.concatenate([-sin, sin], axis=1)
    tok = lambda w: pl.BlockSpec((1, tm, w), lambda bi, i: (bi, i, 0))
    feat = lambda w: pl.BlockSpec((1, w, tm), lambda bi, i: (bi, 0, i))
    sds = jax.ShapeDtypeStruct
    return pl.pallas_call(
        _in_proj_kernel,
        grid=(b, l // tm),
        in_specs=[tok(d), _const_spec((1, d)), _resident_spec((d, d)), _resident_spec((d, d)), _resident_spec((d, d)),
                  _resident_spec((d, sw)), _resident_spec((d, 2 * d)),
                  pl.BlockSpec((tm, HEAD_DIM), lambda bi, i: (i, 0)),
                  pl.BlockSpec((tm, HEAD_DIM), lambda bi, i: (i, 0)),
                  pl.BlockSpec((HALF_DIM, tm), lambda bi, i: (0, i)),
                  pl.BlockSpec((HALF_DIM, tm), lambda bi, i: (0, i))],
        out_specs=[feat(d), tok(d), feat(N_HEADS * V_ROWS),
                   pl.BlockSpec((sw // LANE, 1, tm, LANE), lambda bi, i: (0, bi, i, 0)), tok(d), tok(d)],
        out_shape=[sds((b, d, l), BF16), sds((b, l, d), BF16), sds((b, N_HEADS * V_ROWS, l), BF16),
                   sds((sw // LANE, b, l, LANE), F32), sds((b, l, d), BF16), sds((b, l, d), BF16)],
        compiler_params=_cparams("parallel", "parallel"),
        name="in_proj",
    )(x, g.reshape(1, d), wqt, wk, wvt, wu, wg, cos2, sin2, cos.T, sin.T)


def _ssm_chunk_kernel(u_ref, mx_ref, gx_ref, hx_ref, apr_ref, api_ref, d_ref,
                      y_ref, ere_ref, eim_ref, xre_ref, xim_ref):
    nk = u_ref.shape[2] // SSM_CHUNK
    sl = ere_ref.shape[1]
    uf = jnp.concatenate([u_ref[0, 0, pl.ds(t, nk, stride=SSM_CHUNK), :] for t in range(SSM_CHUNK)], axis=1)
    u = uf.astype(BF16)
    dot = functools.partial(jnp.dot, preferred_element_type=F32)
    e = dot(u, gx_ref[0])
    ere_ref[...] = e[:, :sl]
    eim_ref[...] = e[:, sl:]
    apr, api = apr_ref[0], api_ref[0]
    row = lax.broadcasted_iota(jnp.int32, apr.shape, 0)
    lvl = [(dd, jnp.where(row >= dd, jnp.broadcast_to(apr[dd - 1:dd], apr.shape), 0.0),
            jnp.where(row >= dd, jnp.broadcast_to(api[dd - 1:dd], apr.shape), 0.0)) for dd in (1, 2, 4)]

    def tile(t, carry):
        cr, ci = carry
        r0 = pl.multiple_of(t * SCAN_ROWS, SCAN_ROWS)
        zr = ere_ref[pl.ds(r0, SCAN_ROWS), :]
        zi = eim_ref[pl.ds(r0, SCAN_ROWS), :]
        for dd, adr, adi in lvl:
            sr = pltpu.roll(zr, dd, axis=0)
            si = pltpu.roll(zi, dd, axis=0)
            zr, zi = zr + (adr * sr - adi * si), zi + (adr * si + adi * sr)
        sr = apr * cr - api * ci + zr
        si = apr * ci + api * cr + zi
        xre_ref[pl.ds(r0, SCAN_ROWS), :] = jnp.where(row >= 1, pltpu.roll(sr, 1, axis=0), cr)
        xim_ref[pl.ds(r0, SCAN_ROWS), :] = jnp.where(row >= 1, pltpu.roll(si, 1, axis=0), ci)
        last = SCAN_ROWS - 1
        return (jnp.broadcast_to(sr[last:last + 1], sr.shape), jnp.broadcast_to(si[last:last + 1], si.shape))

    zero = jnp.zeros(apr.shape, F32)
    lax.fori_loop(0, nk // SCAN_ROWS, tile, (zero, zero))
    x0 = jnp.concatenate([xre_ref[...], xim_ref[...]], axis=1).astype(BF16)
    half = SLAB_W // 2
    intra = jnp.concatenate([dot(u[:, :half], mx_ref[0, :half, :half]), dot(u, mx_ref[0, :, half:])], axis=1)
    y = jax.nn.gelu(intra + dot(x0, hx_ref[0]) + d_ref[0] * uf)
    for t in range(SSM_CHUNK):
        y_ref[0, 0, pl.ds(t, nk, stride=SSM_CHUNK), :] = y[:, t * LANE:(t + 1) * LANE]


def _ssm_chunk(u4, mx, gx, hx, apx, aix, dx):
    ns, b, l, lane = u4.shape
    nk = l // SSM_CHUNK
    sl = apx.shape[-1]
    per_slab = lambda r, c: pl.BlockSpec((1, r, c), lambda s, bi: (s, 0, 0))
    rows = pl.BlockSpec((1, 1, l, lane), lambda s, bi: (s, bi, 0, 0))
    return pl.pallas_call(
        _ssm_chunk_kernel,
        grid=(ns, b),
        in_specs=[rows, per_slab(SLAB_W, SLAB_W), per_slab(SLAB_W, 2 * sl), per_slab(2 * sl, SLAB_W),
                  per_slab(SCAN_ROWS, sl), per_slab(SCAN_ROWS, sl), per_slab(1, SLAB_W)],
        out_specs=rows,
        out_shape=jax.ShapeDtypeStruct(u4.shape, F32),
        scratch_shapes=[pltpu.VMEM((nk, sl), F32) for _ in range(4)],
        compiler_params=_cparams("parallel", "parallel"),
        name="ssm_chunk",
    )(u4, mx, gx, hx, apx, aix, dx)


def _moba_kernel(qt_ref, k_ref, vt_ref, o_ref, *scratch):
    i = pl.program_id(2)
    blk = MOBA_BLOCK
    qw = Q_BLOCKS * blk
    nb = k_ref.shape[1] // blk
    fb = i * Q_BLOCKS
    heads = range(ATTN_HEADS)
    hsl = lambda hp: slice(hp * HEAD_DIM, (hp + 1) * HEAD_DIM)
    kmean_ref, sel_ref, m_ref, acc_ref, s_ref, cm_ref, ksplit_ref = (
        scratch[r * ATTN_HEADS:(r + 1) * ATTN_HEADS] for r in range(7))

    @pl.when(i == 0)
    def _():
        for hp in heads:
            for j in range(nb):
                kb = k_ref[0, j * blk:(j + 1) * blk, hsl(hp)].astype(F32)
                kmean_ref[hp][j:j + 1, :] = jnp.mean(kb, axis=0, keepdims=True)
            rest = kmean_ref[hp][...]
            for part in range(3):
                piece = rest.astype(BF16)
                ksplit_ref[hp][part * nb:(part + 1) * nb, :] = piece
                rest = rest - piece.astype(F32)

    vsl = lambda hp: slice(hp * V_ROWS, (hp + 1) * V_ROWS)

    def select_blocks(hp, qt):
        g3 = jnp.dot(ksplit_ref[hp][...], qt, preferred_element_type=F32)
        gate = g3[:nb] + g3[nb:2 * nb] + g3[2 * nb:]
        brow = lax.broadcasted_iota(jnp.int32, gate.shape, 0)
        own = fb + lax.broadcasted_iota(jnp.int32, (1, qw), 1) // blk
        work = jnp.where(brow < own, gate, -jnp.inf)
        sel = jnp.zeros(gate.shape, F32)
        for _ in range(MOBA_TOPK):
            mx = jnp.max(work, axis=0, keepdims=True)
            idx = jnp.min(jnp.where(work == mx, brow, nb), axis=0, keepdims=True)
            hit = brow == idx
            sel = jnp.where(hit, jnp.where(idx < own, 1.0, 0.0), sel)
            work = jnp.where(hit, -jnp.inf, work)
        sel_ref[hp][...] = sel

    def item_blocks(g):
        first, count = (fb, Q_BLOCKS) if g is None else (g * KV_GROUP, KV_GROUP)
        if isinstance(first, int):
            return [slice((first + jl) * blk, (first + jl + 1) * blk) for jl in range(count)]
        return [pl.ds(pl.multiple_of((first + jl) * blk, blk), blk) for jl in range(count)]

    def score_steps(hp, qt, g):
        buf = s_ref[hp]

        def step(jl, rows):
            if g is None:
                c0 = jl * blk
                s = jnp.dot(k_ref[0, rows, hsl(hp)], qt[:, c0:], preferred_element_type=F32)
                krow = lax.broadcasted_iota(jnp.int32, s.shape, 0)
                qcol = lax.broadcasted_iota(jnp.int32, s.shape, 1)
                picked = sel_ref[hp][pl.ds(fb + jl, 1), c0:] > 0.0
                s = jnp.where(qcol >= blk, jnp.where(picked, s, NEG_BIG), jnp.where(krow <= qcol, s, NEG_BIG))
                buf[jl * blk:(jl + 1) * blk, c0:] = s
                cmax = jnp.max(s, axis=0, keepdims=True)
                if c0:
                    cmax = jnp.concatenate([jnp.full((1, c0), NEG_BIG, F32), cmax], axis=1)
                cm_ref[hp][jl:jl + 1, :] = cmax
                return
            s = jnp.dot(k_ref[0, rows, hsl(hp)], qt, preferred_element_type=F32)
            buf[jl * blk:(jl + 1) * blk, :] = s
            cm_ref[hp][jl:jl + 1, :] = jnp.max(s, axis=0, keepdims=True)

        return [functools.partial(step, jl, rows) for jl, rows in enumerate(item_blocks(g))]

    def weight_steps(hp, g):
        buf = s_ref[hp]
        refs = []

        def step(jl, rows):
            if jl == 0:
                cmax = [cm_ref[hp][j:j + 1, :] for j in range(Q_BLOCKS if g is None else KV_GROUP)]
                if g is None:
                    m_new = functools.reduce(jnp.maximum, cmax)
                    refs.extend(m_new for _ in cmax)
                else:
                    picked = [jnp.logical_and(sel_ref[hp][pl.ds(g * KV_GROUP + j, 1), :] > 0.0,
                                              g * KV_GROUP + j < fb) for j in range(KV_GROUP)]
                    m_old = m_ref[hp][...]
                    m_new = m_old
                    for j in range(KV_GROUP):
                        m_new = jnp.maximum(m_new, jnp.where(picked[j], cmax[j], NEG_BIG))
                    acc_ref[hp][...] = jnp.exp2(m_old - m_new) * acc_ref[hp][...]
                    refs.extend(jnp.where(picked[j], m_new, -NEG_BIG) for j in range(KV_GROUP))
                m_ref[hp][...] = m_new
            c0 = jl * blk if g is None else 0
            p = jnp.exp2(buf[jl * blk:(jl + 1) * blk, c0:] - refs[jl][:, c0:]).astype(BF16)
            pv = jnp.dot(vt_ref[0, vsl(hp), rows], p, preferred_element_type=F32)
            if g is None and jl == 0:
                acc_ref[hp][...] = pv
            else:
                acc_ref[hp][:, c0:] = acc_ref[hp][:, c0:] + pv

        return [functools.partial(step, jl, rows) for jl, rows in enumerate(item_blocks(g))]

    def interleave(w_steps, s_steps):
        for n in range(max(len(w_steps) + SCORE_LEAD, len(s_steps))):
            if n < len(s_steps):
                s_steps[n]()
            if SCORE_LEAD <= n < len(w_steps) + SCORE_LEAD:
                w_steps[n - SCORE_LEAD]()

    qts = [qt_ref[0, hsl(hp), :] for hp in heads]
    for hp in heads:
        select_blocks(hp, qts[hp])
    interleave([], score_steps(0, qts[0], None))
    for hp in heads:
        nhp = (hp + 1) % ATTN_HEADS
        interleave(weight_steps(hp, None), score_steps(nhp, qts[nhp], None if nhp else 0))

    ngroups = (fb + KV_GROUP - 1) // KV_GROUP

    def past_group(g, _, last=False):
        for hp in heads:
            nhp = (hp + 1) % ATTN_HEADS
            ahead = [] if (last and not nhp) else score_steps(nhp, qts[nhp], g if nhp else g + 1)
            interleave(weight_steps(hp, g), ahead)
        return 0

    lax.fori_loop(0, ngroups - 1, past_group, 0)

    @pl.when(ngroups > 0)
    def _():
        past_group(ngroups - 1, 0, last=True)

    for hp in heads:
        ot = acc_ref[hp][:HEAD_DIM, :] / acc_ref[hp][HEAD_DIM:HEAD_DIM + 1, :]
        o_ref[0, :, hsl(hp)] = ot.T.astype(o_ref.dtype)


def _moba_attention(qt, k, vt):
    b, l, d = k.shape
    nb = l // MOBA_BLOCK
    assert nb % KV_GROUP == 0 and nb % Q_BLOCKS == 0
    hw = ATTN_HEADS * HEAD_DIM
    qw = Q_BLOCKS * MOBA_BLOCK
    return pl.pallas_call(
        _moba_kernel,
        grid=(b, N_HEADS // ATTN_HEADS, nb // Q_BLOCKS),
        in_specs=[pl.BlockSpec((1, hw, qw), lambda bi, h, i: (bi, h, i)),
                  pl.BlockSpec((1, l, hw), lambda bi, h, i: (bi, 0, h)),
                  pl.BlockSpec((1, ATTN_HEADS * V_ROWS, l), lambda bi, h, i: (bi, h, 0))],
        out_specs=pl.BlockSpec((1, qw, hw), lambda bi, h, i: (bi, i, h)),
        out_shape=jax.ShapeDtypeStruct((b, l, d), BF16),
        scratch_shapes=[pltpu.VMEM(shape, dtype)
                        for shape, dtype in (((nb, HEAD_DIM), F32), ((nb, qw), F32), ((1, qw), F32),
                                             ((V_ROWS, qw), F32), ((KV_GROUP * MOBA_BLOCK, qw), F32),
                                             ((KV_GROUP, qw), F32), ((3 * nb, HEAD_DIM), BF16))
                        for _ in range(ATTN_HEADS)],
        compiler_params=_cparams("parallel", "parallel", "arbitrary"),
        name="moba_attn",
    )(qt, k, vt)


def _merge_mlp_kernel(x_ref, oa_ref, sga_ref, sgb_ref, y_ref, wglu_ref, wout_ref, g1_ref, wup_ref, wdn_ref, g2_ref,
                      o_ref, *, final_norm):
    d = x_ref.shape[-1]
    ff = wup_ref.shape[1]
    y = jnp.concatenate([y_ref[s, 0] for s in range(y_ref.shape[0])], axis=1)
    hgl = jnp.dot(y.astype(BF16), wglu_ref[...], preferred_element_type=F32)
    ob = hgl[:, :d] * jax.nn.sigmoid(hgl[:, d:])
    mixed = sga_ref[0].astype(F32) * oa_ref[0].astype(F32) + sgb_ref[0].astype(F32) * ob
    x = x_ref[0] + jnp.dot(mixed.astype(BF16), wout_ref[...], preferred_element_type=F32)
    ms = jnp.mean(x * x, axis=-1, keepdims=True)
    h = (x * lax.rsqrt(ms + RMS_EPS) * g1_ref[...]).astype(BF16)
    acc = x
    for c0 in range(0, ff, d):
        a = jnp.maximum(jnp.dot(h, wup_ref[:, c0:c0 + d], preferred_element_type=F32), 0.0)
        acc = acc + jnp.dot((a * a).astype(BF16), wdn_ref[c0:c0 + d, :], preferred_element_type=F32)
    if final_norm:
        ms2 = jnp.mean(acc * acc, axis=-1, keepdims=True)
        acc = acc * lax.rsqrt(ms2 + RMS_EPS) * g2_ref[...]
    o_ref[0] = acc


def _merge_mlp(x, oa, sga, sgb, y, w_glu, w_out, g1, w_up, w_down, g2, final_norm):
    b, l, d = x.shape
    tm = TOKEN_TILE
    ns, lane = y.shape[0], y.shape[-1]
    sw = ns * lane
    ff = w_up.shape[1]
    tok = lambda w: pl.BlockSpec((1, tm, w), lambda bi, i: (bi, i, 0))
    return pl.pallas_call(
        functools.partial(_merge_mlp_kernel, final_norm=final_norm),
        grid=(b, l // tm),
        in_specs=[tok(d), tok(d), tok(d), tok(d), pl.BlockSpec((ns, 1, tm, lane), lambda bi, i: (0, bi, i, 0)),
                  _resident_spec((sw, 2 * d)), _resident_spec((d, d)), _const_spec((1, d)),
                  _resident_spec((d, ff)), _resident_spec((ff, d)), _const_spec((1, d))],
        out_specs=tok(d),
        out_shape=jax.ShapeDtypeStruct((b, l, d), F32),
        compiler_params=_cparams("parallel", "parallel"),
        name="merge_mlp",
    )(x, oa, sga, sgb, y, w_glu.astype(BF16), w_out.astype(BF16), g1.reshape(1, d),
      w_up.astype(BF16), w_down.astype(BF16), g2.reshape(1, d))


def kernel(x, norm_mix_g, w_in, lam_re, lam_im, log_step, b_re, b_im, c_re, c_im, d_skip, w_glu, w_out,
           norm_mlp_g, w_up, w_down, norm_final_g):
    b, l, d = x.shape
    depth = w_in.shape[0]
    assert d == N_HEADS * HEAD_DIM and l % TOKEN_TILE == 0 and l % IN_PROJ_TILE == 0
    assert l % (SSM_CHUNK * SCAN_ROWS) == 0
    for i in range(depth):
        slab_mats = _ssm_prep(lam_re[i], lam_im[i], log_step[i], b_re[i], b_im[i], c_re[i], c_im[i], d_skip[i])
        qt, k, vt, u, sga, sgb = _in_proj(x, norm_mix_g[i], w_in[i])
        y = _ssm_chunk(u, *slab_mats)
        oa = _moba_attention(qt, k, vt)
        x = _merge_mlp(x, oa, sga, sgb, y, w_glu[i], w_out[i], norm_mlp_g[i], w_up[i], w_down[i], norm_final_g,
                       final_norm=(i == depth - 1))
    return x
```

```python
import functools
import math

import jax
import jax.numpy as jnp
from jax import lax
from jax.experimental import pallas as pl
from jax.experimental.pallas import tpu as pltpu

F32 = jnp.float32
BF16 = jnp.bfloat16

N_HEADS = 8
HEAD_DIM = 128
HALF_DIM = HEAD_DIM // 2
MOBA_BLOCK = 256
MOBA_TOPK = 3
ATTN_HEADS = 2
Q_BLOCKS = 4
KV_GROUP = 4
SCORE_LEAD = 1
BF16_SUBLANES = 16
V_ROWS = HEAD_DIM + BF16_SUBLANES
ROPE_THETA = 10000.0
SSM_GROUP = 16
SSM_STATE = 64
SSM_CHUNK = 8
CHUNK_W = SSM_CHUNK * SSM_GROUP
SCAN_ROWS = 8
LANE = 128
GROUPS_PER_SLAB = LANE // SSM_GROUP
SLAB_W = SSM_CHUNK * LANE
RMS_EPS = 1e-6
NEG_BIG = -1e30
TOKEN_TILE = 512
IN_PROJ_TILE = 1024
VMEM_LIMIT = 56 * 1024 * 1024

_NT = (((1,), (1,)), ((), ()))


def _cparams(*sem):
    return pltpu.CompilerParams(dimension_semantics=sem, vmem_limit_bytes=VMEM_LIMIT)


def _const_spec(shape):
    nd = len(shape)
    return pl.BlockSpec(shape, lambda *_: (0,) * nd)


def _resident_spec(shape):
    nd = len(shape)
    return pl.BlockSpec(shape, lambda *_: (0,) * nd, pipeline_mode=pl.Buffered(1))


def _cexp(lr, li, step, e):
    mag = jnp.exp(lr * step * e)
    ang = li * step * e
    return mag * jnp.cos(ang), mag * jnp.sin(ang)


def _spread(rows, cols, target_fn):
    r = lax.broadcasted_iota(jnp.int32, (rows, cols), 0)
    c = lax.broadcasted_iota(jnp.int32, (rows, cols), 1)
    return jnp.where(c == target_fn(r), 1.0, 0.0).astype(F32)


def _ssm_prep_kernel(lrr_ref, lir_ref, ls_ref, btr_ref, bti_ref, ctr_ref, cti_ref, dt_ref,
                     mx_ref, gx_ref, hx_ref, apx_ref, aix_ref, dx_ref):
    n = lrr_ref.shape[-1]
    gl = lax.rem(pl.program_id(0), GROUPS_PER_SLAB)
    hp = lax.Precision.HIGHEST
    place = functools.partial(jnp.dot, precision=hp, preferred_element_type=F32)
    place16 = lambda a, spread: jnp.dot(a.astype(BF16), spread.astype(BF16), preferred_element_type=F32)
    to_slab = _spread(CHUNK_W, SLAB_W, lambda r: lax.shift_right_logical(r, 4) * LANE + gl * SSM_GROUP + (r & 15))
    to_state = _spread(n, GROUPS_PER_SLAB * n, lambda r: gl * n + r)
    to_state2 = _spread(2 * n, 2 * GROUPS_PER_SLAB * n,
                        lambda r: jnp.where(r >= n, GROUPS_PER_SLAB * n - n, 0) + gl * n + r)
    step = jnp.exp(ls_ref[0])
    lr_r, li_r = lrr_ref[0], lir_ref[0]
    prow = lax.broadcasted_iota(jnp.int32, (SSM_CHUNK + SCAN_ROWS, 1), 0)
    pw_r, pw_i = _cexp(lr_r, li_r, step,
                       jnp.where(prow <= SSM_CHUNK, prow, (prow - SSM_CHUNK + 1) * SSM_CHUNK).astype(F32))
    tau = lax.shift_right_logical(lax.broadcasted_iota(jnp.int32, pw_r.shape[:1] + (CHUNK_W,), 1), 4)
    pick = lambda shift: jnp.where(tau + shift == lax.broadcasted_iota(jnp.int32, tau.shape, 0), 1.0, 0.0)
    on_sublanes = lambda p, shift: lax.dot_general(p, pick(shift), (((0,), (0,)), ((), ())), precision=hp,
                                                   preferred_element_type=F32)
    ar, ai = pw_r[1:2], pw_i[1:2]
    den = lr_r * lr_r + li_r * li_r
    nr, ni = ar - 1.0, ai
    cr = (nr * lr_r + ni * li_r) / den
    ci = (ni * lr_r - nr * li_r) / den
    btr, bti = btr_ref[0], bti_ref[0]
    bbr = cr * btr - ci * bti
    bbi = cr * bti + ci * btr
    rows_of = lambda p: jnp.concatenate(
        [jnp.broadcast_to(p[SSM_CHUNK - 1 - s:SSM_CHUNK - s], (SSM_GROUP, n)) for s in range(SSM_CHUNK)], axis=0)
    pr, pi_ = rows_of(pw_r), rows_of(pw_i)
    gcat = jnp.concatenate([pr * bbr - pi_ * bbi, pr * bbi + pi_ * bbr], axis=1)
    gwide = place16(gcat, to_state2).astype(gx_ref.dtype)
    row0 = pl.multiple_of(gl * SSM_GROUP, SSM_GROUP)
    for s in range(SSM_CHUNK):
        gx_ref[0, pl.ds(s * LANE + row0, SSM_GROUP), :] = gwide[s * SSM_GROUP:(s + 1) * SSM_GROUP]
    ctr, cti = ctr_ref[0], cti_ref[0]
    wr, wi = on_sublanes(pw_r, 0), on_sublanes(pw_i, 0)
    p_r = ctr * wr - cti * wi
    p_i = ctr * wi + cti * wr
    r0 = place(bbr[:SSM_GROUP], p_r) - place(bbi[:SSM_GROUP], p_i)
    r0w = place16(r0, to_slab)
    lane = lax.broadcasted_iota(jnp.int32, (SSM_GROUP, SLAB_W), 1)
    for s in range(SSM_CHUNK):
        blk = r0w if s == 0 else pltpu.roll(r0w, LANE * s, axis=1)
        blk = jnp.where(lane >= LANE * s, blk, 0.0)
        mx_ref[0, pl.ds(s * LANE + row0, SSM_GROUP), :] = blk.astype(mx_ref.dtype)
    hr, hi = on_sublanes(pw_r, 1), on_sublanes(pw_i, 1)
    st0 = pl.multiple_of(gl * n, n)
    hx_ref[0, pl.ds(st0, n), :] = place16(ctr * hr - cti * hi, to_slab).astype(hx_ref.dtype)
    hx_ref[0, pl.ds(GROUPS_PER_SLAB * n + st0, n), :] = place16(-(ctr * hi + cti * hr), to_slab).astype(hx_ref.dtype)
    qr, qi = pw_r[SSM_CHUNK:], pw_i[SSM_CHUNK:]
    shared = ((apx_ref, place(qr, to_state)), (aix_ref, place(qi, to_state)), (dx_ref, place(dt_ref[0], to_slab)))

    @pl.when(gl == 0)
    def _():
        for ref, val in shared:
            ref[0] = val

    @pl.when(gl != 0)
    def _():
        for ref, val in shared:
            ref[0] = ref[0] + val


def _ssm_prep(lam_re, lam_im, log_step, b_re, b_im, c_re, c_im, d_skip):
    g, n = lam_re.shape
    ns = g // GROUPS_PER_SLAB
    sl = GROUPS_PER_SLAB * n
    bt = lambda b: jnp.tile(jnp.swapaxes(b, 1, 2), (1, SSM_CHUNK, 1))
    ct = lambda m: jnp.tile(jnp.swapaxes(m, 1, 2), (1, 1, SSM_CHUNK))
    args = (lam_re.reshape(g, 1, n), lam_im.reshape(g, 1, n), log_step.reshape(g, 1, 1),
            bt(b_re), bt(b_im), ct(c_re), ct(c_im),
            jnp.tile(d_skip.astype(F32), (1, SSM_CHUNK)).reshape(g, 1, CHUNK_W))
    spec = lambda shp: pl.BlockSpec((1,) + shp, lambda i: (i, 0, 0))
    slab = lambda shp: pl.BlockSpec((1,) + shp, lambda i: (i // GROUPS_PER_SLAB, 0, 0))
    in_specs = [spec((1, n)), spec((1, n)), spec((1, 1)),
                spec((CHUNK_W, n)), spec((CHUNK_W, n)), spec((n, CHUNK_W)), spec((n, CHUNK_W)), spec((1, CHUNK_W))]
    out_shapes = [((SLAB_W, SLAB_W), BF16), ((SLAB_W, 2 * sl), BF16), ((2 * sl, SLAB_W), BF16),
                  ((SCAN_ROWS, sl), F32), ((SCAN_ROWS, sl), F32), ((1, SLAB_W), F32)]
    return pl.pallas_call(
        _ssm_prep_kernel,
        grid=(g,),
        in_specs=in_specs,
        out_specs=[slab(s) for s, _ in out_shapes],
        out_shape=[jax.ShapeDtypeStruct((ns,) + s, d) for s, d in out_shapes],
        compiler_params=_cparams("arbitrary"),
        name="ssm_prep",
    )(*args)


def _in_proj_kernel(x_ref, g_ref, wqt_ref, wk_ref, wvt_ref, wu_ref, wg_ref,
                    cos_ref, sin_ref, cost_ref, sint_ref,
                    qt_ref, k_ref, vt_ref, u_ref, sga_ref, sgb_ref):
    x = x_ref[0]
    d = x.shape[-1]
    ms = jnp.mean(x * x, axis=-1, keepdims=True)
    h = (x * lax.rsqrt(ms + RMS_EPS) * g_ref[...]).astype(BF16)
    for c0 in range(0, 2 * d, 512):
        sg = jax.nn.sigmoid(jnp.dot(h, wg_ref[:, c0:c0 + 512], preferred_element_type=F32)).astype(BF16)
        if c0 < d:
            sga_ref[0, :, c0:c0 + 512] = sg
        else:
            sgb_ref[0, :, c0 - d:c0 - d + 512] = sg
    cos, sin = cos_ref[...], sin_ref[...]
    for c0 in range(0, d, 512):
        p = jnp.dot(h, wk_ref[:, c0:c0 + 512], preferred_element_type=F32)
        for hh in range(512 // HEAD_DIM):
            sl = p[:, hh * HEAD_DIM:(hh + 1) * HEAD_DIM]
            r = sl * cos + pltpu.roll(sl, HALF_DIM, axis=1) * sin
            k_ref[0, :, c0 + hh * HEAD_DIM:c0 + (hh + 1) * HEAD_DIM] = r.astype(BF16)
    cost, sint = cost_ref[...], sint_ref[...]
    scale = HEAD_DIM ** -0.5 * math.log2(math.e)
    for c0 in range(0, d, 256):
        pt = lax.dot_general(wqt_ref[c0:c0 + 256, :], h, _NT, preferred_element_type=F32)
        for hh in range(256 // HEAD_DIM):
            r0 = hh * HEAD_DIM
            x1 = pt[r0:r0 + HALF_DIM]
            x2 = pt[r0 + HALF_DIM:r0 + HEAD_DIM]
            qt_ref[0, c0 + r0:c0 + r0 + HALF_DIM, :] = ((x1 * cost - x2 * sint) * scale).astype(BF16)
            qt_ref[0, c0 + r0 + HALF_DIM:c0 + r0 + HEAD_DIM, :] = ((x2 * cost + x1 * sint) * scale).astype(BF16)
    ones = jnp.ones((BF16_SUBLANES, x.shape[0]), BF16)
    for c0 in range(0, d, 256):
        pv = lax.dot_general(wvt_ref[c0:c0 + 256, :], h, _NT, preferred_element_type=F32).astype(BF16)
        for hh in range(256 // HEAD_DIM):
            r0 = (c0 // HEAD_DIM + hh) * V_ROWS
            vt_ref[0, r0:r0 + HEAD_DIM, :] = pv[hh * HEAD_DIM:(hh + 1) * HEAD_DIM]
            vt_ref[0, r0 + HEAD_DIM:r0 + V_ROWS, :] = ones
    pu = jnp.dot(h, wu_ref[...], preferred_element_type=F32)
    for s in range(pu.shape[1] // LANE):
        u_ref[s, 0] = pu[:, s * LANE:(s + 1) * LANE]


def _in_proj(x, g, w_in):
    b, l, d = x.shape
    tm = IN_PROJ_TILE
    sw = w_in.shape[1] - 5 * d
    wqt = w_in[:, :d].T.astype(BF16)
    wk = w_in[:, d:2 * d].astype(BF16)
    wvt = w_in[:, 2 * d:3 * d].T.astype(BF16)
    wu = w_in[:, 3 * d:3 * d + sw].astype(BF16)
    wg = w_in[:, 3 * d + sw:].astype(BF16)
    inv_freq = ROPE_THETA ** (-jnp.arange(HALF_DIM, dtype=F32) / HALF_DIM)
    ang = jnp.arange(l).astype(F32)[:, None] * inv_freq[None, :]
    cos, sin = jnp.cos(ang), jnp.sin(ang)
    cos2 = jnp.concatenate([cos, cos], axis=1)
    sin2 = jnp.concatenate([-sin, sin], axis=1)
    tok = lambda w: pl.BlockSpec((1, tm, w), lambda bi, i: (bi, i, 0))
    feat = lambda w: pl.BlockSpec((1, w, tm), lambda bi, i: (bi, 0, i))
    sds = jax.ShapeDtypeStruct
    return pl.pallas_call(
        _in_proj_kernel,
        grid=(b, l // tm),
        in_specs=[tok(d), _const_spec((1, d)), _resident_spec((d, d)), _resident_spec((d, d)), _resident_spec((d, d)),
                  _resident_spec((d, sw)), _resident_spec((d, 2 * d)),
                  pl.BlockSpec((tm, HEAD_DIM), lambda bi, i: (i, 0)),
                  pl.BlockSpec((tm, HEAD_DIM), lambda bi, i: (i, 0)),
                  pl.BlockSpec((HALF_DIM, tm), lambda bi, i: (0, i)),
                  pl.BlockSpec((HALF_DIM, tm), lambda bi, i: (0, i))],
        out_specs=[feat(d), tok(d), feat(N_HEADS * V_ROWS),
                   pl.BlockSpec((sw // LANE, 1, tm, LANE), lambda bi, i: (0, bi, i, 0)), tok(d), tok(d)],
        out_shape=[sds((b, d, l), BF16), sds((b, l, d), BF16), sds((b, N_HEADS * V_ROWS, l), BF16),
                   sds((sw // LANE, b, l, LANE), F32), sds((b, l, d), BF16), sds((b, l, d), BF16)],
        compiler_params=_cparams("parallel", "parallel"),
        name="in_proj",
    )(x, g.reshape(1, d), wqt, wk, wvt, wu, wg, cos2, sin2, cos.T, sin.T)


def _ssm_chunk_kernel(u_ref, mx_ref, gx_ref, hx_ref, apr_ref, api_ref, d_ref,
                      y_ref, ere_ref, eim_ref, xre_ref, xim_ref):
    nk = u_ref.shape[2] // SSM_CHUNK
    sl = ere_ref.shape[1]
    uf = jnp.concatenate([u_ref[0, 0, pl.ds(t, nk, stride=SSM_CHUNK), :] for t in range(SSM_CHUNK)], axis=1)
    u = uf.astype(BF16)
    dot = functools.partial(jnp.dot, preferred_element_type=F32)
    e = dot(u, gx_ref[0])
    ere_ref[...] = e[:, :sl]
    eim_ref[...] = e[:, sl:]
    apr, api = apr_ref[0], api_ref[0]
    row = lax.broadcasted_iota(jnp.int32, apr.shape, 0)
    lvl = [(dd, jnp.where(row >= dd, jnp.broadcast_to(apr[dd - 1:dd], apr.shape), 0.0),
            jnp.where(row >= dd, jnp.broadcast_to(api[dd - 1:dd], apr.shape), 0.0)) for dd in (1, 2, 4)]

    def tile(t, carry):
        cr, ci = carry
        r0 = pl.multiple_of(t * SCAN_ROWS, SCAN_ROWS)
        zr = ere_ref[pl.ds(r0, SCAN_ROWS), :]
        zi = eim_ref[pl.ds(r0, SCAN_ROWS), :]
        for dd, adr, adi in lvl:
            sr = pltpu.roll(zr, dd, axis=0)
            si = pltpu.roll(zi, dd, axis=0)
            zr, zi = zr + (adr * sr - adi * si), zi + (adr * si + adi * sr)
        sr = apr * cr - api * ci + zr
        si = apr * ci + api * cr + zi
        xre_ref[pl.ds(r0, SCAN_ROWS), :] = jnp.where(row >= 1, pltpu.roll(sr, 1, axis=0), cr)
        xim_ref[pl.ds(r0, SCAN_ROWS), :] = jnp.where(row >= 1, pltpu.roll(si, 1, axis=0), ci)
        last = SCAN_ROWS - 1
        return (jnp.broadcast_to(sr[last:last + 1], sr.shape), jnp.broadcast_to(si[last:last + 1], si.shape))

    zero = jnp.zeros(apr.shape, F32)
    lax.fori_loop(0, nk // SCAN_ROWS, tile, (zero, zero))
    x0 = jnp.concatenate([xre_ref[...], xim_ref[...]], axis=1).astype(BF16)
    half = SLAB_W // 2
    intra = jnp.concatenate([dot(u[:, :half], mx_ref[0, :half, :half]), dot(u, mx_ref[0, :, half:])], axis=1)
    y = jax.nn.gelu(intra + dot(x0, hx_ref[0]) + d_ref[0] * uf)
    for t in range(SSM_CHUNK):
        y_ref[0, 0, pl.ds(t, nk, stride=SSM_CHUNK), :] = y[:, t * LANE:(t + 1) * LANE]


def _ssm_chunk(u4, mx, gx, hx, apx, aix, dx):
    ns, b, l, lane = u4.shape
    nk = l // SSM_CHUNK
    sl = apx.shape[-1]
    per_slab = lambda r, c: pl.BlockSpec((1, r, c), lambda s, bi: (s, 0, 0))
    rows = pl.BlockSpec((1, 1, l, lane), lambda s, bi: (s, bi, 0, 0))
    return pl.pallas_call(
        _ssm_chunk_kernel,
        grid=(ns, b),
        in_specs=[rows, per_slab(SLAB_W, SLAB_W), per_slab(SLAB_W, 2 * sl), per_slab(2 * sl, SLAB_W),
                  per_slab(SCAN_ROWS, sl), per_slab(SCAN_ROWS, sl), per_slab(1, SLAB_W)],
        out_specs=rows,
        out_shape=jax.ShapeDtypeStruct(u4.shape, F32),
        scratch_shapes=[pltpu.VMEM((nk, sl), F32) for _ in range(4)],
        compiler_params=_cparams("parallel", "parallel"),
        name="ssm_chunk",
    )(u4, mx, gx, hx, apx, aix, dx)


def _moba_kernel(qt_ref, k_ref, vt_ref, o_ref, *scratch):
    i = pl.program_id(2)
    blk = MOBA_BLOCK
    qw = Q_BLOCKS * blk
    nb = k_ref.shape[1] // blk
    fb = i * Q_BLOCKS
    heads = range(ATTN_HEADS)
    hsl = lambda hp: slice(hp * HEAD_DIM, (hp + 1) * HEAD_DIM)
    kmean_ref, sel_ref, m_ref, acc_ref, s_ref, cm_ref, ksplit_ref = (
        scratch[r * ATTN_HEADS:(r + 1) * ATTN_HEADS] for r in range(7))

    @pl.when(i == 0)
    def _():
        for hp in heads:
            for j in range(nb):
                kb = k_ref[0, j * blk:(j + 1) * blk, hsl(hp)].astype(F32)
                kmean_ref[hp][j:j + 1, :] = jnp.mean(kb, axis=0, keepdims=True)
            rest = kmean_ref[hp][...]
            for part in range(3):
                piece = rest.astype(BF16)
                ksplit_ref[hp][part * nb:(part + 1) * nb, :] = piece
                rest = rest - piece.astype(F32)

    vsl = lambda hp: slice(hp * V_ROWS, (hp + 1) * V_ROWS)

    def select_blocks(hp, qt):
        g3 = jnp.dot(ksplit_ref[hp][...], qt, preferred_element_type=F32)
        gate = g3[:nb] + g3[nb:2 * nb] + g3[2 * nb:]
        brow = lax.broadcasted_iota(jnp.int32, gate.shape, 0)
        own = fb + lax.broadcasted_iota(jnp.int32, (1, qw), 1) // blk
        work = jnp.where(brow < own, gate, -jnp.inf)
        sel = jnp.zeros(gate.shape, F32)
        for _ in range(MOBA_TOPK):
            mx = jnp.max(work, axis=0, keepdims=True)
            idx = jnp.min(jnp.where(work == mx, brow, nb), axis=0, keepdims=True)
            hit = brow == idx
            sel = jnp.where(hit, jnp.where(idx < own, 1.0, 0.0), sel)
            work = jnp.where(hit, -jnp.inf, work)
        sel_ref[hp][...] = sel

    def item_blocks(g):
        first, count = (fb, Q_BLOCKS) if g is None else (g * KV_GROUP, KV_GROUP)
        if isinstance(first, int):
            return [slice((first + jl) * blk, (first + jl + 1) * blk) for jl in range(count)]
        return [pl.ds(pl.multiple_of((first + jl) * blk, blk), blk) for jl in range(count)]

    def score_steps(hp, qt, g):
        buf = s_ref[hp]

        def step(jl, rows):
            if g is None:
                c0 = jl * blk
                s = jnp.dot(k_ref[0, rows, hsl(hp)], qt[:, c0:], preferred_element_type=F32)
                krow = lax.broadcasted_iota(jnp.int32, s.shape, 0)
                qcol = lax.broadcasted_iota(jnp.int32, s.shape, 1)
                picked = sel_ref[hp][pl.ds(fb + jl, 1), c0:] > 0.0
                s = jnp.where(qcol >= blk, jnp.where(picked, s, NEG_BIG), jnp.where(krow <= qcol, s, NEG_BIG))
                buf[jl * blk:(jl + 1) * blk, c0:] = s
                cmax = jnp.max(s, axis=0, keepdims=True)
                if c0:
                    cmax = jnp.concatenate([jnp.full((1, c0), NEG_BIG, F32), cmax], axis=1)
                cm_ref[hp][jl:jl + 1, :] = cmax
                return
            s = jnp.dot(k_ref[0, rows, hsl(hp)], qt, preferred_element_type=F32)
            buf[jl * blk:(jl + 1) * blk, :] = s
            cm_ref[hp][jl:jl + 1, :] = jnp.max(s, axis=0, keepdims=True)

        return [functools.partial(step, jl, rows) for jl, rows in enumerate(item_blocks(g))]

    def weight_steps(hp, g):
        buf = s_ref[hp]
        refs = []

        def step(jl, rows):
            if jl == 0:
                cmax = [cm_ref[hp][j:j + 1, :] for j in range(Q_BLOCKS if g is None else KV_GROUP)]
                if g is None:
                    m_new = functools.reduce(jnp.maximum, cmax)
                    refs.extend(m_new for _ in cmax)
                else:
                    picked = [jnp.logical_and(sel_ref[hp][pl.ds(g * KV_GROUP + j, 1), :] > 0.0,
                                              g * KV_GROUP + j < fb) for j in range(KV_GROUP)]
                    m_old = m_ref[hp][...]
                    m_new = m_old
                    for j in range(KV_GROUP):
                        m_new = jnp.maximum(m_new, jnp.where(picked[j], cmax[j], NEG_BIG))
                    acc_ref[hp][...] = jnp.exp2(m_old - m_new) * acc_ref[hp][...]
                    refs.extend(jnp.where(picked[j], m_new, -NEG_BIG) for j in range(KV_GROUP))
                m_ref[hp][...] = m_new
            c0 = jl * blk if g is None else 0
            p = jnp.exp2(buf[jl * blk:(jl + 1) * blk, c0:] - refs[jl][:, c0:]).astype(BF16)
            pv = jnp.dot(vt_ref[0, vsl(hp), rows], p, preferred_element_type=F32)
            if g is None and jl == 0:
                acc_ref[hp][...] = pv
            else:
                acc_ref[hp][:, c0:] = acc_ref[hp][:, c0:] + pv

        return [functools.partial(step, jl, rows) for jl, rows in enumerate(item_blocks(g))]

    def interleave(w_steps, s_steps):
        for n in range(max(len(w_steps) + SCORE_LEAD, len(s_steps))):
            if n < len(s_steps):
                s_steps[n]()
            if SCORE_LEAD <= n < len(w_steps) + SCORE_LEAD:
                w_steps[n - SCORE_LEAD]()

    qts = [qt_ref[0, hsl(hp), :] for hp in heads]
    for hp in heads:
        select_blocks(hp, qts[hp])
    interleave([], score_steps(0, qts[0], None))
    for hp in heads:
        nhp = (hp + 1) % ATTN_HEADS
        interleave(weight_steps(hp, None), score_steps(nhp, qts[nhp], None if nhp else 0))

    ngroups = (fb + KV_GROUP - 1) // KV_GROUP

    def past_group(g, _, last=False):
        for hp in heads:
            nhp = (hp + 1) % ATTN_HEADS
            ahead = [] if (last and not nhp) else score_steps(nhp, qts[nhp], g if nhp else g + 1)
            interleave(weight_steps(hp, g), ahead)
        return 0

    n_ahead = jnp.maximum(ngroups - 1, 0)

    def two_groups(gg, _):
        past_group(2 * gg, 0)
        past_group(2 * gg + 1, 0)
        return 0

    lax.fori_loop(0, n_ahead // 2, two_groups, 0)

    @pl.when(n_ahead % 2 == 1)
    def _():
        past_group(n_ahead - 1, 0)

    @pl.when(ngroups > 0)
    def _():
        past_group(ngroups - 1, 0, last=True)

    for hp in heads:
        ot = acc_ref[hp][:HEAD_DIM, :] / acc_ref[hp][HEAD_DIM:HEAD_DIM + 1, :]
        o_ref[0, :, hsl(hp)] = ot.T.astype(o_ref.dtype)


def _moba_attention(qt, k, vt):
    b, l, d = k.shape
    nb = l // MOBA_BLOCK
    assert nb % KV_GROUP == 0 and nb % Q_BLOCKS == 0
    hw = ATTN_HEADS * HEAD_DIM
    qw = Q_BLOCKS * MOBA_BLOCK
    return pl.pallas_call(
        _moba_kernel,
        grid=(b, N_HEADS // ATTN_HEADS, nb // Q_BLOCKS),
        in_specs=[pl.BlockSpec((1, hw, qw), lambda bi, h, i: (bi, h, i)),
                  pl.BlockSpec((1, l, hw), lambda bi, h, i: (bi, 0, h)),
                  pl.BlockSpec((1, ATTN_HEADS * V_ROWS, l), lambda bi, h, i: (bi, h, 0))],
        out_specs=pl.BlockSpec((1, qw, hw), lambda bi, h, i: (bi, i, h)),
        out_shape=jax.ShapeDtypeStruct((b, l, d), BF16),
        scratch_shapes=[pltpu.VMEM(shape, dtype)
                        for shape, dtype in (((nb, HEAD_DIM), F32), ((nb, qw), F32), ((1, qw), F32),
                                             ((V_ROWS, qw), F32), ((KV_GROUP * MOBA_BLOCK, qw), F32),
                                             ((KV_GROUP, qw), F32), ((3 * nb, HEAD_DIM), BF16))
                        for _ in range(ATTN_HEADS)],
        compiler_params=_cparams("parallel", "parallel", "arbitrary"),
        name="moba_attn",
    )(qt, k, vt)


def _merge_mlp_kernel(x_ref, oa_ref, sga_ref, sgb_ref, y_ref, wglu_ref, wout_ref, g1_ref, wup_ref, wdn_ref, g2_ref,
                      o_ref, *, final_norm):
    d = x_ref.shape[-1]
    ff = wup_ref.shape[1]
    y = jnp.concatenate([y_ref[s, 0] for s in range(y_ref.shape[0])], axis=1)
    hgl = jnp.dot(y.astype(BF16), wglu_ref[...], preferred_element_type=F32)
    ob = hgl[:, :d] * jax.nn.sigmoid(hgl[:, d:])
    mixed = sga_ref[0].astype(F32) * oa_ref[0].astype(F32) + sgb_ref[0].astype(F32) * ob
    x = x_ref[0] + jnp.dot(mixed.astype(BF16), wout_ref[...], preferred_element_type=F32)
    ms = jnp.mean(x * x, axis=-1, keepdims=True)
    h = (x * lax.rsqrt(ms + RMS_EPS) * g1_ref[...]).astype(BF16)
    acc = x
    for c0 in range(0, ff, d):
        a = jnp.maximum(jnp.dot(h, wup_ref[:, c0:c0 + d], preferred_element_type=F32), 0.0)
        acc = acc + jnp.dot((a * a).astype(BF16), wdn_ref[c0:c0 + d, :], preferred_element_type=F32)
    if final_norm:
        ms2 = jnp.mean(acc * acc, axis=-1, keepdims=True)
        acc = acc * lax.rsqrt(ms2 + RMS_EPS) * g2_ref[...]
    o_ref[0] = acc


def _merge_mlp(x, oa, sga, sgb, y, w_glu, w_out, g1, w_up, w_down, g2, final_norm):
    b, l, d = x.shape
    tm = TOKEN_TILE
    ns, lane = y.shape[0], y.shape[-1]
    sw = ns * lane
    ff = w_up.shape[1]
    tok = lambda w: pl.BlockSpec((1, tm, w), lambda bi, i: (bi, i, 0))
    return pl.pallas_call(
        functools.partial(_merge_mlp_kernel, final_norm=final_norm),
        grid=(b, l // tm),
        in_specs=[tok(d), tok(d), tok(d), tok(d), pl.BlockSpec((ns, 1, tm, lane), lambda bi, i: (0, bi, i, 0)),
                  _resident_spec((sw, 2 * d)), _resident_spec((d, d)), _const_spec((1, d)),
                  _resident_spec((d, ff)), _resident_spec((ff, d)), _const_spec((1, d))],
        out_specs=tok(d),
        out_shape=jax.ShapeDtypeStruct((b, l, d), F32),
        compiler_params=_cparams("parallel", "parallel"),
        name="merge_mlp",
    )(x, oa, sga, sgb, y, w_glu.astype(BF16), w_out.astype(BF16), g1.reshape(1, d),
      w_up.astype(BF16), w_down.astype(BF16), g2.reshape(1, d))


def kernel(x, norm_mix_g, w_in, lam_re, lam_im, log_step, b_re, b_im, c_re, c_im, d_skip, w_glu, w_out,
           norm_mlp_g, w_up, w_down, norm_final_g):
    b, l, d = x.shape
    depth = w_in.shape[0]
    assert d == N_HEADS * HEAD_DIM and l % TOKEN_TILE == 0 and l % IN_PROJ_TILE == 0
    assert l % (SSM_CHUNK * SCAN_ROWS) == 0
    for i in range(depth):
        slab_mats = _ssm_prep(lam_re[i], lam_im[i], log_step[i], b_re[i], b_im[i], c_re[i], c_im[i], d_skip[i])
        qt, k, vt, u, sga, sgb = _in_proj(x, norm_mix_g[i], w_in[i])
        y = _ssm_chunk(u, *slab_mats)
        oa = _moba_attention(qt, k, vt)
        x = _merge_mlp(x, oa, sga, sgb, y, w_glu[i], w_out[i], norm_mlp_g[i], w_up[i], w_down[i], norm_final_g,
                       final_norm=(i == depth - 1))
    return x
```

```python
import functools
import math

import jax
import jax.numpy as jnp
from jax import lax
from jax.experimental import pallas as pl
from jax.experimental.pallas import tpu as pltpu

F32 = jnp.float32
BF16 = jnp.bfloat16

N_HEADS = 8
HEAD_DIM = 128
HALF_DIM = HEAD_DIM // 2
MOBA_BLOCK = 256
MOBA_TOPK = 3
ATTN_HEADS = 2
Q_BLOCKS = 4
KV_GROUP = 4
SCORE_LEAD = 1
BF16_SUBLANES = 16
V_ROWS = HEAD_DIM + BF16_SUBLANES
ROPE_THETA = 10000.0
SSM_GROUP = 16
SSM_CHUNK = 8
CHUNK_W = SSM_CHUNK * SSM_GROUP
SCAN_ROWS = 8
LANE = 128
GROUPS_PER_SLAB = LANE // SSM_GROUP
SLAB_W = SSM_CHUNK * LANE
RMS_EPS = 1e-6
NEG_BIG = -1e30
TOKEN_TILE = 512
IN_PROJ_TILE = 1024
VMEM_LIMIT = 56 * 1024 * 1024

_NT = (((1,), (1,)), ((), ()))


def _cparams(*sem):
    return pltpu.CompilerParams(dimension_semantics=sem, vmem_limit_bytes=VMEM_LIMIT)


def _const_spec(shape):
    nd = len(shape)
    return pl.BlockSpec(shape, lambda *_: (0,) * nd)


def _resident_spec(shape):
    nd = len(shape)
    return pl.BlockSpec(shape, lambda *_: (0,) * nd, pipeline_mode=pl.Buffered(1))


def _cexp(lr, li, step, e):
    mag = jnp.exp(lr * step * e)
    ang = li * step * e
    return mag * jnp.cos(ang), mag * jnp.sin(ang)


def _spread(rows, cols, target_fn):
    r = lax.broadcasted_iota(jnp.int32, (rows, cols), 0)
    c = lax.broadcasted_iota(jnp.int32, (rows, cols), 1)
    return jnp.where(c == target_fn(r), 1.0, 0.0).astype(F32)


def _ssm_prep_kernel(lrr_ref, lir_ref, ls_ref, btr_ref, bti_ref, ctr_ref, cti_ref, dt_ref,
                     mx_ref, gx_ref, hx_ref, apx_ref, aix_ref, dx_ref):
    n = lrr_ref.shape[-1]
    gl = lax.rem(pl.program_id(0), GROUPS_PER_SLAB)
    hp = lax.Precision.HIGHEST
    place = functools.partial(jnp.dot, precision=hp, preferred_element_type=F32)
    place16 = lambda a, spread: jnp.dot(a.astype(BF16), spread.astype(BF16), preferred_element_type=F32)
    to_slab = _spread(CHUNK_W, SLAB_W,
                      lambda r: lax.div(r, SSM_GROUP) * LANE + gl * SSM_GROUP + lax.rem(r, SSM_GROUP))
    to_state = _spread(n, GROUPS_PER_SLAB * n, lambda r: gl * n + r)
    to_state2 = _spread(2 * n, 2 * GROUPS_PER_SLAB * n,
                        lambda r: jnp.where(r >= n, GROUPS_PER_SLAB * n - n, 0) + gl * n + r)
    step = jnp.exp(ls_ref[0])
    lr_r, li_r = lrr_ref[0], lir_ref[0]
    prow = lax.broadcasted_iota(jnp.int32, (SSM_CHUNK + SCAN_ROWS, 1), 0)
    pw_r, pw_i = _cexp(lr_r, li_r, step,
                       jnp.where(prow <= SSM_CHUNK, prow, (prow - SSM_CHUNK + 1) * SSM_CHUNK).astype(F32))
    tau = lax.div(lax.broadcasted_iota(jnp.int32, pw_r.shape[:1] + (CHUNK_W,), 1), SSM_GROUP)
    pick = lambda shift: jnp.where(tau + shift == lax.broadcasted_iota(jnp.int32, tau.shape, 0), 1.0, 0.0)
    on_sublanes = lambda p, shift: lax.dot_general(p, pick(shift), (((0,), (0,)), ((), ())), precision=hp,
                                                   preferred_element_type=F32)
    ar, ai = pw_r[1:2], pw_i[1:2]
    den = lr_r * lr_r + li_r * li_r
    nr, ni = ar - 1.0, ai
    cr = (nr * lr_r + ni * li_r) / den
    ci = (ni * lr_r - nr * li_r) / den
    btr, bti = btr_ref[0], bti_ref[0]
    bbr = cr * btr - ci * bti
    bbi = cr * bti + ci * btr
    rows_of = lambda p: jnp.concatenate(
        [jnp.broadcast_to(p[SSM_CHUNK - 1 - s:SSM_CHUNK - s], (SSM_GROUP, n)) for s in range(SSM_CHUNK)], axis=0)
    pr, pi_ = rows_of(pw_r), rows_of(pw_i)
    gcat = jnp.concatenate([pr * bbr - pi_ * bbi, pr * bbi + pi_ * bbr], axis=1)
    gwide = place16(gcat, to_state2).astype(gx_ref.dtype)
    row0 = pl.multiple_of(gl * SSM_GROUP, SSM_GROUP)
    for s in range(SSM_CHUNK):
        gx_ref[0, pl.ds(s * LANE + row0, SSM_GROUP), :] = gwide[s * SSM_GROUP:(s + 1) * SSM_GROUP]
    ctr, cti = ctr_ref[0], cti_ref[0]
    wr, wi = on_sublanes(pw_r, 0), on_sublanes(pw_i, 0)
    p_r = ctr * wr - cti * wi
    p_i = ctr * wi + cti * wr
    r0 = place(bbr[:SSM_GROUP], p_r) - place(bbi[:SSM_GROUP], p_i)
    r0w = place16(r0, to_slab)
    lane = lax.broadcasted_iota(jnp.int32, (SSM_GROUP, SLAB_W), 1)
    for s in range(SSM_CHUNK):
        blk = r0w if s == 0 else pltpu.roll(r0w, LANE * s, axis=1)
        blk = jnp.where(lane >= LANE * s, blk, 0.0)
        mx_ref[0, pl.ds(s * LANE + row0, SSM_GROUP), :] = blk.astype(mx_ref.dtype)
    hr, hi = on_sublanes(pw_r, 1), on_sublanes(pw_i, 1)
    st0 = pl.multiple_of(gl * n, n)
    hx_ref[0, pl.ds(st0, n), :] = place16(ctr * hr - cti * hi, to_slab).astype(hx_ref.dtype)
    hx_ref[0, pl.ds(GROUPS_PER_SLAB * n + st0, n), :] = place16(-(ctr * hi + cti * hr), to_slab).astype(hx_ref.dtype)
    qr, qi = pw_r[SSM_CHUNK:], pw_i[SSM_CHUNK:]
    shared = ((apx_ref, place(qr, to_state)), (aix_ref, place(qi, to_state)), (dx_ref, place(dt_ref[0], to_slab)))

    @pl.when(gl == 0)
    def _():
        for ref, val in shared:
            ref[0] = val

    @pl.when(gl != 0)
    def _():
        for ref, val in shared:
            ref[0] = ref[0] + val


def _ssm_prep(lam_re, lam_im, log_step, b_re, b_im, c_re, c_im, d_skip):
    g, n = lam_re.shape
    ns = g // GROUPS_PER_SLAB
    sl = GROUPS_PER_SLAB * n
    bt = lambda b: jnp.tile(jnp.swapaxes(b, 1, 2), (1, SSM_CHUNK, 1))
    ct = lambda m: jnp.tile(jnp.swapaxes(m, 1, 2), (1, 1, SSM_CHUNK))
    args = (lam_re.reshape(g, 1, n), lam_im.reshape(g, 1, n), log_step.reshape(g, 1, 1),
            bt(b_re), bt(b_im), ct(c_re), ct(c_im),
            jnp.tile(d_skip.astype(F32), (1, SSM_CHUNK)).reshape(g, 1, CHUNK_W))
    spec = lambda shp: pl.BlockSpec((1,) + shp, lambda i: (i, 0, 0))
    slab = lambda shp: pl.BlockSpec((1,) + shp, lambda i: (i // GROUPS_PER_SLAB, 0, 0))
    in_specs = [spec((1, n)), spec((1, n)), spec((1, 1)),
                spec((CHUNK_W, n)), spec((CHUNK_W, n)), spec((n, CHUNK_W)), spec((n, CHUNK_W)), spec((1, CHUNK_W))]
    out_shapes = [((SLAB_W, SLAB_W), BF16), ((SLAB_W, 2 * sl), BF16), ((2 * sl, SLAB_W), BF16),
                  ((SCAN_ROWS, sl), F32), ((SCAN_ROWS, sl), F32), ((1, SLAB_W), F32)]
    return pl.pallas_call(
        _ssm_prep_kernel,
        grid=(g,),
        in_specs=in_specs,
        out_specs=[slab(s) for s, _ in out_shapes],
        out_shape=[jax.ShapeDtypeStruct((ns,) + s, d) for s, d in out_shapes],
        compiler_params=_cparams("arbitrary"),
        name="ssm_prep",
    )(*args)


def _in_proj_kernel(x_ref, g_ref, wqt_ref, wk_ref, wvt_ref, wu_ref, wg_ref,
                    cos_ref, sin_ref, cost_ref, sint_ref,
                    qt_ref, k_ref, vt_ref, u_ref, sga_ref, sgb_ref):
    x = x_ref[0]
    d = x.shape[-1]
    ms = jnp.mean(x * x, axis=-1, keepdims=True)
    h = (x * lax.rsqrt(ms + RMS_EPS) * g_ref[...]).astype(BF16)
    for c0 in range(0, 2 * d, 512):
        sg = jax.nn.sigmoid(jnp.dot(h, wg_ref[:, c0:c0 + 512], preferred_element_type=F32)).astype(BF16)
        if c0 < d:
            sga_ref[0, :, c0:c0 + 512] = sg
        else:
            sgb_ref[0, :, c0 - d:c0 - d + 512] = sg
    cos, sin = cos_ref[...], sin_ref[...]
    for c0 in range(0, d, 512):
        p = jnp.dot(h, wk_ref[:, c0:c0 + 512], preferred_element_type=F32)
        for hh in range(512 // HEAD_DIM):
            sl = p[:, hh * HEAD_DIM:(hh + 1) * HEAD_DIM]
            r = sl * cos + pltpu.roll(sl, HALF_DIM, axis=1) * sin
            k_ref[0, :, c0 + hh * HEAD_DIM:c0 + (hh + 1) * HEAD_DIM] = r.astype(BF16)
    cost, sint = cost_ref[...], sint_ref[...]
    scale = HEAD_DIM ** -0.5 * math.log2(math.e)
    for c0 in range(0, d, 256):
        pt = lax.dot_general(wqt_ref[c0:c0 + 256, :], h, _NT, preferred_element_type=F32)
        for hh in range(256 // HEAD_DIM):
            r0 = hh * HEAD_DIM
            x1 = pt[r0:r0 + HALF_DIM]
            x2 = pt[r0 + HALF_DIM:r0 + HEAD_DIM]
            qt_ref[0, c0 + r0:c0 + r0 + HALF_DIM, :] = ((x1 * cost - x2 * sint) * scale).astype(BF16)
            qt_ref[0, c0 + r0 + HALF_DIM:c0 + r0 + HEAD_DIM, :] = ((x2 * cost + x1 * sint) * scale).astype(BF16)
    ones = jnp.ones((BF16_SUBLANES, x.shape[0]), BF16)
    for c0 in range(0, d, 256):
        pv = lax.dot_general(wvt_ref[c0:c0 + 256, :], h, _NT, preferred_element_type=F32).astype(BF16)
        for hh in range(256 // HEAD_DIM):
            r0 = (c0 // HEAD_DIM + hh) * V_ROWS
            vt_ref[0, r0:r0 + HEAD_DIM, :] = pv[hh * HEAD_DIM:(hh + 1) * HEAD_DIM]
            vt_ref[0, r0 + HEAD_DIM:r0 + V_ROWS, :] = ones
    pu = jnp.dot(h, wu_ref[...], preferred_element_type=F32)
    for s in range(pu.shape[1] // LANE):
        u_ref[s, 0] = pu[:, s * LANE:(s + 1) * LANE]


def _in_proj(x, g, w_in):
    b, l, d = x.shape
    tm = IN_PROJ_TILE
    sw = w_in.shape[1] - 5 * d
    wqt = w_in[:, :d].T.astype(BF16)
    wk = w_in[:, d:2 * d].astype(BF16)
    wvt = w_in[:, 2 * d:3 * d].T.astype(BF16)
    wu = w_in[:, 3 * d:3 * d + sw].astype(BF16)
    wg = w_in[:, 3 * d + sw:].astype(BF16)
    inv_freq = ROPE_THETA ** (-jnp.arange(HALF_DIM, dtype=F32) / HALF_DIM)
    ang = jnp.arange(l).astype(F32)[:, None] * inv_freq[None, :]
    cos, sin = jnp.cos(ang), jnp.sin(ang)
    cos2 = jnp.concatenate([cos, cos], axis=1)
    sin2 = jnp.concatenate([-sin, sin], axis=1)
    tok = lambda w: pl.BlockSpec((1, tm, w), lambda bi, i: (bi, i, 0))
    feat = lambda w: pl.BlockSpec((1, w, tm), lambda bi, i: (bi, 0, i))
    sds = jax.ShapeDtypeStruct
    return pl.pallas_call(
        _in_proj_kernel,
        grid=(b, l // tm),
        in_specs=[tok(d), _const_spec((1, d)), _resident_spec((d, d)), _resident_spec((d, d)), _resident_spec((d, d)),
                  _resident_spec((d, sw)), _resident_spec((d, 2 * d)),
                  pl.BlockSpec((tm, HEAD_DIM), lambda bi, i: (i, 0)),
                  pl.BlockSpec((tm, HEAD_DIM), lambda bi, i: (i, 0)),
                  pl.BlockSpec((HALF_DIM, tm), lambda bi, i: (0, i)),
                  pl.BlockSpec((HALF_DIM, tm), lambda bi, i: (0, i))],
        out_specs=[feat(d), tok(d), feat(N_HEADS * V_ROWS),
                   pl.BlockSpec((sw // LANE, 1, tm, LANE), lambda bi, i: (0, bi, i, 0)), tok(d), tok(d)],
        out_shape=[sds((b, d, l), BF16), sds((b, l, d), BF16), sds((b, N_HEADS * V_ROWS, l), BF16),
                   sds((sw // LANE, b, l, LANE), F32), sds((b, l, d), BF16), sds((b, l, d), BF16)],
        compiler_params=_cparams("parallel", "parallel"),
        name="in_proj",
    )(x, g.reshape(1, d), wqt, wk, wvt, wu, wg, cos2, sin2, cos.T, sin.T)


def _ssm_chunk_kernel(u_ref, mx_ref, gx_ref, hx_ref, apr_ref, api_ref, d_ref,
                      y_ref, ere_ref, eim_ref, xre_ref, xim_ref):
    nk = u_ref.shape[2] // SSM_CHUNK
    sl = ere_ref.shape[1]
    uf = jnp.concatenate([u_ref[0, 0, pl.ds(t, nk, stride=SSM_CHUNK), :] for t in range(SSM_CHUNK)], axis=1)
    u = uf.astype(BF16)
    dot = functools.partial(jnp.dot, preferred_element_type=F32)
    e = dot(u, gx_ref[0])
    ere_ref[...] = e[:, :sl]
    eim_ref[...] = e[:, sl:]
    apr, api = apr_ref[0], api_ref[0]
    row = lax.broadcasted_iota(jnp.int32, apr.shape, 0)
    lvl = [(dd, jnp.where(row >= dd, jnp.broadcast_to(apr[dd - 1:dd], apr.shape), 0.0),
            jnp.where(row >= dd, jnp.broadcast_to(api[dd - 1:dd], apr.shape), 0.0)) for dd in (1, 2, 4)]

    def tile(t, carry):
        cr, ci = carry
        r0 = pl.multiple_of(t * SCAN_ROWS, SCAN_ROWS)
        zr = ere_ref[pl.ds(r0, SCAN_ROWS), :]
        zi = eim_ref[pl.ds(r0, SCAN_ROWS), :]
        for dd, adr, adi in lvl:
            sr = pltpu.roll(zr, dd, axis=0)
            si = pltpu.roll(zi, dd, axis=0)
            zr, zi = zr + (adr * sr - adi * si), zi + (adr * si + adi * sr)
        sr = apr * cr - api * ci + zr
        si = apr * ci + api * cr + zi
        xre_ref[pl.ds(r0, SCAN_ROWS), :] = jnp.where(row >= 1, pltpu.roll(sr, 1, axis=0), cr)
        xim_ref[pl.ds(r0, SCAN_ROWS), :] = jnp.where(row >= 1, pltpu.roll(si, 1, axis=0), ci)
        last = SCAN_ROWS - 1
        return (jnp.broadcast_to(sr[last:last + 1], sr.shape), jnp.broadcast_to(si[last:last + 1], si.shape))

    zero = jnp.zeros(apr.shape, F32)
    lax.fori_loop(0, nk // SCAN_ROWS, tile, (zero, zero))
    x0 = jnp.concatenate([xre_ref[...], xim_ref[...]], axis=1).astype(BF16)
    half = SLAB_W // 2
    intra = jnp.concatenate([dot(u[:, :half], mx_ref[0, :half, :half]), dot(u, mx_ref[0, :, half:])], axis=1)
    y = jax.nn.gelu(intra + dot(x0, hx_ref[0]) + d_ref[0] * uf)
    for t in range(SSM_CHUNK):
        y_ref[0, 0, pl.ds(t, nk, stride=SSM_CHUNK), :] = y[:, t * LANE:(t + 1) * LANE]


def _ssm_chunk(u4, mx, gx, hx, apx, aix, dx):
    ns, b, l, lane = u4.shape
    nk = l // SSM_CHUNK
    sl = apx.shape[-1]
    per_slab = lambda r, c: pl.BlockSpec((1, r, c), lambda s, bi: (s, 0, 0))
    rows = pl.BlockSpec((1, 1, l, lane), lambda s, bi: (s, bi, 0, 0))
    return pl.pallas_call(
        _ssm_chunk_kernel,
        grid=(ns, b),
        in_specs=[rows, per_slab(SLAB_W, SLAB_W), per_slab(SLAB_W, 2 * sl), per_slab(2 * sl, SLAB_W),
                  per_slab(SCAN_ROWS, sl), per_slab(SCAN_ROWS, sl), per_slab(1, SLAB_W)],
        out_specs=rows,
        out_shape=jax.ShapeDtypeStruct(u4.shape, F32),
        scratch_shapes=[pltpu.VMEM((nk, sl), F32) for _ in range(4)],
        compiler_params=_cparams("parallel", "parallel"),
        name="ssm_chunk",
    )(u4, mx, gx, hx, apx, aix, dx)


def _moba_kernel(qt_ref, k_ref, vt_ref, o_ref, *scratch):
    i = pl.program_id(2)
    blk = MOBA_BLOCK
    qw = Q_BLOCKS * blk
    nb = k_ref.shape[1] // blk
    fb = i * Q_BLOCKS
    heads = range(ATTN_HEADS)
    hsl = lambda hp: slice(hp * HEAD_DIM, (hp + 1) * HEAD_DIM)
    kmean_ref, sel_ref, m_ref, acc_ref, s_ref, cm_ref, ksplit_ref = (
        scratch[r * ATTN_HEADS:(r + 1) * ATTN_HEADS] for r in range(7))

    @pl.when(i == 0)
    def _():
        for hp in heads:
            for j in range(nb):
                kb = k_ref[0, j * blk:(j + 1) * blk, hsl(hp)].astype(F32)
                kmean_ref[hp][j:j + 1, :] = jnp.mean(kb, axis=0, keepdims=True)
            rest = kmean_ref[hp][...]
            for part in range(3):
                piece = rest.astype(BF16)
                ksplit_ref[hp][part * nb:(part + 1) * nb, :] = piece
                rest = rest - piece.astype(F32)

    vsl = lambda hp: slice(hp * V_ROWS, (hp + 1) * V_ROWS)

    def select_blocks(hp, qt):
        g3 = jnp.dot(ksplit_ref[hp][...], qt, preferred_element_type=F32)
        gate = g3[:nb] + g3[nb:2 * nb] + g3[2 * nb:]
        brow = lax.broadcasted_iota(jnp.int32, gate.shape, 0)
        own = fb + lax.broadcasted_iota(jnp.int32, (1, qw), 1) // blk
        work = jnp.where(brow < own, gate, -jnp.inf)
        sel = jnp.zeros(gate.shape, F32)
        for _ in range(MOBA_TOPK):
            mx = jnp.max(work, axis=0, keepdims=True)
            idx = jnp.min(jnp.where(work == mx, brow, nb), axis=0, keepdims=True)
            hit = brow == idx
            sel = jnp.where(hit, jnp.where(idx < own, 1.0, 0.0), sel)
            work = jnp.where(hit, -jnp.inf, work)
        sel_ref[hp][...] = sel

    def item_blocks(g):
        first, count = (fb, Q_BLOCKS) if g is None else (g * KV_GROUP, KV_GROUP)
        if isinstance(first, int):
            return [slice((first + jl) * blk, (first + jl + 1) * blk) for jl in range(count)]
        return [pl.ds(pl.multiple_of((first + jl) * blk, blk), blk) for jl in range(count)]

    def score_steps(hp, qt, g):
        buf = s_ref[hp]

        def step(jl, rows):
            if g is None:
                c0 = jl * blk
                s = jnp.dot(k_ref[0, rows, hsl(hp)], qt[:, c0:], preferred_element_type=F32)
                krow = lax.broadcasted_iota(jnp.int32, s.shape, 0)
                qcol = lax.broadcasted_iota(jnp.int32, s.shape, 1)
                picked = sel_ref[hp][pl.ds(fb + jl, 1), c0:] > 0.0
                s = jnp.where(qcol >= blk, jnp.where(picked, s, NEG_BIG), jnp.where(krow <= qcol, s, NEG_BIG))
                buf[jl * blk:(jl + 1) * blk, c0:] = s
                cmax = jnp.max(s, axis=0, keepdims=True)
                if c0:
                    cmax = jnp.concatenate([jnp.full((1, c0), NEG_BIG, F32), cmax], axis=1)
                cm_ref[hp][jl:jl + 1, :] = cmax
                return
            s = jnp.dot(k_ref[0, rows, hsl(hp)], qt, preferred_element_type=F32)
            buf[jl * blk:(jl + 1) * blk, :] = s
            cm_ref[hp][jl:jl + 1, :] = jnp.max(s, axis=0, keepdims=True)

        return [functools.partial(step, jl, rows) for jl, rows in enumerate(item_blocks(g))]

    def weight_steps(hp, g):
        buf = s_ref[hp]
        refs = []

        def step(jl, rows):
            if jl == 0:
                cmax = [cm_ref[hp][j:j + 1, :] for j in range(Q_BLOCKS if g is None else KV_GROUP)]
                if g is None:
                    m_new = functools.reduce(jnp.maximum, cmax)
                    refs.extend(m_new for _ in cmax)
                else:
                    picked = [jnp.logical_and(sel_ref[hp][pl.ds(g * KV_GROUP + j, 1), :] > 0.0,
                                              g * KV_GROUP + j < fb) for j in range(KV_GROUP)]
                    m_old = m_ref[hp][...]
                    m_new = m_old
                    for j in range(KV_GROUP):
                        m_new = jnp.maximum(m_new, jnp.where(picked[j], cmax[j], NEG_BIG))
                    acc_ref[hp][...] = jnp.exp2(m_old - m_new) * acc_ref[hp][...]
                    refs.extend(jnp.where(picked[j], m_new, -NEG_BIG) for j in range(KV_GROUP))
                m_ref[hp][...] = m_new
            c0 = jl * blk if g is None else 0
            p = jnp.exp2(buf[jl * blk:(jl + 1) * blk, c0:] - refs[jl][:, c0:]).astype(BF16)
            pv = jnp.dot(vt_ref[0, vsl(hp), rows], p, preferred_element_type=F32)
            if g is None and jl == 0:
                acc_ref[hp][...] = pv
            else:
                acc_ref[hp][:, c0:] = acc_ref[hp][:, c0:] + pv

        return [functools.partial(step, jl, rows) for jl, rows in enumerate(item_blocks(g))]

    def interleave(w_steps, s_steps):
        for n in range(max(len(w_steps) + SCORE_LEAD, len(s_steps))):
            if n < len(s_steps):
                s_steps[n]()
            if SCORE_LEAD <= n < len(w_steps) + SCORE_LEAD:
                w_steps[n - SCORE_LEAD]()

    qts = [qt_ref[0, hsl(hp), :] for hp in heads]
    for hp in heads:
        select_blocks(hp, qts[hp])
    interleave([], score_steps(0, qts[0], None))
    for hp in heads:
        nhp = (hp + 1) % ATTN_HEADS
        interleave(weight_steps(hp, None), score_steps(nhp, qts[nhp], None if nhp else 0))

    ngroups = (fb + KV_GROUP - 1) // KV_GROUP

    def past_group(g, _, last=False):
        for hp in heads:
            nhp = (hp + 1) % ATTN_HEADS
            ahead = [] if (last and not nhp) else score_steps(nhp, qts[nhp], g if nhp else g + 1)
            interleave(weight_steps(hp, g), ahead)
        return 0

    n_ahead = jnp.maximum(ngroups - 1, 0)

    def two_groups(gg, _):
        past_group(2 * gg, 0)
        past_group(2 * gg + 1, 0)
        return 0

    lax.fori_loop(0, n_ahead // 2, two_groups, 0)

    @pl.when(n_ahead % 2 == 1)
    def _():
        past_group(n_ahead - 1, 0)

    @pl.when(ngroups > 0)
    def _():
        past_group(ngroups - 1, 0, last=True)

    for hp in heads:
        ot = acc_ref[hp][:HEAD_DIM, :] / acc_ref[hp][HEAD_DIM:HEAD_DIM + 1, :]
        o_ref[0, :, hsl(hp)] = ot.T.astype(o_ref.dtype)


def _moba_attention(qt, k, vt):
    b, l, d = k.shape
    nb = l // MOBA_BLOCK
    assert nb % KV_GROUP == 0 and nb % Q_BLOCKS == 0
    hw = ATTN_HEADS * HEAD_DIM
    qw = Q_BLOCKS * MOBA_BLOCK
    return pl.pallas_call(
        _moba_kernel,
        grid=(b, N_HEADS // ATTN_HEADS, nb // Q_BLOCKS),
        in_specs=[pl.BlockSpec((1, hw, qw), lambda bi, h, i: (bi, h, i)),
                  pl.BlockSpec((1, l, hw), lambda bi, h, i: (bi, 0, h)),
                  pl.BlockSpec((1, ATTN_HEADS * V_ROWS, l), lambda bi, h, i: (bi, h, 0))],
        out_specs=pl.BlockSpec((1, qw, hw), lambda bi, h, i: (bi, i, h)),
        out_shape=jax.ShapeDtypeStruct((b, l, d), BF16),
        scratch_shapes=[pltpu.VMEM(shape, dtype)
                        for shape, dtype in (((nb, HEAD_DIM), F32), ((nb, qw), F32), ((1, qw), F32),
                                             ((V_ROWS, qw), F32), ((KV_GROUP * MOBA_BLOCK, qw), F32),
                                             ((KV_GROUP, qw), F32), ((3 * nb, HEAD_DIM), BF16))
                        for _ in range(ATTN_HEADS)],
        compiler_params=_cparams("parallel", "parallel", "arbitrary"),
        name="moba_attn",
    )(qt, k, vt)


def _merge_mlp_kernel(x_ref, oa_ref, sga_ref, sgb_ref, y_ref, wglu_ref, wout_ref, g1_ref, wup_ref, wdn_ref, g2_ref,
                      o_ref, *, final_norm):
    d = x_ref.shape[-1]
    ff = wup_ref.shape[1]
    y = jnp.concatenate([y_ref[s, 0] for s in range(y_ref.shape[0])], axis=1)
    hgl = jnp.dot(y.astype(BF16), wglu_ref[...], preferred_element_type=F32)
    ob = hgl[:, :d] * jax.nn.sigmoid(hgl[:, d:])
    mixed = sga_ref[0].astype(F32) * oa_ref[0].astype(F32) + sgb_ref[0].astype(F32) * ob
    x = x_ref[0] + jnp.dot(mixed.astype(BF16), wout_ref[...], preferred_element_type=F32)
    ms = jnp.mean(x * x, axis=-1, keepdims=True)
    h = (x * lax.rsqrt(ms + RMS_EPS) * g1_ref[...]).astype(BF16)
    acc = x
    for c0 in range(0, ff, d):
        a = jnp.maximum(jnp.dot(h, wup_ref[:, c0:c0 + d], preferred_element_type=F32), 0.0)
        acc = acc + jnp.dot((a * a).astype(BF16), wdn_ref[c0:c0 + d, :], preferred_element_type=F32)
    if final_norm:
        ms2 = jnp.mean(acc * acc, axis=-1, keepdims=True)
        acc = acc * lax.rsqrt(ms2 + RMS_EPS) * g2_ref[...]
    o_ref[0] = acc


def _merge_mlp(x, oa, sga, sgb, y, w_glu, w_out, g1, w_up, w_down, g2, final_norm):
    b, l, d = x.shape
    tm = TOKEN_TILE
    ns, lane = y.shape[0], y.shape[-1]
    sw = ns * lane
    ff = w_up.shape[1]
    tok = lambda w: pl.BlockSpec((1, tm, w), lambda bi, i: (bi, i, 0))
    return pl.pallas_call(
        functools.partial(_merge_mlp_kernel, final_norm=final_norm),
        grid=(b, l // tm),
        in_specs=[tok(d), tok(d), tok(d), tok(d), pl.BlockSpec((ns, 1, tm, lane), lambda bi, i: (0, bi, i, 0)),
                  _resident_spec((sw, 2 * d)), _resident_spec((d, d)), _const_spec((1, d)),
                  _resident_spec((d, ff)), _resident_spec((ff, d)), _const_spec((1, d))],
        out_specs=tok(d),
        out_shape=jax.ShapeDtypeStruct((b, l, d), F32),
        compiler_params=_cparams("parallel", "parallel"),
        name="merge_mlp",
    )(x, oa, sga, sgb, y, w_glu.astype(BF16), w_out.astype(BF16), g1.reshape(1, d),
      w_up.astype(BF16), w_down.astype(BF16), g2.reshape(1, d))


def kernel(x, norm_mix_g, w_in, lam_re, lam_im, log_step, b_re, b_im, c_re, c_im, d_skip, w_glu, w_out,
           norm_mlp_g, w_up, w_down, norm_final_g):
    b, l, d = x.shape
    depth = w_in.shape[0]
    assert d == N_HEADS * HEAD_DIM and l % TOKEN_TILE == 0 and l % IN_PROJ_TILE == 0
    assert l % (SSM_CHUNK * SCAN_ROWS) == 0
    for i in range(depth):
        slab_mats = _ssm_prep(lam_re[i], lam_im[i], log_step[i], b_re[i], b_im[i], c_re[i], c_im[i], d_skip[i])
        qt, k, vt, u, sga, sgb = _in_proj(x, norm_mix_g[i], w_in[i])
        y = _ssm_chunk(u, *slab_mats)
        oa = _moba_attention(qt, k, vt)
        x = _merge_mlp(x, oa, sga, sgb, y, w_glu[i], w_out[i], norm_mlp_g[i], w_up[i], w_down[i], norm_final_g,
                       final_norm=(i == depth - 1))
    return x
```

```python
import functools
import math

import jax
import jax.numpy as jnp
from jax import lax
from jax.experimental import pallas as pl
from jax.experimental.pallas import tpu as pltpu

F32 = jnp.float32
BF16 = jnp.bfloat16

N_HEADS = 8
HEAD_DIM = 128
HALF_DIM = HEAD_DIM // 2
MOBA_BLOCK = 256
MOBA_TOPK = 3
ATTN_HEADS = 2
Q_BLOCKS = 4
KV_GROUP = 4
SCORE_LEAD = 1
BF16_SUBLANES = 16
V_ROWS = HEAD_DIM + BF16_SUBLANES
ROPE_THETA = 10000.0
SSM_GROUP = 16
SSM_CHUNK = 8
CHUNK_W = SSM_CHUNK * SSM_GROUP
SCAN_ROWS = 8
INTRA_PIECES = 8
LANE = 128
GROUPS_PER_SLAB = LANE // SSM_GROUP
SLAB_W = SSM_CHUNK * LANE
RMS_EPS = 1e-6
NEG_BIG = -1e30
TOKEN_TILE = 512
IN_PROJ_TILE = 1024
VMEM_LIMIT = 56 * 1024 * 1024

_NT = (((1,), (1,)), ((), ()))


def _cparams(*sem):
    return pltpu.CompilerParams(dimension_semantics=sem, vmem_limit_bytes=VMEM_LIMIT)


def _const_spec(shape):
    nd = len(shape)
    return pl.BlockSpec(shape, lambda *_: (0,) * nd)


def _resident_spec(shape):
    nd = len(shape)
    return pl.BlockSpec(shape, lambda *_: (0,) * nd, pipeline_mode=pl.Buffered(1))


def _cexp(lr, li, step, e):
    mag = jnp.exp(lr * step * e)
    ang = li * step * e
    return mag * jnp.cos(ang), mag * jnp.sin(ang)


def _spread(rows, cols, target_fn):
    r = lax.broadcasted_iota(jnp.int32, (rows, cols), 0)
    c = lax.broadcasted_iota(jnp.int32, (rows, cols), 1)
    return jnp.where(c == target_fn(r), 1.0, 0.0).astype(F32)


def _ssm_prep_kernel(lrr_ref, lir_ref, ls_ref, btr_ref, bti_ref, ctr_ref, cti_ref, dt_ref,
                     mx_ref, gx_ref, hx_ref, apx_ref, aix_ref, dx_ref):
    n = lrr_ref.shape[-1]
    gl = lax.rem(pl.program_id(0), GROUPS_PER_SLAB)
    hp = lax.Precision.HIGHEST
    place = functools.partial(jnp.dot, precision=hp, preferred_element_type=F32)
    place16 = lambda a, spread: jnp.dot(a.astype(BF16), spread.astype(BF16), preferred_element_type=F32)
    to_slab = _spread(CHUNK_W, SLAB_W,
                      lambda r: lax.div(r, SSM_GROUP) * LANE + gl * SSM_GROUP + lax.rem(r, SSM_GROUP))
    to_state = _spread(n, GROUPS_PER_SLAB * n, lambda r: gl * n + r)
    to_state2 = _spread(2 * n, 2 * GROUPS_PER_SLAB * n,
                        lambda r: jnp.where(r >= n, GROUPS_PER_SLAB * n - n, 0) + gl * n + r)
    step = jnp.exp(ls_ref[0])
    lr_r, li_r = lrr_ref[0], lir_ref[0]
    prow = lax.broadcasted_iota(jnp.int32, (SSM_CHUNK + SCAN_ROWS, 1), 0)
    pw_r, pw_i = _cexp(lr_r, li_r, step,
                       jnp.where(prow <= SSM_CHUNK, prow, (prow - SSM_CHUNK + 1) * SSM_CHUNK).astype(F32))
    tau = lax.div(lax.broadcasted_iota(jnp.int32, pw_r.shape[:1] + (CHUNK_W,), 1), SSM_GROUP)
    pick = lambda shift: jnp.where(tau + shift == lax.broadcasted_iota(jnp.int32, tau.shape, 0), 1.0, 0.0)
    on_sublanes = lambda p, shift: lax.dot_general(p, pick(shift), (((0,), (0,)), ((), ())), precision=hp,
                                                   preferred_element_type=F32)
    ar, ai = pw_r[1:2], pw_i[1:2]
    den = lr_r * lr_r + li_r * li_r
    nr, ni = ar - 1.0, ai
    cr = (nr * lr_r + ni * li_r) / den
    ci = (ni * lr_r - nr * li_r) / den
    btr, bti = btr_ref[0], bti_ref[0]
    bbr = cr * btr - ci * bti
    bbi = cr * bti + ci * btr
    rows_of = lambda p: jnp.concatenate(
        [jnp.broadcast_to(p[SSM_CHUNK - 1 - s:SSM_CHUNK - s], (SSM_GROUP, n)) for s in range(SSM_CHUNK)], axis=0)
    pr, pi_ = rows_of(pw_r), rows_of(pw_i)
    gcat = jnp.concatenate([pr * bbr - pi_ * bbi, pr * bbi + pi_ * bbr], axis=1)
    gwide = place16(gcat, to_state2).astype(gx_ref.dtype)
    row0 = pl.multiple_of(gl * SSM_GROUP, SSM_GROUP)
    for s in range(SSM_CHUNK):
        gx_ref[0, pl.ds(s * LANE + row0, SSM_GROUP), :] = gwide[s * SSM_GROUP:(s + 1) * SSM_GROUP]
    ctr, cti = ctr_ref[0], cti_ref[0]
    wr, wi = on_sublanes(pw_r, 0), on_sublanes(pw_i, 0)
    p_r = ctr * wr - cti * wi
    p_i = ctr * wi + cti * wr
    r0 = place(bbr[:SSM_GROUP], p_r) - place(bbi[:SSM_GROUP], p_i)
    r0w = place16(r0, to_slab)
    lane = lax.broadcasted_iota(jnp.int32, (SSM_GROUP, SLAB_W), 1)
    for s in range(SSM_CHUNK):
        blk = r0w if s == 0 else pltpu.roll(r0w, LANE * s, axis=1)
        blk = jnp.where(lane >= LANE * s, blk, 0.0)
        mx_ref[0, pl.ds(s * LANE + row0, SSM_GROUP), :] = blk.astype(mx_ref.dtype)
    hr, hi = on_sublanes(pw_r, 1), on_sublanes(pw_i, 1)
    st0 = pl.multiple_of(gl * n, n)
    hx_ref[0, pl.ds(st0, n), :] = place16(ctr * hr - cti * hi, to_slab).astype(hx_ref.dtype)
    hx_ref[0, pl.ds(GROUPS_PER_SLAB * n + st0, n), :] = place16(-(ctr * hi + cti * hr), to_slab).astype(hx_ref.dtype)
    qr, qi = pw_r[SSM_CHUNK:], pw_i[SSM_CHUNK:]
    shared = ((apx_ref, place(qr, to_state)), (aix_ref, place(qi, to_state)), (dx_ref, place(dt_ref[0], to_slab)))

    @pl.when(gl == 0)
    def _():
        for ref, val in shared:
            ref[0] = val

    @pl.when(gl != 0)
    def _():
        for ref, val in shared:
            ref[0] = ref[0] + val


def _ssm_prep(lam_re, lam_im, log_step, b_re, b_im, c_re, c_im, d_skip):
    g, n = lam_re.shape
    ns = g // GROUPS_PER_SLAB
    sl = GROUPS_PER_SLAB * n
    bt = lambda b: jnp.tile(jnp.swapaxes(b, 1, 2), (1, SSM_CHUNK, 1))
    ct = lambda m: jnp.tile(jnp.swapaxes(m, 1, 2), (1, 1, SSM_CHUNK))
    args = (lam_re.reshape(g, 1, n), lam_im.reshape(g, 1, n), log_step.reshape(g, 1, 1),
            bt(b_re), bt(b_im), ct(c_re), ct(c_im),
            jnp.tile(d_skip.astype(F32), (1, SSM_CHUNK)).reshape(g, 1, CHUNK_W))
    spec = lambda shp: pl.BlockSpec((1,) + shp, lambda i: (i, 0, 0))
    slab = lambda shp: pl.BlockSpec((1,) + shp, lambda i: (i // GROUPS_PER_SLAB, 0, 0))
    in_specs = [spec((1, n)), spec((1, n)), spec((1, 1)),
                spec((CHUNK_W, n)), spec((CHUNK_W, n)), spec((n, CHUNK_W)), spec((n, CHUNK_W)), spec((1, CHUNK_W))]
    out_shapes = [((SLAB_W, SLAB_W), BF16), ((SLAB_W, 2 * sl), BF16), ((2 * sl, SLAB_W), BF16),
                  ((SCAN_ROWS, sl), F32), ((SCAN_ROWS, sl), F32), ((1, SLAB_W), F32)]
    return pl.pallas_call(
        _ssm_prep_kernel,
        grid=(g,),
        in_specs=in_specs,
        out_specs=[slab(s) for s, _ in out_shapes],
        out_shape=[jax.ShapeDtypeStruct((ns,) + s, d) for s, d in out_shapes],
        compiler_params=_cparams("arbitrary"),
        name="ssm_prep",
    )(*args)


def _in_proj_kernel(x_ref, g_ref, wqt_ref, wk_ref, wvt_ref, wu_ref, wg_ref,
                    cos_ref, sin_ref, cost_ref, sint_ref,
                    qt_ref, k_ref, vt_ref, u_ref, sga_ref, sgb_ref):
    x = x_ref[0]
    d = x.shape[-1]
    ms = jnp.mean(x * x, axis=-1, keepdims=True)
    h = (x * lax.rsqrt(ms + RMS_EPS) * g_ref[...]).astype(BF16)
    for c0 in range(0, 2 * d, 512):
        sg = jax.nn.sigmoid(jnp.dot(h, wg_ref[:, c0:c0 + 512], preferred_element_type=F32)).astype(BF16)
        if c0 < d:
            sga_ref[0, :, c0:c0 + 512] = sg
        else:
            sgb_ref[0, :, c0 - d:c0 - d + 512] = sg
    cos, sin = cos_ref[...], sin_ref[...]
    for c0 in range(0, d, 512):
        p = jnp.dot(h, wk_ref[:, c0:c0 + 512], preferred_element_type=F32)
        for hh in range(512 // HEAD_DIM):
            sl = p[:, hh * HEAD_DIM:(hh + 1) * HEAD_DIM]
            r = sl * cos + pltpu.roll(sl, HALF_DIM, axis=1) * sin
            k_ref[0, :, c0 + hh * HEAD_DIM:c0 + (hh + 1) * HEAD_DIM] = r.astype(BF16)
    cost, sint = cost_ref[...], sint_ref[...]
    scale = HEAD_DIM ** -0.5 * math.log2(math.e)
    for c0 in range(0, d, 256):
        pt = lax.dot_general(wqt_ref[c0:c0 + 256, :], h, _NT, preferred_element_type=F32)
        for hh in range(256 // HEAD_DIM):
            r0 = hh * HEAD_DIM
            x1 = pt[r0:r0 + HALF_DIM]
            x2 = pt[r0 + HALF_DIM:r0 + HEAD_DIM]
            qt_ref[0, c0 + r0:c0 + r0 + HALF_DIM, :] = ((x1 * cost - x2 * sint) * scale).astype(BF16)
            qt_ref[0, c0 + r0 + HALF_DIM:c0 + r0 + HEAD_DIM, :] = ((x2 * cost + x1 * sint) * scale).astype(BF16)
    ones = jnp.ones((BF16_SUBLANES, x.shape[0]), BF16)
    for c0 in range(0, d, 256):
        pv = lax.dot_general(wvt_ref[c0:c0 + 256, :], h, _NT, preferred_element_type=F32).astype(BF16)
        for hh in range(256 // HEAD_DIM):
            r0 = (c0 // HEAD_DIM + hh) * V_ROWS
            vt_ref[0, r0:r0 + HEAD_DIM, :] = pv[hh * HEAD_DIM:(hh + 1) * HEAD_DIM]
            vt_ref[0, r0 + HEAD_DIM:r0 + V_ROWS, :] = ones
    pu = jnp.dot(h, wu_ref[...], preferred_element_type=F32)
    for s in range(pu.shape[1] // LANE):
        u_ref[s, 0] = pu[:, s * LANE:(s + 1) * LANE]


def _in_proj(x, g, w_in):
    b, l, d = x.shape
    tm = IN_PROJ_TILE
    sw = w_in.shape[1] - 5 * d
    wqt = w_in[:, :d].T.astype(BF16)
    wk = w_in[:, d:2 * d].astype(BF16)
    wvt = w_in[:, 2 * d:3 * d].T.astype(BF16)
    wu = w_in[:, 3 * d:3 * d + sw].astype(BF16)
    wg = w_in[:, 3 * d + sw:].astype(BF16)
    inv_freq = ROPE_THETA ** (-jnp.arange(HALF_DIM, dtype=F32) / HALF_DIM)
    ang = jnp.arange(l).astype(F32)[:, None] * inv_freq[None, :]
    cos, sin = jnp.cos(ang), jnp.sin(ang)
    cos2 = jnp.concatenate([cos, cos], axis=1)
    sin2 = jnp.concatenate([-sin, sin], axis=1)
    tok = lambda w: pl.BlockSpec((1, tm, w), lambda bi, i: (bi, i, 0))
    feat = lambda w: pl.BlockSpec((1, w, tm), lambda bi, i: (bi, 0, i))
    sds = jax.ShapeDtypeStruct
    return pl.pallas_call(
        _in_proj_kernel,
        grid=(b, l // tm),
        in_specs=[tok(d), _const_spec((1, d)), _resident_spec((d, d)), _resident_spec((d, d)), _resident_spec((d, d)),
                  _resident_spec((d, sw)), _resident_spec((d, 2 * d)),
                  pl.BlockSpec((tm, HEAD_DIM), lambda bi, i: (i, 0)),
                  pl.BlockSpec((tm, HEAD_DIM), lambda bi, i: (i, 0)),
                  pl.BlockSpec((HALF_DIM, tm), lambda bi, i: (0, i)),
                  pl.BlockSpec((HALF_DIM, tm), lambda bi, i: (0, i))],
        out_specs=[feat(d), tok(d), feat(N_HEADS * V_ROWS),
                   pl.BlockSpec((sw // LANE, 1, tm, LANE), lambda bi, i: (0, bi, i, 0)), tok(d), tok(d)],
        out_shape=[sds((b, d, l), BF16), sds((b, l, d), BF16), sds((b, N_HEADS * V_ROWS, l), BF16),
                   sds((sw // LANE, b, l, LANE), F32), sds((b, l, d), BF16), sds((b, l, d), BF16)],
        compiler_params=_cparams("parallel", "parallel"),
        name="in_proj",
    )(x, g.reshape(1, d), wqt, wk, wvt, wu, wg, cos2, sin2, cos.T, sin.T)


def _ssm_chunk_kernel(u_ref, mx_ref, gx_ref, hx_ref, apr_ref, api_ref, d_ref,
                      y_ref, ere_ref, eim_ref, xre_ref, xim_ref):
    nk = u_ref.shape[2] // SSM_CHUNK
    sl = ere_ref.shape[1]
    uf = jnp.concatenate([u_ref[0, 0, pl.ds(t, nk, stride=SSM_CHUNK), :] for t in range(SSM_CHUNK)], axis=1)
    u = uf.astype(BF16)
    dot = functools.partial(jnp.dot, preferred_element_type=F32)
    e = dot(u, gx_ref[0])
    ere_ref[...] = e[:, :sl]
    eim_ref[...] = e[:, sl:]
    apr, api = apr_ref[0], api_ref[0]
    row = lax.broadcasted_iota(jnp.int32, apr.shape, 0)
    lvl = [(dd, jnp.where(row >= dd, jnp.broadcast_to(apr[dd - 1:dd], apr.shape), 0.0),
            jnp.where(row >= dd, jnp.broadcast_to(api[dd - 1:dd], apr.shape), 0.0)) for dd in (1, 2, 4)]

    def tile(t, carry):
        cr, ci = carry
        r0 = t * SCAN_ROWS
        zr = ere_ref[pl.ds(r0, SCAN_ROWS), :]
        zi = eim_ref[pl.ds(r0, SCAN_ROWS), :]
        for dd, adr, adi in lvl:
            sr = pltpu.roll(zr, dd, axis=0)
            si = pltpu.roll(zi, dd, axis=0)
            zr, zi = zr + (adr * sr - adi * si), zi + (adr * si + adi * sr)
        sr = apr * cr - api * ci + zr
        si = apr * ci + api * cr + zi
        xre_ref[pl.ds(r0, SCAN_ROWS), :] = jnp.where(row >= 1, pltpu.roll(sr, 1, axis=0), cr)
        xim_ref[pl.ds(r0, SCAN_ROWS), :] = jnp.where(row >= 1, pltpu.roll(si, 1, axis=0), ci)
        last = SCAN_ROWS - 1
        return (jnp.broadcast_to(sr[last:last + 1], sr.shape), jnp.broadcast_to(si[last:last + 1], si.shape))

    half = SLAB_W // 2
    ntiles = nk // SCAN_ROWS
    tiles_per_piece = ntiles // INTRA_PIECES
    piece_rows = nk // INTRA_PIECES
    carry = (jnp.zeros(apr.shape, F32),) * 2
    pieces = []
    for t in range(ntiles):
        if t % tiles_per_piece == 0:
            up = u[t // tiles_per_piece * piece_rows:(t // tiles_per_piece + 1) * piece_rows]
            pieces.append(jnp.concatenate(
                [dot(up[:, :half], mx_ref[0, :half, :half]), dot(up, mx_ref[0, :, half:])], axis=1))
        carry = tile(t, carry)
    intra = jnp.concatenate(pieces, axis=0)
    x0 = jnp.concatenate([xre_ref[...], xim_ref[...]], axis=1).astype(BF16)
    y = jax.nn.gelu(intra + dot(x0, hx_ref[0]) + d_ref[0] * uf)
    for t in range(SSM_CHUNK):
        y_ref[0, 0, pl.ds(t, nk, stride=SSM_CHUNK), :] = y[:, t * LANE:(t + 1) * LANE]


def _ssm_chunk(u4, mx, gx, hx, apx, aix, dx):
    ns, b, l, lane = u4.shape
    nk = l // SSM_CHUNK
    sl = apx.shape[-1]
    per_slab = lambda r, c: pl.BlockSpec((1, r, c), lambda s, bi: (s, 0, 0))
    rows = pl.BlockSpec((1, 1, l, lane), lambda s, bi: (s, bi, 0, 0))
    return pl.pallas_call(
        _ssm_chunk_kernel,
        grid=(ns, b),
        in_specs=[rows, per_slab(SLAB_W, SLAB_W), per_slab(SLAB_W, 2 * sl), per_slab(2 * sl, SLAB_W),
                  per_slab(SCAN_ROWS, sl), per_slab(SCAN_ROWS, sl), per_slab(1, SLAB_W)],
        out_specs=rows,
        out_shape=jax.ShapeDtypeStruct(u4.shape, F32),
        scratch_shapes=[pltpu.VMEM((nk, sl), F32) for _ in range(4)],
        compiler_params=_cparams("parallel", "parallel"),
        name="ssm_chunk",
    )(u4, mx, gx, hx, apx, aix, dx)


def _moba_kernel(qt_ref, k_ref, vt_ref, o_ref, *scratch):
    i = pl.program_id(2)
    blk = MOBA_BLOCK
    qw = Q_BLOCKS * blk
    nb = k_ref.shape[1] // blk
    fb = i * Q_BLOCKS
    heads = range(ATTN_HEADS)
    hsl = lambda hp: slice(hp * HEAD_DIM, (hp + 1) * HEAD_DIM)
    kmean_ref, sel_ref, m_ref, acc_ref, s_ref, cm_ref, ksplit_ref = (
        scratch[r * ATTN_HEADS:(r + 1) * ATTN_HEADS] for r in range(7))

    @pl.when(i == 0)
    def _():
        for hp in heads:
            for j in range(nb):
                kb = k_ref[0, j * blk:(j + 1) * blk, hsl(hp)].astype(F32)
                kmean_ref[hp][j:j + 1, :] = jnp.mean(kb, axis=0, keepdims=True)
            rest = kmean_ref[hp][...]
            for part in range(3):
                piece = rest.astype(BF16)
                ksplit_ref[hp][part * nb:(part + 1) * nb, :] = piece
                rest = rest - piece.astype(F32)

    vsl = lambda hp: slice(hp * V_ROWS, (hp + 1) * V_ROWS)

    def select_blocks(hp, qt):
        g3 = jnp.dot(ksplit_ref[hp][...], qt, preferred_element_type=F32)
        gate = g3[:nb] + g3[nb:2 * nb] + g3[2 * nb:]
        brow = lax.broadcasted_iota(jnp.int32, gate.shape, 0)
        own = fb + lax.broadcasted_iota(jnp.int32, (1, qw), 1) // blk
        work = jnp.where(brow < own, gate, -jnp.inf)
        sel = jnp.zeros(gate.shape, F32)
        for _ in range(MOBA_TOPK):
            mx = jnp.max(work, axis=0, keepdims=True)
            idx = jnp.min(jnp.where(work == mx, brow, nb), axis=0, keepdims=True)
            hit = brow == idx
            sel = jnp.where(hit, jnp.where(idx < own, 1.0, 0.0), sel)
            work = jnp.where(hit, -jnp.inf, work)
        sel_ref[hp][...] = sel

    def item_blocks(g):
        first, count = (fb, Q_BLOCKS) if g is None else (g * KV_GROUP, KV_GROUP)
        if isinstance(first, int):
            return [slice((first + jl) * blk, (first + jl + 1) * blk) for jl in range(count)]
        return [pl.ds(pl.multiple_of((first + jl) * blk, blk), blk) for jl in range(count)]

    def score_steps(hp, qt, g):
        buf = s_ref[hp]

        def step(jl, rows):
            if g is None:
                c0 = jl * blk
                s = jnp.dot(k_ref[0, rows, hsl(hp)], qt[:, c0:], preferred_element_type=F32)
                krow = lax.broadcasted_iota(jnp.int32, s.shape, 0)
                qcol = lax.broadcasted_iota(jnp.int32, s.shape, 1)
                picked = sel_ref[hp][pl.ds(fb + jl, 1), c0:] > 0.0
                s = jnp.where(qcol >= blk, jnp.where(picked, s, NEG_BIG), jnp.where(krow <= qcol, s, NEG_BIG))
                buf[jl * blk:(jl + 1) * blk, c0:] = s
                cmax = jnp.max(s, axis=0, keepdims=True)
                if c0:
                    cmax = jnp.concatenate([jnp.full((1, c0), NEG_BIG, F32), cmax], axis=1)
                cm_ref[hp][jl:jl + 1, :] = cmax
                return
            s = jnp.dot(k_ref[0, rows, hsl(hp)], qt, preferred_element_type=F32)
            buf[jl * blk:(jl + 1) * blk, :] = s
            cm_ref[hp][jl:jl + 1, :] = jnp.max(s, axis=0, keepdims=True)

        return [functools.partial(step, jl, rows) for jl, rows in enumerate(item_blocks(g))]

    def weight_steps(hp, g):
        buf = s_ref[hp]
        refs = []

        def step(jl, rows):
            if jl == 0:
                cmax = [cm_ref[hp][j:j + 1, :] for j in range(Q_BLOCKS if g is None else KV_GROUP)]
                if g is None:
                    m_new = functools.reduce(jnp.maximum, cmax)
                    refs.extend(m_new for _ in cmax)
                else:
                    picked = [jnp.logical_and(sel_ref[hp][pl.ds(g * KV_GROUP + j, 1), :] > 0.0,
                                              g * KV_GROUP + j < fb) for j in range(KV_GROUP)]
                    m_old = m_ref[hp][...]
                    m_new = m_old
                    for j in range(KV_GROUP):
                        m_new = jnp.maximum(m_new, jnp.where(picked[j], cmax[j], NEG_BIG))
                    acc_ref[hp][...] = jnp.exp2(m_old - m_new) * acc_ref[hp][...]
                    refs.extend(jnp.where(picked[j], m_new, -NEG_BIG) for j in range(KV_GROUP))
                m_ref[hp][...] = m_new
            c0 = jl * blk if g is None else 0
            p = jnp.exp2(buf[jl * blk:(jl + 1) * blk, c0:] - refs[jl][:, c0:]).astype(BF16)
            pv = jnp.dot(vt_ref[0, vsl(hp), rows], p, preferred_element_type=F32)
            if g is None and jl == 0:
                acc_ref[hp][...] = pv
            else:
                acc_ref[hp][:, c0:] = acc_ref[hp][:, c0:] + pv

        return [functools.partial(step, jl, rows) for jl, rows in enumerate(item_blocks(g))]

    def interleave(w_steps, s_steps):
        for n in range(max(len(w_steps) + SCORE_LEAD, len(s_steps))):
            if n < len(s_steps):
                s_steps[n]()
            if SCORE_LEAD <= n < len(w_steps) + SCORE_LEAD:
                w_steps[n - SCORE_LEAD]()

    qts = [qt_ref[0, hsl(hp), :] for hp in heads]
    for hp in heads:
        select_blocks(hp, qts[hp])
    interleave([], score_steps(0, qts[0], None))
    for hp in heads:
        nhp = (hp + 1) % ATTN_HEADS
        interleave(weight_steps(hp, None), score_steps(nhp, qts[nhp], None if nhp else 0))

    ngroups = (fb + KV_GROUP - 1) // KV_GROUP

    def past_group(g, _, last=False):
        for hp in heads:
            nhp = (hp + 1) % ATTN_HEADS
            ahead = [] if (last and not nhp) else score_steps(nhp, qts[nhp], g if nhp else g + 1)
            interleave(weight_steps(hp, g), ahead)
        return 0

    n_ahead = jnp.maximum(ngroups - 1, 0)

    def two_groups(gg, _):
        past_group(2 * gg, 0)
        past_group(2 * gg + 1, 0)
        return 0

    lax.fori_loop(0, n_ahead // 2, two_groups, 0)

    @pl.when(n_ahead % 2 == 1)
    def _():
        past_group(n_ahead - 1, 0)

    @pl.when(ngroups > 0)
    def _():
        past_group(ngroups - 1, 0, last=True)

    for hp in heads:
        ot = acc_ref[hp][:HEAD_DIM, :] / acc_ref[hp][HEAD_DIM:HEAD_DIM + 1, :]
        o_ref[0, :, hsl(hp)] = ot.T.astype(o_ref.dtype)


def _moba_attention(qt, k, vt):
    b, l, d = k.shape
    nb = l // MOBA_BLOCK
    assert nb % KV_GROUP == 0 and nb % Q_BLOCKS == 0
    hw = ATTN_HEADS * HEAD_DIM
    qw = Q_BLOCKS * MOBA_BLOCK
    return pl.pallas_call(
        _moba_kernel,
        grid=(b, N_HEADS // ATTN_HEADS, nb // Q_BLOCKS),
        in_specs=[pl.BlockSpec((1, hw, qw), lambda bi, h, i: (bi, h, i)),
                  pl.BlockSpec((1, l, hw), lambda bi, h, i: (bi, 0, h)),
                  pl.BlockSpec((1, ATTN_HEADS * V_ROWS, l), lambda bi, h, i: (bi, h, 0))],
        out_specs=pl.BlockSpec((1, qw, hw), lambda bi, h, i: (bi, i, h)),
        out_shape=jax.ShapeDtypeStruct((b, l, d), BF16),
        scratch_shapes=[pltpu.VMEM(shape, dtype)
                        for shape, dtype in (((nb, HEAD_DIM), F32), ((nb, qw), F32), ((1, qw), F32),
                                             ((V_ROWS, qw), F32), ((KV_GROUP * MOBA_BLOCK, qw), F32),
                                             ((KV_GROUP, qw), F32), ((3 * nb, HEAD_DIM), BF16))
                        for _ in range(ATTN_HEADS)],
        compiler_params=_cparams("parallel", "parallel", "arbitrary"),
        name="moba_attn",
    )(qt, k, vt)


def _merge_mlp_kernel(x_ref, oa_ref, sga_ref, sgb_ref, y_ref, wglu_ref, wout_ref, g1_ref, wup_ref, wdn_ref, g2_ref,
                      o_ref, *, final_norm):
    d = x_ref.shape[-1]
    ff = wup_ref.shape[1]
    y = jnp.concatenate([y_ref[s, 0] for s in range(y_ref.shape[0])], axis=1)
    hgl = jnp.dot(y.astype(BF16), wglu_ref[...], preferred_element_type=F32)
    ob = hgl[:, :d] * jax.nn.sigmoid(hgl[:, d:])
    mixed = sga_ref[0].astype(F32) * oa_ref[0].astype(F32) + sgb_ref[0].astype(F32) * ob
    x = x_ref[0] + jnp.dot(mixed.astype(BF16), wout_ref[...], preferred_element_type=F32)
    ms = jnp.mean(x * x, axis=-1, keepdims=True)
    h = (x * lax.rsqrt(ms + RMS_EPS) * g1_ref[...]).astype(BF16)
    acc = x
    for c0 in range(0, ff, d):
        a = jnp.maximum(jnp.dot(h, wup_ref[:, c0:c0 + d], preferred_element_type=F32), 0.0)
        acc = acc + jnp.dot((a * a).astype(BF16), wdn_ref[c0:c0 + d, :], preferred_element_type=F32)
    if final_norm:
        ms2 = jnp.mean(acc * acc, axis=-1, keepdims=True)
        acc = acc * lax.rsqrt(ms2 + RMS_EPS) * g2_ref[...]
    o_ref[0] = acc


def _merge_mlp(x, oa, sga, sgb, y, w_glu, w_out, g1, w_up, w_down, g2, final_norm):
    b, l, d = x.shape
    tm = TOKEN_TILE
    ns, lane = y.shape[0], y.shape[-1]
    sw = ns * lane
    ff = w_up.shape[1]
    tok = lambda w: pl.BlockSpec((1, tm, w), lambda bi, i: (bi, i, 0))
    return pl.pallas_call(
        functools.partial(_merge_mlp_kernel, final_norm=final_norm),
        grid=(b, l // tm),
        in_specs=[tok(d), tok(d), tok(d), tok(d), pl.BlockSpec((ns, 1, tm, lane), lambda bi, i: (0, bi, i, 0)),
                  _resident_spec((sw, 2 * d)), _resident_spec((d, d)), _const_spec((1, d)),
                  _resident_spec((d, ff)), _resident_spec((ff, d)), _const_spec((1, d))],
        out_specs=tok(d),
        out_shape=jax.ShapeDtypeStruct((b, l, d), F32),
        compiler_params=_cparams("parallel", "parallel"),
        name="merge_mlp",
    )(x, oa, sga, sgb, y, w_glu.astype(BF16), w_out.astype(BF16), g1.reshape(1, d),
      w_up.astype(BF16), w_down.astype(BF16), g2.reshape(1, d))


def kernel(x, norm_mix_g, w_in, lam_re, lam_im, log_step, b_re, b_im, c_re, c_im, d_skip, w_glu, w_out,
           norm_mlp_g, w_up, w_down, norm_final_g):
    b, l, d = x.shape
    depth = w_in.shape[0]
    assert d == N_HEADS * HEAD_DIM and l % TOKEN_TILE == 0 and l % IN_PROJ_TILE == 0
    assert l % (SSM_CHUNK * SCAN_ROWS) == 0
    for i in range(depth):
        slab_mats = _ssm_prep(lam_re[i], lam_im[i], log_step[i], b_re[i], b_im[i], c_re[i], c_im[i], d_skip[i])
        qt, k, vt, u, sga, sgb = _in_proj(x, norm_mix_g[i], w_in[i])
        y = _ssm_chunk(u, *slab_mats)
        oa = _moba_attention(qt, k, vt)
        x = _merge_mlp(x, oa, sga, sgb, y, w_glu[i], w_out[i], norm_mlp_g[i], w_up[i], w_down[i], norm_final_g,
                       final_norm=(i == depth - 1))
    return x
```
